```python
import jax, jax.numpy as jnp
from jax import lax
import numpy as np

D_MODEL = 1024
BATCH = 16
SEQ = 2048
DEPTH = 4

CHUNK = 64
Q_BLOCK = 128
N_MEM = 256
FOX_HEADS = 8
FOX_HEAD_DIM = 64
MLA_HEADS = 8
MLA_NOPE_DIM = 64
MLA_ROPE_DIM = 32
MLA_V_DIM = 64
MLA_Q_RANK = 384
MLA_KV_RANK = 256
ROPE_BASE = 10000.0
MEM_HEADS = 4
MEM_HEAD_DIM = 128
N_BRANCHES = 3
BRANCH_WIDTH = 512
D_FF = 4 * D_MODEL
DEEPNORM_ALPHA = (2 * DEPTH) ** 0.25
DEEPNORM_BETA = (8 * DEPTH) ** -0.25
LN_EPS = 1e-5
RMS_EPS = 1e-6
NEG_INF = -1e30

IN_SPLITS = (
    3 * FOX_HEADS * FOX_HEAD_DIM,
    FOX_HEADS,
    MLA_Q_RANK,
    MLA_KV_RANK,
    MLA_ROPE_DIM,
    MEM_HEADS * MEM_HEAD_DIM,
    N_BRANCHES * D_MODEL,
)
D_IN = sum(IN_SPLITS)
SPLIT_POINTS = tuple(int(p) for p in np.cumsum(IN_SPLITS)[:-1])

kernel_name = 'hybrid_fox_mla_memory_deepnorm_trunk'


def _layer_norm(x, g, b):
    xf = x.astype(jnp.float32)
    mu = jnp.mean(xf, axis=-1, keepdims=True)
    var = jnp.mean(jnp.square(xf - mu), axis=-1, keepdims=True)
    y = (xf - mu) * lax.rsqrt(var + LN_EPS)
    return (y * g.astype(jnp.float32) + b.astype(jnp.float32)).astype(x.dtype)


def _rms_norm(x, g):
    xf = x.astype(jnp.float32)
    y = xf * lax.rsqrt(jnp.mean(jnp.square(xf), axis=-1, keepdims=True) + RMS_EPS)
    return (y * g.astype(jnp.float32)).astype(x.dtype)


def _rope_tables(positions):
    inv_freq = ROPE_BASE ** (-jnp.arange(0, MLA_ROPE_DIM, 2, dtype=jnp.float32) / MLA_ROPE_DIM)
    ang = positions.astype(jnp.float32)[..., None] * inv_freq
    return jnp.cos(ang), jnp.sin(ang)


def _rope(x, cos, sin):
    half = x.shape[-1] // 2
    x1 = x[..., :half].astype(jnp.float32)
    x2 = x[..., half:].astype(jnp.float32)
    return jnp.concatenate([x1 * cos - x2 * sin, x2 * cos + x1 * sin], axis=-1).astype(x.dtype)


def _swept_attention(q, k, v, bias_fn):
    seq = q.shape[2]
    scale = q.shape[-1] ** -0.5
    outs = []
    for i in range(seq // Q_BLOCK):
        q0 = i * Q_BLOCK
        k_len = q0 + Q_BLOCK
        logits = jnp.einsum('bhqd,bhkd->bhqk', q[:, :, q0:k_len], k[:, :, :k_len]).astype(jnp.float32)
        logits = logits * scale + bias_fn(q0, k_len)
        p = jax.nn.softmax(logits, axis=-1).astype(v.dtype)
        outs.append(jnp.einsum('bhqk,bhkd->bhqd', p, v[:, :, :k_len]))
    return jnp.concatenate(outs, axis=2)


def _chunk_causal_bias(q0, k_len):
    t_chunk = (q0 + jnp.arange(Q_BLOCK)) // CHUNK
    s_chunk = jnp.arange(k_len) // CHUNK
    allowed = s_chunk[None, :] <= t_chunk[:, None]
    return jnp.where(allowed, jnp.float32(0.0), jnp.float32(NEG_INF))[None, None]


def _mixer(h, mem, cos, sin, w_in, b_forget, w_uq, g_cq, w_ukv, g_ckv, w_mem_kv, w_br, w_out):
    B, S, _ = h.shape
    proj = h @ w_in
    fox_qkv, f_logit, c_q, c_kv, k_rope, q_mem, gate_logit = jnp.split(proj, SPLIT_POINTS, axis=-1)

    qkv = fox_qkv.reshape(B, S, 3, FOX_HEADS, FOX_HEAD_DIM).transpose(2, 0, 3, 1, 4)
    log_f = jax.nn.log_sigmoid(f_logit.astype(jnp.float32) + b_forget.astype(jnp.float32))
    cum_f = jnp.cumsum(log_f, axis=1).transpose(0, 2, 1)

    def fox_bias(q0, k_len):
        t = q0 + jnp.arange(Q_BLOCK)
        s = jnp.arange(k_len)
        decay = cum_f[:, :, q0:q0 + Q_BLOCK, None] - cum_f[:, :, None, :k_len]
        return jnp.where(s[None, :] <= t[:, None], decay, jnp.float32(NEG_INF))

    o_a = _swept_attention(qkv[0], qkv[1], qkv[2], fox_bias)
    o_a = o_a.transpose(0, 2, 1, 3).reshape(B, S, BRANCH_WIDTH)

    q_b = (_rms_norm(c_q, g_cq) @ w_uq).reshape(B, S, MLA_HEADS, MLA_NOPE_DIM + MLA_ROPE_DIM)
    q_b = q_b.transpose(0, 2, 1, 3)
    q_nope, q_pe = q_b[..., :MLA_NOPE_DIM], q_b[..., MLA_NOPE_DIM:]
    q_pe = _rope(q_pe, cos[:, None], sin[:, None])
    kv_b = (_rms_norm(c_kv, g_ckv) @ w_ukv).reshape(B, S, MLA_HEADS, MLA_NOPE_DIM + MLA_V_DIM)
    kv_b = kv_b.transpose(0, 2, 1, 3)
    k_nope, v_b = kv_b[..., :MLA_NOPE_DIM], kv_b[..., MLA_NOPE_DIM:]
    k_pe = _rope(k_rope, cos, sin)[:, None]
    q_full = jnp.concatenate([q_nope, q_pe], axis=-1)
    k_full = jnp.concatenate([k_nope, jnp.broadcast_to(k_pe, (B, MLA_HEADS, S, MLA_ROPE_DIM))], axis=-1)
    o_b = _swept_attention(q_full, k_full, v_b, _chunk_causal_bias)
    o_b = o_b.transpose(0, 2, 1, 3).reshape(B, S, BRANCH_WIDTH)

    mkv = (mem @ w_mem_kv).reshape(B, mem.shape[1], 2, MEM_HEADS, MEM_HEAD_DIM).transpose(2, 0, 3, 1, 4)
    qm = q_mem.reshape(B, S, MEM_HEADS, MEM_HEAD_DIM).transpose(0, 2, 1, 3)
    logits_m = jnp.einsum('bhqd,bhmd->bhqm', qm, mkv[0]).astype(jnp.float32) * (MEM_HEAD_DIM ** -0.5)
    p_m = jax.nn.softmax(logits_m, axis=-1).astype(mkv.dtype)
    o_c = jnp.einsum('bhqm,bhmd->bhqd', p_m, mkv[1]).transpose(0, 2, 1, 3).reshape(B, S, BRANCH_WIDTH)

    branches = jnp.stack([o_a, o_b, o_c], axis=2)
    branch_proj = jnp.einsum('bsnc,ncd->bsnd', branches, w_br)
    gates = jax.nn.sigmoid(gate_logit.reshape(B, S, N_BRANCHES, D_MODEL))
    merged = jnp.sum(gates * branch_proj, axis=2)
    return merged @ w_out


def setup_inputs(seed: int = 0) -> dict:
    key = jax.random.key(seed)
    ks = jax.random.split(key, 24)
    L = DEPTH

    def w(k, shape, fan_in, scale=1.0):
        return jax.random.normal(k, shape, jnp.float32) * (fan_in ** -0.5) * scale

    def gain(k, shape):
        return 1.0 + 0.02 * jax.random.normal(k, shape, jnp.float32)

    def bias(k, shape):
        return 0.02 * jax.random.normal(k, shape, jnp.float32)

    x = jax.random.normal(ks[0], (BATCH, SEQ, D_MODEL), jnp.float32)
    mem = jax.random.normal(ks[1], (BATCH, N_MEM, D_MODEL), jnp.float32)
    offsets = jax.random.randint(ks[2], (BATCH, 1), 0, 8192, dtype=jnp.int32)
    positions = offsets + jnp.arange(SEQ, dtype=jnp.int32)[None, :]
    return {
        'x': x,
        'mem': mem,
        'positions': positions,
        'ln_in_g': gain(ks[3], (D_MODEL,)),
        'ln_in_b': bias(ks[4], (D_MODEL,)),
        'w_in': w(ks[5], (L, D_MODEL, D_IN), D_MODEL),
        'b_forget': jax.random.uniform(ks[6], (L, FOX_HEADS), jnp.float32, 1.0, 6.0),
        'w_uq': w(ks[7], (L, MLA_Q_RANK, MLA_HEADS * (MLA_NOPE_DIM + MLA_ROPE_DIM)), MLA_Q_RANK),
        'g_cq': gain(ks[8], (L, MLA_Q_RANK)),
        'w_ukv': w(ks[9], (L, MLA_KV_RANK, MLA_HEADS * (MLA_NOPE_DIM + MLA_V_DIM)), MLA_KV_RANK),
        'g_ckv': gain(ks[10], (L, MLA_KV_RANK)),
        'w_mem_kv': w(ks[11], (L, D_MODEL, 2 * MEM_HEADS * MEM_HEAD_DIM), D_MODEL),
        'w_br': w(ks[12], (L, N_BRANCHES, BRANCH_WIDTH, D_MODEL), BRANCH_WIDTH, DEEPNORM_BETA),
        'w_out': w(ks[13], (L, D_MODEL, D_MODEL), D_MODEL, DEEPNORM_BETA),
        'ln1_g': gain(ks[14], (L, D_MODEL)),
        'ln1_b': bias(ks[15], (L, D_MODEL)),
        'w_ff1': w(ks[16], (L, D_MODEL, D_FF), D_MODEL),
        'w_ff2': w(ks[17], (L, D_FF, D_MODEL), D_FF, DEEPNORM_BETA),
        'ln2_g': gain(ks[18], (L, D_MODEL)),
        'ln2_b': bias(ks[19], (L, D_MODEL)),
    }


def reference(x, mem, positions, ln_in_g, ln_in_b, w_in, b_forget, w_uq, g_cq, w_ukv, g_ckv,
              w_mem_kv, w_br, w_out, ln1_g, ln1_b, w_ff1, w_ff2, ln2_g, ln2_b):
    cos, sin = _rope_tables(positions)
    h = _layer_norm(x, ln_in_g, ln_in_b)
    for l in range(DEPTH):
        y = _mixer(h, mem, cos, sin, w_in[l], b_forget[l], w_uq[l], g_cq[l], w_ukv[l], g_ckv[l],
                   w_mem_kv[l], w_br[l], w_out[l])
        h = _layer_norm(DEEPNORM_ALPHA * h + y, ln1_g[l], ln1_b[l])
        ff = jnp.square(jax.nn.relu(h @ w_ff1[l])) @ w_ff2[l]
        h = _layer_norm(DEEPNORM_ALPHA * h + ff, ln2_g[l], ln2_b[l])
    return h
```

```python
import functools

import numpy as np
import jax
import jax.numpy as jnp
from jax import lax
from jax.experimental import pallas as pl
from jax.experimental.pallas import tpu as pltpu

D_MODEL = 1024
CHUNK = 64
FOX_HEADS = 8
FOX_HEAD_DIM = 64
MLA_HEADS = 8
MLA_NOPE_DIM = 64
MLA_ROPE_DIM = 32
MLA_V_DIM = 64
MLA_Q_RANK = 384
MLA_KV_RANK = 256
ROPE_BASE = 10000.0
MEM_HEADS = 4
MEM_HEAD_DIM = 128
N_BRANCHES = 3
BRANCH_WIDTH = 512
D_FF = 4 * D_MODEL
LN_EPS = 1e-5
RMS_EPS = 1e-6
NEG_INF = -1e30

LANES = 128
HALF = 64
N_GROUPS = 8
VMEM_LIMIT = 56 * 1024 * 1024

ROW_TILE = 256
ATTN_TILE = 256

F32 = jnp.float32
BF16 = jnp.bfloat16


def _params(n_axes):
    return pltpu.CompilerParams(
        dimension_semantics=("arbitrary",) * n_axes, vmem_limit_bytes=VMEM_LIMIT)


def _dot(a, b):
    return jnp.dot(a, b, preferred_element_type=F32)


def _dot_nt(a, b):
    return lax.dot_general(a, b, (((1,), (1,)), ((), ())), preferred_element_type=F32)


def _layer_norm(z, g, b):
    mu = jnp.mean(z, axis=-1, keepdims=True)
    d = z - mu
    var = jnp.mean(d * d, axis=-1, keepdims=True)
    return d * lax.rsqrt(var + LN_EPS) * g + b


def _rms_norm(z, g):
    return z * lax.rsqrt(jnp.mean(z * z, axis=-1, keepdims=True) + RMS_EPS) * g


def _row_spec(tile, width, col_block=0):
    return pl.BlockSpec((tile, width), lambda i: (i, col_block))


def _full_spec(shape):
    return pl.BlockSpec(shape, lambda *_: (0,) * len(shape))


def _ln_kernel(x_ref, g_ref, b_ref, o_ref):
    o_ref[...] = _layer_norm(x_ref[...], g_ref[...], b_ref[...])


def _entry_norm(x2, g, b):
    t = x2.shape[0]
    return pl.pallas_call(
        _ln_kernel,
        grid=(t // ROW_TILE,),
        in_specs=[_row_spec(ROW_TILE, D_MODEL), _full_spec((1, D_MODEL)), _full_spec((1, D_MODEL))],
        out_specs=_row_spec(ROW_TILE, D_MODEL),
        out_shape=jax.ShapeDtypeStruct((t, D_MODEL), F32),
        compiler_params=_params(1),
        name="entry_norm",
    )(x2, g.reshape(1, -1), b.reshape(1, -1))


def _rope_table_kernel(pos_ref, invf_ref, c_ref, s_ref):
    ang = pos_ref[...].astype(F32) * invf_ref[...]
    lane = lax.broadcasted_iota(jnp.int32, ang.shape, 1)
    rot = (lane >= MLA_NOPE_DIM) & (lane < MLA_NOPE_DIM + MLA_ROPE_DIM)
    c_ref[...] = jnp.where(lane < MLA_NOPE_DIM, 1.0, jnp.where(rot, jnp.cos(ang), 0.0))
    s_ref[...] = jnp.where(rot, jnp.sin(ang), 0.0)


def _rope_tables(positions):
    t = positions.size
    half = MLA_ROPE_DIM // 2
    inv_freq = ROPE_BASE ** (-jnp.arange(0, MLA_ROPE_DIM, 2, dtype=F32) / MLA_ROPE_DIM)
    invf = jnp.zeros((1, LANES), F32).at[0, MLA_NOPE_DIM:MLA_NOPE_DIM + MLA_ROPE_DIM].set(
        jnp.concatenate([inv_freq, inv_freq]))
    del half
    return pl.pallas_call(
        _rope_table_kernel,
        grid=(t // ROW_TILE,),
        in_specs=[_row_spec(ROW_TILE, 1), _full_spec((1, LANES))],
        out_specs=[_row_spec(ROW_TILE, LANES), _row_spec(ROW_TILE, LANES)],
        out_shape=[jax.ShapeDtypeStruct((t, LANES), F32)] * 2,
        compiler_params=_params(1),
        name="rope_tables",
    )(positions.reshape(t, 1), invf)


GATE_W = N_BRANCHES * D_MODEL
FOX_W = 3 * FOX_HEADS * FOX_HEAD_DIM
QMEM_W = MEM_HEADS * MEM_HEAD_DIM
MAIN_W = GATE_W + FOX_W + QMEM_W + MLA_Q_RANK + MLA_KV_RANK
SMALL_W = 3 * LANES
DOT_CHUNK = 512


def _inproj_kernel(h_ref, wm_ref, ws_ref, gates_ref, fqkv_ref, qmem_ref, cq_ref, ckv_ref, small_ref):
    xb = h_ref[...].astype(BF16)

    def mm(c0, n):
        return _dot(xb, wm_ref[:, c0:c0 + n])

    for c in range(0, GATE_W, DOT_CHUNK):
        g = mm(c, DOT_CHUNK)
        gates_ref[:, c:c + DOT_CHUNK] = (1.0 / (1.0 + jnp.exp(-g))).astype(BF16)
    for c in range(0, FOX_W, DOT_CHUNK):
        fqkv_ref[:, c:c + DOT_CHUNK] = mm(GATE_W + c, DOT_CHUNK).astype(BF16)
    off = GATE_W + FOX_W
    qmem_ref[...] = mm(off, QMEM_W).astype(BF16)
    off += QMEM_W
    cq_ref[...] = mm(off, MLA_Q_RANK).astype(BF16)
    off += MLA_Q_RANK
    ckv_ref[...] = mm(off, MLA_KV_RANK).astype(BF16)
    small_ref[...] = _dot(xb, ws_ref[...])


def _in_projection(h, w_main, w_small):
    t = h.shape[0]
    widths = (GATE_W, FOX_W, QMEM_W, MLA_Q_RANK, MLA_KV_RANK)
    return pl.pallas_call(
        _inproj_kernel,
        grid=(t // ROW_TILE,),
        in_specs=[_row_spec(ROW_TILE, D_MODEL), _full_spec((D_MODEL, MAIN_W)),
                  _full_spec((D_MODEL, SMALL_W))],
        out_specs=[_row_spec(ROW_TILE, w) for w in widths] + [_row_spec(ROW_TILE, SMALL_W)],
        out_shape=[jax.ShapeDtypeStruct((t, w), BF16) for w in widths]
        + [jax.ShapeDtypeStruct((t, SMALL_W), F32)],
        compiler_params=_params(1),
        name="in_projection",
    )(h, w_main, w_small)


N_SPLIT = 3


def _split3(x):
    hi = x.astype(BF16)
    r = x - hi.astype(F32)
    mid = r.astype(BF16)
    lo = (r - mid.astype(F32)).astype(BF16)
    return hi, mid, lo


def _fox_prep_kernel(fq_ref, fk_ref, f_ref, bf_ref, pq_ref, pk_ref, oq_ref, ok_ref,
                     qa_ref, ka_ref, carry_ref):
    tm = fq_ref.shape[0]

    @pl.when(pl.program_id(1) == 0)
    def _():
        carry_ref[...] = jnp.zeros_like(carry_ref)

    x = f_ref[...] + bf_ref[...]
    logf = jnp.minimum(x, 0.0) - jnp.log1p(jnp.exp(-jnp.abs(x)))
    row = lax.broadcasted_iota(jnp.int32, (tm, tm), 0)
    col = lax.broadcasted_iota(jnp.int32, (tm, tm), 1)
    tri = jnp.where(col <= row, 1.0, 0.0).astype(BF16)
    hi, mid, lo = _split3(logf)
    cum = _dot(tri, hi) + _dot(tri, mid) + _dot(tri, lo) + carry_ref[...]
    carry_ref[...] = cum[tm - 1:tm, :]

    lane = lax.broadcasted_iota(jnp.int32, (tm, LANES), 1)
    chi, cmid, clo = _split3(cum)
    zero = jnp.zeros_like(chi)
    pieces = jnp.where(lane < FOX_HEADS, chi,
                       jnp.where(lane < 2 * FOX_HEADS, cmid,
                                 jnp.where(lane < 3 * FOX_HEADS, clo, zero)))
    q_spare = _dot(pieces, pq_ref[...]) + oq_ref[...]
    k_spare = _dot(pieces, pk_ref[...]) + ok_ref[...]

    left = lane < HALF
    for h in range(FOX_HEADS):
        pair = slice((h // 2) * LANES, (h // 2 + 1) * LANES)
        grp = slice(h * LANES, (h + 1) * LANES)
        own = left if h % 2 == 0 else jnp.logical_not(left)
        qp = fq_ref[:, pair].astype(F32) * (FOX_HEAD_DIM ** -0.5)
        kp = fk_ref[:, pair].astype(F32)
        qa_ref[:, grp] = jnp.where(own, qp, q_spare[:, grp]).astype(BF16)
        ka_ref[:, grp] = jnp.where(own, kp, k_spare[:, grp]).astype(BF16)


def _fox_placement():
    pq = np.zeros((LANES, N_GROUPS * LANES), np.float32)
    pk = np.zeros((LANES, N_GROUPS * LANES), np.float32)
    oq = np.zeros((1, N_GROUPS * LANES), np.float32)
    ok = np.zeros((1, N_GROUPS * LANES), np.float32)
    for h in range(FOX_HEADS):
        base = h * LANES + (HALF if h % 2 == 0 else 0)
        for j in range(N_SPLIT):
            pq[j * FOX_HEADS + h, base + j] = 1.0
            ok[0, base + j] = 1.0
            oq[0, base + N_SPLIT + j] = 1.0
            pk[j * FOX_HEADS + h, base + N_SPLIT + j] = -1.0
    return (jnp.asarray(pq, BF16), jnp.asarray(pk, BF16), jnp.asarray(oq), jnp.asarray(ok))


def _fox_prep(fqkv, small, b_forget3, batch, seq):
    t = fqkv.shape[0]
    nt = seq // ROW_TILE
    pq, pk, oq, ok = _fox_placement()
    wide = N_GROUPS * LANES

    def rows(width, col_block):
        return pl.BlockSpec((ROW_TILE, width), lambda b, i: (b * nt + i, col_block))

    return pl.pallas_call(
        _fox_prep_kernel,
        grid=(batch, nt),
        in_specs=[rows(BRANCH_WIDTH, 0), rows(BRANCH_WIDTH, 1), rows(LANES, 2),
                  _full_spec((1, LANES)), _full_spec((LANES, wide)), _full_spec((LANES, wide)),
                  _full_spec((1, wide)), _full_spec((1, wide))],
        out_specs=[rows(wide, 0), rows(wide, 0)],
        out_shape=[jax.ShapeDtypeStruct((t, wide), BF16)] * 2,
        scratch_shapes=[pltpu.VMEM((1, LANES), F32)],
        compiler_params=_params(2),
        name="fox_prep",
    )(fqkv, fqkv, small, b_forget3, pq, pk, oq, ok)


def _mla_prep_kernel(cq_ref, ckv_ref, kr_ref, ct_ref, st_ref, gq_ref, gkv_ref,
                     wq_ref, wk_ref, wv_ref, qa_ref, ka_ref, v_ref):
    ct = ct_ref[...]
    st = st_ref[...]
    wide = N_GROUPS * LANES
    scale = (MLA_NOPE_DIM + MLA_ROPE_DIM) ** -0.5
    xq = _rms_norm(cq_ref[...].astype(F32), gq_ref[...]).astype(BF16)
    xkv = _rms_norm(ckv_ref[...].astype(F32), gkv_ref[...]).astype(BF16)
    kpe = kr_ref[:, :LANES] * ct + kr_ref[:, LANES:] * st
    for h in range(MLA_HEADS):
        grp = slice(h * LANES, (h + 1) * LANES)
        rot = slice(wide + h * LANES, wide + (h + 1) * LANES)
        q = _dot(xq, wq_ref[:, grp]) * ct + _dot(xq, wq_ref[:, rot]) * st
        qa_ref[:, grp] = (q * scale).astype(BF16)
        ka_ref[:, grp] = (_dot(xkv, wk_ref[:, grp]) + kpe).astype(BF16)
    v_ref[...] = _dot(xkv, wv_ref[...]).astype(BF16)


def _mla_prep(cq, ckv, small, ctab, stab, g_cq, g_ckv, wq, wk, wv):
    t = cq.shape[0]
    wide = N_GROUPS * LANES
    return pl.pallas_call(
        _mla_prep_kernel,
        grid=(t // ROW_TILE,),
        in_specs=[_row_spec(ROW_TILE, MLA_Q_RANK), _row_spec(ROW_TILE, MLA_KV_RANK),
                  _row_spec(ROW_TILE, 2 * LANES), _row_spec(ROW_TILE, LANES),
                  _row_spec(ROW_TILE, LANES), _full_spec((1, MLA_Q_RANK)),
                  _full_spec((1, MLA_KV_RANK)), _full_spec((MLA_Q_RANK, 2 * wide)),
                  _full_spec((MLA_KV_RANK, wide)), _full_spec((MLA_KV_RANK, BRANCH_WIDTH))],
        out_specs=[_row_spec(ROW_TILE, wide), _row_spec(ROW_TILE, wide),
                   _row_spec(ROW_TILE, BRANCH_WIDTH)],
        out_shape=[jax.ShapeDtypeStruct((t, wide), BF16), jax.ShapeDtypeStruct((t, wide), BF16),
                   jax.ShapeDtypeStruct((t, BRANCH_WIDTH), BF16)],
        compiler_params=_params(1),
        name="mla_prep",
    )(cq, ckv, small, ctab, stab, g_cq.reshape(1, -1), g_ckv.reshape(1, -1), wq, wk, wv)


def _attn_kernel(q_ref, k_ref, v_ref, o_ref, *, granule_shift):
    tq = q_ref.shape[1]
    tk = tq
    qi = pl.program_id(1)
    row = lax.broadcasted_iota(jnp.int32, (tq, tk), 0)
    col = lax.broadcasted_iota(jnp.int32, (tq, tk), 1)
    diag_ok = (col >> granule_shift) <= (row >> granule_shift)
    v_left = lax.broadcasted_iota(jnp.int32, (tk, LANES), 1) < HALF
    o_left = lax.broadcasted_iota(jnp.int32, (tq, LANES), 1) < HALF

    for p in range(N_GROUPS // 2):
        g0 = slice(2 * p * LANES, (2 * p + 1) * LANES)
        g1 = slice((2 * p + 1) * LANES, (2 * p + 2) * LANES)
        pair = slice(p * LANES, (p + 1) * LANES)
        q0 = q_ref[0, :, g0]
        q1 = q_ref[0, :, g1]

        def softmax_step(s, m, l):
            m_new = jnp.maximum(m, jnp.max(s, axis=1, keepdims=True))
            alpha = jnp.exp(m - m_new)
            e = jnp.exp(s - m_new)
            return e.astype(BF16), m_new, alpha * l + jnp.sum(e, axis=1, keepdims=True), alpha

        def tile(j, carry, masked):
            m0, l0, m1, l1, acc = carry
            ks = pl.multiple_of(j * tk, tk)
            s0 = _dot_nt(q0, k_ref[0, pl.ds(ks, tk), g0])
            s1 = _dot_nt(q1, k_ref[0, pl.ds(ks, tk), g1])
            if masked:
                s0 = jnp.where(diag_ok, s0, NEG_INF)
                s1 = jnp.where(diag_ok, s1, NEG_INF)
            e0, m0, l0, a0 = softmax_step(s0, m0, l0)
            e1, m1, l1, a1 = softmax_step(s1, m1, l1)
            vp = v_ref[0, pl.ds(ks, tk), pair]
            zero = jnp.zeros_like(vp)
            v_cat = jnp.concatenate([jnp.where(v_left, vp, zero), jnp.where(v_left, zero, vp)], axis=0)
            pv = _dot(jnp.concatenate([e0, e1], axis=1), v_cat)
            acc = jnp.where(o_left, a0, a1) * acc + pv
            return m0, l0, m1, l1, acc

        stat = jnp.full((tq, 1), NEG_INF, F32)
        zsum = jnp.zeros((tq, 1), F32)
        carry = (stat, zsum, stat, zsum, jnp.zeros((tq, LANES), F32))
        carry = lax.fori_loop(0, qi, lambda j, c: tile(j, c, False), carry)
        _, l0, _, l1, acc = tile(qi, carry, True)
        o_ref[0, :, pair] = (acc * jnp.where(o_left, 1.0 / l0, 1.0 / l1)).astype(BF16)


def _causal_attention(q_aug, k_aug, v, v_col_block, batch, seq, granule):
    wide = N_GROUPS * LANES
    q3 = q_aug.reshape(batch, seq, wide)
    k3 = k_aug.reshape(batch, seq, wide)
    v3 = v.reshape(batch, seq, -1)
    out = pl.pallas_call(
        functools.partial(_attn_kernel, granule_shift=granule.bit_length() - 1),
        grid=(batch, seq // ATTN_TILE),
        in_specs=[pl.BlockSpec((1, ATTN_TILE, wide), lambda b, i: (b, i, 0)),
                  pl.BlockSpec((1, seq, wide), lambda b, i: (b, 0, 0)),
                  pl.BlockSpec((1, seq, BRANCH_WIDTH), lambda b, i: (b, 0, v_col_block))],
        out_specs=pl.BlockSpec((1, ATTN_TILE, BRANCH_WIDTH), lambda b, i: (b, i, 0)),
        out_shape=jax.ShapeDtypeStruct((batch, seq, BRANCH_WIDTH), BF16),
        compiler_params=_params(2),
        name="causal_attention_g%d" % granule,
    )(q3, k3, v3)
    return out.reshape(batch * seq, BRANCH_WIDTH)


def _mem_kv_kernel(mem_ref, w_ref, k_ref, v_ref):
    kv = _dot(mem_ref[0].astype(BF16), w_ref[0])
    k_ref[0, 0] = (kv[:, :BRANCH_WIDTH] * (MEM_HEAD_DIM ** -0.5)).astype(BF16)
    v_ref[0, 0] = kv[:, BRANCH_WIDTH:].astype(BF16)


def _mem_kv(mem, w_mem_kv):
    depth = w_mem_kv.shape[0]
    batch, n_mem, _ = mem.shape
    return pl.pallas_call(
        _mem_kv_kernel,
        grid=(depth, batch),
        in_specs=[pl.BlockSpec((1, n_mem, D_MODEL), lambda l, b: (b, 0, 0)),
                  pl.BlockSpec((1, D_MODEL, 2 * BRANCH_WIDTH), lambda l, b: (l, 0, 0))],
        out_specs=[pl.BlockSpec((1, 1, n_mem, BRANCH_WIDTH), lambda l, b: (l, b, 0, 0))] * 2,
        out_shape=[jax.ShapeDtypeStruct((depth, batch, n_mem, BRANCH_WIDTH), BF16)] * 2,
        compiler_params=_params(2),
        name="mem_kv",
    )(mem, w_mem_kv)


def _mem_attn_kernel(q_ref, k_ref, v_ref, o_ref):
    for h in range(MEM_HEADS):
        grp = slice(h * MEM_HEAD_DIM, (h + 1) * MEM_HEAD_DIM)
        s = _dot_nt(q_ref[0, :, grp], k_ref[0, 0, :, grp])
        e = jnp.exp(s - jnp.max(s, axis=1, keepdims=True))
        o = _dot(e.astype(BF16), v_ref[0, 0, :, grp])
        o_ref[0, :, grp] = (o * (1.0 / jnp.sum(e, axis=1, keepdims=True))).astype(BF16)


def _mem_attention(qmem, mk, mv, layer, batch, seq):
    n_mem = mk.shape[2]
    q3 = qmem.reshape(batch, seq, BRANCH_WIDTH)
    kv_spec = pl.BlockSpec((1, 1, n_mem, BRANCH_WIDTH), lambda b, i: (layer, b, 0, 0))
    out = pl.pallas_call(
        _mem_attn_kernel,
        grid=(batch, seq // ROW_TILE),
        in_specs=[pl.BlockSpec((1, ROW_TILE, BRANCH_WIDTH), lambda b, i: (b, i, 0)), kv_spec, kv_spec],
        out_specs=pl.BlockSpec((1, ROW_TILE, BRANCH_WIDTH), lambda b, i: (b, i, 0)),
        out_shape=jax.ShapeDtypeStruct((batch, seq, BRANCH_WIDTH), BF16),
        compiler_params=_params(2),
        name="mem_attention",
    )(q3, mk, mv)
    return out.reshape(batch * seq, BRANCH_WIDTH)


def _merge_kernel(oa_ref, ob_ref, oc_ref, gates_ref, h_ref, wbr_ref, wout_ref, g_ref, b_ref,
                  o_ref, *, alpha):
    merged = None
    for n, br_ref in enumerate((oa_ref, ob_ref, oc_ref)):
        gate = gates_ref[:, n * D_MODEL:(n + 1) * D_MODEL].astype(F32)
        term = gate * _dot(br_ref[...], wbr_ref[n])
        merged = term if merged is None else merged + term
    y = _dot(merged.astype(BF16), wout_ref[...])
    o_ref[...] = _layer_norm(alpha * h_ref[...] + y, g_ref[...], b_ref[...])


def _merge(oa, ob, oc, gates, h, w_br, w_out, g, b, alpha):
    t = h.shape[0]
    br = _row_spec(ROW_TILE, BRANCH_WIDTH)
    return pl.pallas_call(
        functools.partial(_merge_kernel, alpha=alpha),
        grid=(t // ROW_TILE,),
        in_specs=[br, br, br, _row_spec(ROW_TILE, GATE_W), _row_spec(ROW_TILE, D_MODEL),
                  _full_spec((N_BRANCHES, BRANCH_WIDTH, D_MODEL)), _full_spec((D_MODEL, D_MODEL)),
                  _full_spec((1, D_MODEL)), _full_spec((1, D_MODEL))],
        out_specs=_row_spec(ROW_TILE, D_MODEL),
        out_shape=jax.ShapeDtypeStruct((t, D_MODEL), F32),
        compiler_params=_params(1),
        name="merge",
    )(oa, ob, oc, gates, h, w_br, w_out, g.reshape(1, -1), b.reshape(1, -1))


FF_CHUNK = 1024


def _ffn_kernel(h_ref, w1_ref, w2_ref, g_ref, b_ref, o_ref, *, alpha):
    h = h_ref[...]
    xb = h.astype(BF16)
    acc = alpha * h
    for c in range(0, D_FF, FF_CHUNK):
        u = jnp.maximum(_dot(xb, w1_ref[:, c:c + FF_CHUNK]), 0.0)
        acc = acc + _dot((u * u).astype(BF16), w2_ref[c:c + FF_CHUNK, :])
    o_ref[...] = _layer_norm(acc, g_ref[...], b_ref[...])


def _ffn(h, w1, w2, g, b, alpha):
    t = h.shape[0]
    return pl.pallas_call(
        functools.partial(_ffn_kernel, alpha=alpha),
        grid=(t // ROW_TILE,),
        in_specs=[_row_spec(ROW_TILE, D_MODEL), _full_spec((D_MODEL, D_FF)),
                  _full_spec((D_FF, D_MODEL)), _full_spec((1, D_MODEL)), _full_spec((1, D_MODEL))],
        out_specs=_row_spec(ROW_TILE, D_MODEL),
        out_shape=jax.ShapeDtypeStruct((t, D_MODEL), F32),
        compiler_params=_params(1),
        name="ffn",
    )(h, w1, w2, g.reshape(1, -1), b.reshape(1, -1))


def _rotate_half_cols(w):
    half = w.shape[-1] // 2
    return jnp.concatenate([-w[..., half:], w[..., :half]], axis=-1)


def _layer_weights(w_in, b_forget, w_uq, w_ukv):
    depth = w_in.shape[0]
    splits = np.cumsum([FOX_W, FOX_HEADS, MLA_Q_RANK, MLA_KV_RANK, MLA_ROPE_DIM, QMEM_W])
    fox, f_log, c_q, c_kv, k_rope, q_mem, gate = jnp.split(w_in, [int(s) for s in splits], axis=-1)
    w_main = jnp.concatenate([gate, fox, q_mem, c_q, c_kv], axis=-1).astype(BF16)

    def rope_group(w):
        return jnp.pad(w, ((0, 0), (0, 0), (MLA_NOPE_DIM, LANES - MLA_NOPE_DIM - MLA_ROPE_DIM)))

    f_group = jnp.pad(jnp.concatenate([f_log] * N_SPLIT, axis=-1),
                      ((0, 0), (0, 0), (0, LANES - N_SPLIT * FOX_HEADS)))
    w_small = jnp.concatenate(
        [rope_group(k_rope), rope_group(_rotate_half_cols(k_rope)), f_group], axis=-1).astype(BF16)
    b3 = jnp.pad(jnp.concatenate([b_forget] * N_SPLIT, axis=-1),
                 ((0, 0), (0, LANES - N_SPLIT * FOX_HEADS))).reshape(depth, 1, LANES)

    dq = MLA_NOPE_DIM + MLA_ROPE_DIM
    wq = w_uq.reshape(depth, MLA_Q_RANK, MLA_HEADS, dq)
    wq_a = jnp.pad(wq, ((0, 0), (0, 0), (0, 0), (0, LANES - dq)))
    wq_r = jnp.pad(_rotate_half_cols(wq[..., MLA_NOPE_DIM:]),
                   ((0, 0), (0, 0), (0, 0), (MLA_NOPE_DIM, LANES - dq)))
    wq_full = jnp.concatenate([wq_a.reshape(depth, MLA_Q_RANK, -1),
                               wq_r.reshape(depth, MLA_Q_RANK, -1)], axis=-1).astype(BF16)
    wkv = w_ukv.reshape(depth, MLA_KV_RANK, MLA_HEADS, MLA_NOPE_DIM + MLA_V_DIM)
    wk = jnp.pad(wkv[..., :MLA_NOPE_DIM], ((0, 0), (0, 0), (0, 0), (0, LANES - MLA_NOPE_DIM)))
    wk = wk.reshape(depth, MLA_KV_RANK, -1).astype(BF16)
    wv = wkv[..., MLA_NOPE_DIM:].reshape(depth, MLA_KV_RANK, -1).astype(BF16)
    return w_main, w_small, b3, wq_full, wk, wv


def kernel(x, mem, positions, ln_in_g, ln_in_b, w_in, b_forget, w_uq, g_cq, w_ukv, g_ckv,
           w_mem_kv, w_br, w_out, ln1_g, ln1_b, w_ff1, w_ff2, ln2_g, ln2_b):
    batch, seq, _ = x.shape
    depth = w_in.shape[0]
    assert seq % ATTN_TILE == 0 and seq % ROW_TILE == 0
    alpha = float((2 * depth) ** 0.25)

    w_main, w_small, b3, wq, wk, wv = _layer_weights(w_in, b_forget, w_uq, w_ukv)
    w_br_b = w_br.astype(BF16)
    w_out_b = w_out.astype(BF16)
    w_ff1_b = w_ff1.astype(BF16)
    w_ff2_b = w_ff2.astype(BF16)

    ctab, stab = _rope_tables(positions)
    mk, mv = _mem_kv(mem, w_mem_kv.astype(BF16))
    h = _entry_norm(x.reshape(batch * seq, D_MODEL), ln_in_g, ln_in_b)
    for l in range(depth):
        gates, fqkv, qmem, cq, ckv, small = _in_projection(h, w_main[l], w_small[l])
        fq_aug, fk_aug = _fox_prep(fqkv, small, b3[l], batch, seq)
        o_a = _causal_attention(fq_aug, fk_aug, fqkv, 2, batch, seq, 1)
        mq_aug, mk_aug, mv_b = _mla_prep(cq, ckv, small, ctab, stab, g_cq[l], g_ckv[l],
                                         wq[l], wk[l], wv[l])
        o_b = _causal_attention(mq_aug, mk_aug, mv_b, 0, batch, seq, CHUNK)
        o_c = _mem_attention(qmem, mk, mv, l, batch, seq)
        h = _merge(o_a, o_b, o_c, gates, h, w_br_b[l], w_out_b[l], ln1_g[l], ln1_b[l], alpha)
        h = _ffn(h, w_ff1_b[l], w_ff2_b[l], ln2_g[l], ln2_b[l], alpha)
    return h.reshape(batch, seq, D_MODEL)
```

```python
import functools
import math

import numpy as np
import jax
import jax.numpy as jnp
from jax import lax
from jax.experimental import pallas as pl
from jax.experimental.pallas import tpu as pltpu

D_MODEL = 1024
CHUNK = 64
FOX_HEADS = 8
FOX_HEAD_DIM = 64
MLA_HEADS = 8
MLA_NOPE_DIM = 64
MLA_ROPE_DIM = 32
MLA_V_DIM = 64
MLA_Q_RANK = 384
MLA_KV_RANK = 256
ROPE_BASE = 10000.0
MEM_HEADS = 4
MEM_HEAD_DIM = 128
N_BRANCHES = 3
BRANCH_WIDTH = 512
D_FF = 4 * D_MODEL
LN_EPS = 1e-5
RMS_EPS = 1e-6
NEG_INF = -1e30
LOG2E = math.log2(math.e)

LANES = 128
HALF = 64
N_GROUPS = 8
VMEM_LIMIT = 56 * 1024 * 1024

ROW_TILE = 256
ATTN_TILE = 256
SUM_ROWS = 16
F32 = jnp.float32
BF16 = jnp.bfloat16


def _params(n_axes):
    return pltpu.CompilerParams(
        dimension_semantics=("arbitrary",) * n_axes, vmem_limit_bytes=VMEM_LIMIT)


def _dot(a, b):
    return jnp.dot(a, b, preferred_element_type=F32)


def _dot_nt(a, b):
    return lax.dot_general(a, b, (((1,), (1,)), ((), ())), preferred_element_type=F32)


def _layer_norm(z, g, b):
    mu = jnp.mean(z, axis=-1, keepdims=True)
    d = z - mu
    var = jnp.mean(d * d, axis=-1, keepdims=True)
    return d * lax.rsqrt(var + LN_EPS) * g + b


def _rms_norm(z, g):
    return z * lax.rsqrt(jnp.mean(z * z, axis=-1, keepdims=True) + RMS_EPS) * g


def _row_spec(tile, width, col_block=0):
    return pl.BlockSpec((tile, width), lambda i: (i, col_block))


def _full_spec(shape):
    return pl.BlockSpec(shape, lambda *_: (0,) * len(shape))


def _ln_kernel(x_ref, g_ref, b_ref, o_ref):
    o_ref[...] = _layer_norm(x_ref[...], g_ref[...], b_ref[...])


def _entry_norm(x2, g, b):
    t = x2.shape[0]
    return pl.pallas_call(
        _ln_kernel,
        grid=(t // ROW_TILE,),
        in_specs=[_row_spec(ROW_TILE, D_MODEL), _full_spec((1, D_MODEL)), _full_spec((1, D_MODEL))],
        out_specs=_row_spec(ROW_TILE, D_MODEL),
        out_shape=jax.ShapeDtypeStruct((t, D_MODEL), F32),
        compiler_params=_params(1),
        name="entry_norm",
    )(x2, g.reshape(1, -1), b.reshape(1, -1))


def _rope_table_kernel(pos_ref, invf_ref, c_ref, s_ref):
    ang = pos_ref[...].astype(F32) * invf_ref[...]
    lane = lax.broadcasted_iota(jnp.int32, ang.shape, 1)
    rot = (lane >= MLA_NOPE_DIM) & (lane < MLA_NOPE_DIM + MLA_ROPE_DIM)
    c_ref[...] = jnp.where(lane < MLA_NOPE_DIM, 1.0, jnp.where(rot, jnp.cos(ang), 0.0))
    s_ref[...] = jnp.where(rot, jnp.sin(ang), 0.0)


def _rope_tables(positions):
    t = positions.size
    inv_freq = ROPE_BASE ** (-jnp.arange(0, MLA_ROPE_DIM, 2, dtype=F32) / MLA_ROPE_DIM)
    invf = jnp.zeros((1, LANES), F32).at[0, MLA_NOPE_DIM:MLA_NOPE_DIM + MLA_ROPE_DIM].set(
        jnp.concatenate([inv_freq, inv_freq]))
    return pl.pallas_call(
        _rope_table_kernel,
        grid=(t // ROW_TILE,),
        in_specs=[_row_spec(ROW_TILE, 1), _full_spec((1, LANES))],
        out_specs=[_row_spec(ROW_TILE, LANES), _row_spec(ROW_TILE, LANES)],
        out_shape=[jax.ShapeDtypeStruct((t, LANES), F32)] * 2,
        compiler_params=_params(1),
        name="rope_tables",
    )(positions.reshape(t, 1), invf)


GATE_W = N_BRANCHES * D_MODEL
FOX_W = 3 * FOX_HEADS * FOX_HEAD_DIM
QMEM_W = MEM_HEADS * MEM_HEAD_DIM
MAIN_W = GATE_W + FOX_W + QMEM_W + MLA_Q_RANK + MLA_KV_RANK
SMALL_W = 3 * LANES
DOT_CHUNK = 512


def _inproj_kernel(h_ref, wm_ref, ws_ref, gates_ref, fqkv_ref, qmem_ref, cq_ref, ckv_ref, small_ref):
    xb = h_ref[...].astype(BF16)

    def mm(c0, n):
        return _dot(xb, wm_ref[:, c0:c0 + n])

    for c in range(0, GATE_W, DOT_CHUNK):
        g = mm(c, DOT_CHUNK)
        gates_ref[:, c:c + DOT_CHUNK] = (1.0 / (1.0 + jnp.exp(-g))).astype(BF16)
    for c in range(0, FOX_W, DOT_CHUNK):
        fqkv_ref[:, c:c + DOT_CHUNK] = mm(GATE_W + c, DOT_CHUNK).astype(BF16)
    off = GATE_W + FOX_W
    qmem_ref[...] = mm(off, QMEM_W).astype(BF16)
    off += QMEM_W
    cq_ref[...] = mm(off, MLA_Q_RANK).astype(BF16)
    off += MLA_Q_RANK
    ckv_ref[...] = mm(off, MLA_KV_RANK).astype(BF16)
    small_ref[...] = _dot(xb, ws_ref[...])


def _in_projection(h, w_main, w_small):
    t = h.shape[0]
    widths = (GATE_W, FOX_W, QMEM_W, MLA_Q_RANK, MLA_KV_RANK)
    return pl.pallas_call(
        _inproj_kernel,
        grid=(t // ROW_TILE,),
        in_specs=[_row_spec(ROW_TILE, D_MODEL), _full_spec((D_MODEL, MAIN_W)),
                  _full_spec((D_MODEL, SMALL_W))],
        out_specs=[_row_spec(ROW_TILE, w) for w in widths] + [_row_spec(ROW_TILE, SMALL_W)],
        out_shape=[jax.ShapeDtypeStruct((t, w), BF16) for w in widths]
        + [jax.ShapeDtypeStruct((t, SMALL_W), F32)],
        compiler_params=_params(1),
        name="in_projection",
    )(h, w_main, w_small)


N_SPLIT = 3


def _split3(x):
    hi = x.astype(BF16)
    r = x - hi.astype(F32)
    mid = r.astype(BF16)
    lo = (r - mid.astype(F32)).astype(BF16)
    return hi, mid, lo


def _fox_prep_kernel(fq_ref, fk_ref, fv_ref, f_ref, bf_ref, pq_ref, pk_ref, oq_ref, ok_ref,
                     qa_ref, ka_ref, vt_ref, carry_ref):
    tm = fq_ref.shape[0]

    @pl.when(pl.program_id(1) == 0)
    def _():
        carry_ref[...] = jnp.zeros_like(carry_ref)

    x = f_ref[...] + bf_ref[...]
    logf = jnp.minimum(x, 0.0) - jnp.log1p(jnp.exp(-jnp.abs(x)))
    row = lax.broadcasted_iota(jnp.int32, (tm, tm), 0)
    col = lax.broadcasted_iota(jnp.int32, (tm, tm), 1)
    tri = jnp.where(col <= row, 1.0, 0.0).astype(BF16)
    hi, mid, lo = _split3(logf)
    cum = _dot(tri, hi) + _dot(tri, mid) + _dot(tri, lo) + carry_ref[...]
    carry_ref[...] = cum[tm - 1:tm, :]

    lane = lax.broadcasted_iota(jnp.int32, (tm, LANES), 1)
    chi, cmid, clo = _split3(cum * LOG2E)
    zero = jnp.zeros_like(chi)
    pieces = jnp.where(lane < FOX_HEADS, chi,
                       jnp.where(lane < 2 * FOX_HEADS, cmid,
                                 jnp.where(lane < 3 * FOX_HEADS, clo, zero)))
    q_spare = _dot(pieces, pq_ref[...]) + oq_ref[...]
    k_spare = _dot(pieces, pk_ref[...]) + ok_ref[...]

    left = lane < HALF
    for h in range(FOX_HEADS):
        pair = slice((h // 2) * LANES, (h // 2 + 1) * LANES)
        grp = slice(h * LANES, (h + 1) * LANES)
        own = left if h % 2 == 0 else jnp.logical_not(left)
        qp = fq_ref[:, pair].astype(F32) * (FOX_HEAD_DIM ** -0.5 * LOG2E)
        kp = fk_ref[:, pair].astype(F32)
        qa_ref[:, grp] = jnp.where(own, qp, q_spare[:, grp]).astype(BF16)
        ka_ref[:, grp] = jnp.where(own, kp, k_spare[:, grp]).astype(BF16)
    vt_ref[0] = fv_ref[...].astype(F32).T.astype(BF16)


def _fox_placement():
    pq = np.zeros((LANES, N_GROUPS * LANES), np.float32)
    pk = np.zeros((LANES, N_GROUPS * LANES), np.float32)
    oq = np.zeros((1, N_GROUPS * LANES), np.float32)
    ok = np.zeros((1, N_GROUPS * LANES), np.float32)
    for h in range(FOX_HEADS):
        base = h * LANES + (HALF if h % 2 == 0 else 0)
        for j in range(N_SPLIT):
            pq[j * FOX_HEADS + h, base + j] = 1.0
            ok[0, base + j] = 1.0
            oq[0, base + N_SPLIT + j] = 1.0
            pk[j * FOX_HEADS + h, base + N_SPLIT + j] = -1.0
    return (jnp.asarray(pq, BF16), jnp.asarray(pk, BF16), jnp.asarray(oq), jnp.asarray(ok))


def _fox_prep(fqkv, small, b_forget3, batch, seq):
    t = fqkv.shape[0]
    nt = seq // ROW_TILE
    pq, pk, oq, ok = _fox_placement()
    wide = N_GROUPS * LANES

    def rows(width, col_block):
        return pl.BlockSpec((ROW_TILE, width), lambda b, i: (b * nt + i, col_block))

    return pl.pallas_call(
        _fox_prep_kernel,
        grid=(batch, nt),
        in_specs=[rows(BRANCH_WIDTH, 0), rows(BRANCH_WIDTH, 1), rows(BRANCH_WIDTH, 2),
                  rows(LANES, 2), _full_spec((1, LANES)), _full_spec((LANES, wide)),
                  _full_spec((LANES, wide)), _full_spec((1, wide)), _full_spec((1, wide))],
        out_specs=[rows(wide, 0), rows(wide, 0),
                   pl.BlockSpec((1, BRANCH_WIDTH, ROW_TILE), lambda b, i: (b, 0, i))],
        out_shape=[jax.ShapeDtypeStruct((t, wide), BF16)] * 2
        + [jax.ShapeDtypeStruct((batch, BRANCH_WIDTH, seq), BF16)],
        scratch_shapes=[pltpu.VMEM((1, LANES), F32)],
        compiler_params=_params(2),
        name="fox_prep",
    )(fqkv, fqkv, fqkv, small, b_forget3, pq, pk, oq, ok)


def _mla_prep_kernel(cq_ref, ckv_ref, kr_ref, ct_ref, st_ref, gq_ref, gkv_ref,
                     wq_ref, wk_ref, wvt_ref, qa_ref, ka_ref, vt_ref):
    ct = ct_ref[...]
    st = st_ref[...]
    wide = N_GROUPS * LANES
    scale = (MLA_NOPE_DIM + MLA_ROPE_DIM) ** -0.5 * LOG2E
    xq = _rms_norm(cq_ref[...].astype(F32), gq_ref[...]).astype(BF16)
    xkv = _rms_norm(ckv_ref[...].astype(F32), gkv_ref[...]).astype(BF16)
    kpe = kr_ref[:, :LANES] * ct + kr_ref[:, LANES:] * st
    for h in range(MLA_HEADS):
        grp = slice(h * LANES, (h + 1) * LANES)
        rot = slice(wide + h * LANES, wide + (h + 1) * LANES)
        q = _dot(xq, wq_ref[:, grp]) * ct + _dot(xq, wq_ref[:, rot]) * st
        qa_ref[:, grp] = (q * scale).astype(BF16)
        ka_ref[:, grp] = (_dot(xkv, wk_ref[:, grp]) + kpe).astype(BF16)
    vt_ref[0] = _dot_nt(wvt_ref[...], xkv).astype(BF16)


def _mla_prep(cq, ckv, small, ctab, stab, g_cq, g_ckv, wq, wk, wvt, batch, seq):
    t = cq.shape[0]
    nt = seq // ROW_TILE
    wide = N_GROUPS * LANES
    return pl.pallas_call(
        _mla_prep_kernel,
        grid=(t // ROW_TILE,),
        in_specs=[_row_spec(ROW_TILE, MLA_Q_RANK), _row_spec(ROW_TILE, MLA_KV_RANK),
                  _row_spec(ROW_TILE, 2 * LANES), _row_spec(ROW_TILE, LANES),
                  _row_spec(ROW_TILE, LANES), _full_spec((1, MLA_Q_RANK)),
                  _full_spec((1, MLA_KV_RANK)), _full_spec((MLA_Q_RANK, 2 * wide)),
                  _full_spec((MLA_KV_RANK, wide)), _full_spec((BRANCH_WIDTH, MLA_KV_RANK))],
        out_specs=[_row_spec(ROW_TILE, wide), _row_spec(ROW_TILE, wide),
                   pl.BlockSpec((1, BRANCH_WIDTH, ROW_TILE), lambda i: (i // nt, 0, i % nt))],
        out_shape=[jax.ShapeDtypeStruct((t, wide), BF16), jax.ShapeDtypeStruct((t, wide), BF16),
                   jax.ShapeDtypeStruct((batch, BRANCH_WIDTH, seq), BF16)],
        compiler_params=_params(1),
        name="mla_prep",
    )(cq, ckv, small, ctab, stab, g_cq.reshape(1, -1), g_ckv.reshape(1, -1), wq, wk, wvt)


def _attn_kernel(q_ref, k_ref, vt_ref, o_ref, qt_ref, sa_ref, sb_ref, m_ref, acc_ref,
                 *, granule_shift):
    tq = q_ref.shape[1]
    tk = tq
    qi = pl.program_id(1)
    heads = range(N_GROUPS)

    for h in heads:
        qt_ref[h] = q_ref[0, :, h * LANES:(h + 1) * LANES].astype(F32).T.astype(BF16)
    m_ref[...] = jnp.full(m_ref.shape, NEG_INF, F32)
    acc_ref[...] = jnp.zeros(acc_ref.shape, F32)
    ones_rows = jnp.ones((SUM_ROWS, tk), BF16)

    def scores(j, s_ref):
        ks = pl.multiple_of(j * tk, tk)
        for h in heads:
            s_ref[h] = _dot(k_ref[0, pl.ds(ks, tk), h * LANES:(h + 1) * LANES], qt_ref[h])

    def update(j, s_ref, masked):
        ks = pl.multiple_of(j * tk, tk)
        for h in heads:
            s = s_ref[h]
            if masked:
                key = lax.broadcasted_iota(jnp.int32, (tk, tq), 0)
                qry = lax.broadcasted_iota(jnp.int32, (tk, tq), 1)
                s = jnp.where((key >> granule_shift) <= (qry >> granule_shift), s, NEG_INF)
            m = m_ref[h]
            m_new = jnp.maximum(m, jnp.max(s, axis=0, keepdims=True))
            alpha = jnp.exp2(m - m_new)
            e = jnp.exp2(s - m_new)
            m_ref[h] = m_new
            v_t = jnp.concatenate([vt_ref[0, h * HALF:(h + 1) * HALF, pl.ds(ks, tk)], ones_rows],
                                  axis=0)
            acc_ref[h] = alpha * acc_ref[h] + _dot(v_t, e.astype(BF16))

    scores(0, sa_ref)

    def two_tiles(i, _):
        j = 2 * i
        scores(j + 1, sb_ref)
        update(j, sa_ref, False)
        scores(j + 2, sa_ref)
        update(j + 1, sb_ref, False)
        return 0

    lax.fori_loop(0, qi // 2, two_tiles, 0)
    j_last = 2 * (qi // 2)

    @pl.when(qi % 2 == 1)
    def _():
        scores(j_last + 1, sb_ref)
        update(j_last, sa_ref, False)
        update(j_last + 1, sb_ref, True)

    @pl.when(qi % 2 == 0)
    def _():
        update(j_last, sa_ref, True)

    for pair in range(N_GROUPS // 2):
        halves = []
        for h in (2 * pair, 2 * pair + 1):
            halves.append(acc_ref[h, :HALF, :] * (1.0 / acc_ref[h, HALF:HALF + 1, :]))
        o_t = jnp.concatenate(halves, axis=0)
        o_ref[0, :, pair * LANES:(pair + 1) * LANES] = o_t.T.astype(BF16)


def _causal_attention(q_aug, k_aug, v_t, batch, seq, granule):
    wide = N_GROUPS * LANES
    tile = ATTN_TILE
    q3 = q_aug.reshape(batch, seq, wide)
    k3 = k_aug.reshape(batch, seq, wide)
    out = pl.pallas_call(
        functools.partial(_attn_kernel, granule_shift=granule.bit_length() - 1),
        grid=(batch, seq // tile),
        in_specs=[pl.BlockSpec((1, tile, wide), lambda b, i: (b, i, 0)),
                  pl.BlockSpec((1, seq, wide), lambda b, i: (b, 0, 0)),
                  pl.BlockSpec((1, BRANCH_WIDTH, seq), lambda b, i: (b, 0, 0))],
        out_specs=pl.BlockSpec((1, tile, BRANCH_WIDTH), lambda b, i: (b, i, 0)),
        out_shape=jax.ShapeDtypeStruct((batch, seq, BRANCH_WIDTH), BF16),
        scratch_shapes=[pltpu.VMEM((N_GROUPS, LANES, tile), BF16),
                        pltpu.VMEM((N_GROUPS, tile, tile), F32),
                        pltpu.VMEM((N_GROUPS, tile, tile), F32),
                        pltpu.VMEM((N_GROUPS, 1, tile), F32),
                        pltpu.VMEM((N_GROUPS, HALF + SUM_ROWS, tile), F32)],
        compiler_params=_params(2),
        name="causal_attention_g%d" % granule,
    )(q3, k3, v_t)
    return out.reshape(batch * seq, BRANCH_WIDTH)


def _mem_kv_kernel(mem_ref, w_ref, k_ref, v_ref):
    kv = _dot(mem_ref[0].astype(BF16), w_ref[0])
    k_ref[0, 0] = (kv[:, :BRANCH_WIDTH] * (MEM_HEAD_DIM ** -0.5)).astype(BF16)
    v_ref[0, 0] = kv[:, BRANCH_WIDTH:].astype(BF16)


def _mem_kv(mem, w_mem_kv):
    depth = w_mem_kv.shape[0]
    batch, n_mem, _ = mem.shape
    return pl.pallas_call(
        _mem_kv_kernel,
        grid=(depth, batch),
        in_specs=[pl.BlockSpec((1, n_mem, D_MODEL), lambda l, b: (b, 0, 0)),
                  pl.BlockSpec((1, D_MODEL, 2 * BRANCH_WIDTH), lambda l, b: (l, 0, 0))],
        out_specs=[pl.BlockSpec((1, 1, n_mem, BRANCH_WIDTH), lambda l, b: (l, b, 0, 0))] * 2,
        out_shape=[jax.ShapeDtypeStruct((depth, batch, n_mem, BRANCH_WIDTH), BF16)] * 2,
        compiler_params=_params(2),
        name="mem_kv",
    )(mem, w_mem_kv)


def _mem_attn_kernel(q_ref, k_ref, v_ref, o_ref):
    for h in range(MEM_HEADS):
        grp = slice(h * MEM_HEAD_DIM, (h + 1) * MEM_HEAD_DIM)
        s = _dot_nt(q_ref[0, :, grp], k_ref[0, 0, :, grp])
        e = jnp.exp(s - jnp.max(s, axis=1, keepdims=True))
        o = _dot(e.astype(BF16), v_ref[0, 0, :, grp])
        o_ref[0, :, grp] = (o * (1.0 / jnp.sum(e, axis=1, keepdims=True))).astype(BF16)


def _mem_attention(qmem, mk, mv, layer, batch, seq):
    n_mem = mk.shape[2]
    q3 = qmem.reshape(batch, seq, BRANCH_WIDTH)
    kv_spec = pl.BlockSpec((1, 1, n_mem, BRANCH_WIDTH), lambda b, i: (layer, b, 0, 0))
    out = pl.pallas_call(
        _mem_attn_kernel,
        grid=(batch, seq // ROW_TILE),
        in_specs=[pl.BlockSpec((1, ROW_TILE, BRANCH_WIDTH), lambda b, i: (b, i, 0)), kv_spec, kv_spec],
        out_specs=pl.BlockSpec((1, ROW_TILE, BRANCH_WIDTH), lambda b, i: (b, i, 0)),
        out_shape=jax.ShapeDtypeStruct((batch, seq, BRANCH_WIDTH), BF16),
        compiler_params=_params(2),
        name="mem_attention",
    )(q3, mk, mv)
    return out.reshape(batch * seq, BRANCH_WIDTH)


def _merge_kernel(oa_ref, ob_ref, oc_ref, gates_ref, h_ref, wbr_ref, wout_ref, g_ref, b_ref,
                  o_ref, *, alpha):
    merged = None
    for n, br_ref in enumerate((oa_ref, ob_ref, oc_ref)):
        gate = gates_ref[:, n * D_MODEL:(n + 1) * D_MODEL].astype(F32)
        term = gate * _dot(br_ref[...], wbr_ref[n])
        merged = term if merged is None else merged + term
    y = _dot(merged.astype(BF16), wout_ref[...])
    o_ref[...] = _layer_norm(alpha * h_ref[...] + y, g_ref[...], b_ref[...])


def _merge(oa, ob, oc, gates, h, w_br, w_out, g, b, alpha):
    t = h.shape[0]
    br = _row_spec(ROW_TILE, BRANCH_WIDTH)
    return pl.pallas_call(
        functools.partial(_merge_kernel, alpha=alpha),
        grid=(t // ROW_TILE,),
        in_specs=[br, br, br, _row_spec(ROW_TILE, GATE_W), _row_spec(ROW_TILE, D_MODEL),
                  _full_spec((N_BRANCHES, BRANCH_WIDTH, D_MODEL)), _full_spec((D_MODEL, D_MODEL)),
                  _full_spec((1, D_MODEL)), _full_spec((1, D_MODEL))],
        out_specs=_row_spec(ROW_TILE, D_MODEL),
        out_shape=jax.ShapeDtypeStruct((t, D_MODEL), F32),
        compiler_params=_params(1),
        name="merge",
    )(oa, ob, oc, gates, h, w_br, w_out, g.reshape(1, -1), b.reshape(1, -1))


FF_CHUNK = 1024


def _ffn_kernel(h_ref, w1_ref, w2_ref, g_ref, b_ref, o_ref, *, alpha):
    h = h_ref[...]
    xb = h.astype(BF16)
    acc = alpha * h
    for c in range(0, D_FF, FF_CHUNK):
        u = jnp.maximum(_dot(xb, w1_ref[:, c:c + FF_CHUNK]), 0.0)
        acc = acc + _dot((u * u).astype(BF16), w2_ref[c:c + FF_CHUNK, :])
    o_ref[...] = _layer_norm(acc, g_ref[...], b_ref[...])


def _ffn(h, w1, w2, g, b, alpha):
    t = h.shape[0]
    return pl.pallas_call(
        functools.partial(_ffn_kernel, alpha=alpha),
        grid=(t // ROW_TILE,),
        in_specs=[_row_spec(ROW_TILE, D_MODEL), _full_spec((D_MODEL, D_FF)),
                  _full_spec((D_FF, D_MODEL)), _full_spec((1, D_MODEL)), _full_spec((1, D_MODEL))],
        out_specs=_row_spec(ROW_TILE, D_MODEL),
        out_shape=jax.ShapeDtypeStruct((t, D_MODEL), F32),
        compiler_params=_params(1),
        name="ffn",
    )(h, w1, w2, g.reshape(1, -1), b.reshape(1, -1))


def _rotate_half_cols(w):
    half = w.shape[-1] // 2
    return jnp.concatenate([-w[..., half:], w[..., :half]], axis=-1)


def _layer_weights(w_in, b_forget, w_uq, w_ukv):
    depth = w_in.shape[0]
    splits = np.cumsum([FOX_W, FOX_HEADS, MLA_Q_RANK, MLA_KV_RANK, MLA_ROPE_DIM, QMEM_W])
    fox, f_log, c_q, c_kv, k_rope, q_mem, gate = jnp.split(w_in, [int(s) for s in splits], axis=-1)
    w_main = jnp.concatenate([gate, fox, q_mem, c_q, c_kv], axis=-1).astype(BF16)

    def rope_group(w):
        return jnp.pad(w, ((0, 0), (0, 0), (MLA_NOPE_DIM, LANES - MLA_NOPE_DIM - MLA_ROPE_DIM)))

    f_group = jnp.pad(jnp.concatenate([f_log] * N_SPLIT, axis=-1),
                      ((0, 0), (0, 0), (0, LANES - N_SPLIT * FOX_HEADS)))
    w_small = jnp.concatenate(
        [rope_group(k_rope), rope_group(_rotate_half_cols(k_rope)), f_group], axis=-1).astype(BF16)
    b3 = jnp.pad(jnp.concatenate([b_forget] * N_SPLIT, axis=-1),
                 ((0, 0), (0, LANES - N_SPLIT * FOX_HEADS))).reshape(depth, 1, LANES)

    dq = MLA_NOPE_DIM + MLA_ROPE_DIM
    wq = w_uq.reshape(depth, MLA_Q_RANK, MLA_HEADS, dq)
    wq_a = jnp.pad(wq, ((0, 0), (0, 0), (0, 0), (0, LANES - dq)))
    wq_r = jnp.pad(_rotate_half_cols(wq[..., MLA_NOPE_DIM:]),
                   ((0, 0), (0, 0), (0, 0), (MLA_NOPE_DIM, LANES - dq)))
    wq_full = jnp.concatenate([wq_a.reshape(depth, MLA_Q_RANK, -1),
                               wq_r.reshape(depth, MLA_Q_RANK, -1)], axis=-1).astype(BF16)
    wkv = w_ukv.reshape(depth, MLA_KV_RANK, MLA_HEADS, MLA_NOPE_DIM + MLA_V_DIM)
    wk = jnp.pad(wkv[..., :MLA_NOPE_DIM], ((0, 0), (0, 0), (0, 0), (0, LANES - MLA_NOPE_DIM)))
    wk = wk.reshape(depth, MLA_KV_RANK, -1).astype(BF16)
    wv_t = jnp.swapaxes(wkv[..., MLA_NOPE_DIM:].reshape(depth, MLA_KV_RANK, -1), 1, 2).astype(BF16)
    return w_main, w_small, b3, wq_full, wk, wv_t


def kernel(x, mem, positions, ln_in_g, ln_in_b, w_in, b_forget, w_uq, g_cq, w_ukv, g_ckv,
           w_mem_kv, w_br, w_out, ln1_g, ln1_b, w_ff1, w_ff2, ln2_g, ln2_b):
    batch, seq, _ = x.shape
    depth = w_in.shape[0]
    assert seq % ATTN_TILE == 0 and seq % ROW_TILE == 0
    alpha = float((2 * depth) ** 0.25)

    w_main, w_small, b3, wq, wk, wv_t = _layer_weights(w_in, b_forget, w_uq, w_ukv)
    w_br_b = w_br.astype(BF16)
    w_out_b = w_out.astype(BF16)
    w_ff1_b = w_ff1.astype(BF16)
    w_ff2_b = w_ff2.astype(BF16)

    ctab, stab = _rope_tables(positions)
    mk, mv = _mem_kv(mem, w_mem_kv.astype(BF16))
    h = _entry_norm(x.reshape(batch * seq, D_MODEL), ln_in_g, ln_in_b)
    for l in range(depth):
        gates, fqkv, qmem, cq, ckv, small = _in_projection(h, w_main[l], w_small[l])
        fq_aug, fk_aug, fv_t = _fox_prep(fqkv, small, b3[l], batch, seq)
        o_a = _causal_attention(fq_aug, fk_aug, fv_t, batch, seq, 1)
        mq_aug, mk_aug, mv_t = _mla_prep(cq, ckv, small, ctab, stab, g_cq[l], g_ckv[l],
                                         wq[l], wk[l], wv_t[l], batch, seq)
        o_b = _causal_attention(mq_aug, mk_aug, mv_t, batch, seq, CHUNK)
        o_c = _mem_attention(qmem, mk, mv, l, batch, seq)
        h = _merge(o_a, o_b, o_c, gates, h, w_br_b[l], w_out_b[l], ln1_g[l], ln1_b[l], alpha)
        h = _ffn(h, w_ff1_b[l], w_ff2_b[l], ln2_g[l], ln2_b[l], alpha)
    return h.reshape(batch, seq, D_MODEL)
```

```python
import functools
import math

import numpy as np
import jax
import jax.numpy as jnp
from jax import lax
from jax.experimental import pallas as pl
from jax.experimental.pallas import tpu as pltpu

D_MODEL = 1024
CHUNK = 64
FOX_HEADS = 8
FOX_HEAD_DIM = 64
MLA_HEADS = 8
MLA_NOPE_DIM = 64
MLA_ROPE_DIM = 32
MLA_V_DIM = 64
MLA_Q_RANK = 384
MLA_KV_RANK = 256
ROPE_BASE = 10000.0
MEM_HEADS = 4
MEM_HEAD_DIM = 128
N_BRANCHES = 3
BRANCH_WIDTH = 512
D_FF = 4 * D_MODEL
LN_EPS = 1e-5
RMS_EPS = 1e-6
NEG_INF = -1e30
LOG2E = math.log2(math.e)

LANES = 128
HALF = 64
N_GROUPS = 8
WIDE = N_GROUPS * LANES
MXU_WIDTH = 256
VMEM_LIMIT = 56 * 1024 * 1024

ROW_TILE = 256
ATTN_TILE = 256
SCORE_LEAD = 2
SUM_ROWS = 16

F32 = jnp.float32
BF16 = jnp.bfloat16


def _params(n_axes):
    return pltpu.CompilerParams(
        dimension_semantics=("arbitrary",) * n_axes, vmem_limit_bytes=VMEM_LIMIT)


def _dot(a, b):
    return jnp.dot(a, b, preferred_element_type=F32)


def _dot_nt(a, b):
    return lax.dot_general(a, b, (((1,), (1,)), ((), ())), preferred_element_type=F32)


def _layer_norm(z, g, b):
    mu = jnp.mean(z, axis=-1, keepdims=True)
    d = z - mu
    var = jnp.mean(d * d, axis=-1, keepdims=True)
    return d * lax.rsqrt(var + LN_EPS) * g + b


def _rms_norm(z, g):
    return z * lax.rsqrt(jnp.mean(z * z, axis=-1, keepdims=True) + RMS_EPS) * g


def _row_spec(tile, width, col_block=0):
    return pl.BlockSpec((tile, width), lambda i: (i, col_block))


def _full_spec(shape):
    return pl.BlockSpec(shape, lambda *_: (0,) * len(shape), pipeline_mode=pl.Buffered(1))


def _rope_table_kernel(pos_ref, invf_ref, c_ref, s_ref):
    ang = pos_ref[...].astype(F32) * invf_ref[...]
    lane = lax.broadcasted_iota(jnp.int32, ang.shape, 1)
    rot = (lane >= MLA_NOPE_DIM) & (lane < MLA_NOPE_DIM + MLA_ROPE_DIM)
    c_ref[...] = jnp.where(lane < MLA_NOPE_DIM, 1.0, jnp.where(rot, jnp.cos(ang), 0.0))
    s_ref[...] = jnp.where(rot, jnp.sin(ang), 0.0)


def _rope_tables(positions):
    t = positions.size
    inv_freq = ROPE_BASE ** (-jnp.arange(0, MLA_ROPE_DIM, 2, dtype=F32) / MLA_ROPE_DIM)
    invf = jnp.zeros((1, LANES), F32).at[0, MLA_NOPE_DIM:MLA_NOPE_DIM + MLA_ROPE_DIM].set(
        jnp.concatenate([inv_freq, inv_freq]))
    return pl.pallas_call(
        _rope_table_kernel,
        grid=(t // ROW_TILE,),
        in_specs=[_row_spec(ROW_TILE, 1), _full_spec((1, LANES))],
        out_specs=[_row_spec(ROW_TILE, LANES), _row_spec(ROW_TILE, LANES)],
        out_shape=[jax.ShapeDtypeStruct((t, LANES), F32)] * 2,
        compiler_params=_params(1),
        name="rope_tables",
    )(positions.reshape(t, 1), invf)


def _mem_kv_kernel(mem_ref, w_ref, k_ref, v_ref):
    kv = _dot(mem_ref[0].astype(BF16), w_ref[0])
    k_ref[0, 0] = (kv[:, :BRANCH_WIDTH] * (MEM_HEAD_DIM ** -0.5 * LOG2E)).astype(BF16)
    v_ref[0, 0] = kv[:, BRANCH_WIDTH:].astype(BF16)


def _mem_kv(mem, w_mem_kv):
    depth = w_mem_kv.shape[0]
    batch, n_mem, _ = mem.shape
    return pl.pallas_call(
        _mem_kv_kernel,
        grid=(depth, batch),
        in_specs=[pl.BlockSpec((1, n_mem, D_MODEL), lambda l, b: (b, 0, 0)),
                  pl.BlockSpec((1, D_MODEL, 2 * BRANCH_WIDTH), lambda l, b: (l, 0, 0))],
        out_specs=[pl.BlockSpec((1, 1, n_mem, BRANCH_WIDTH), lambda l, b: (l, b, 0, 0))] * 2,
        out_shape=[jax.ShapeDtypeStruct((depth, batch, n_mem, BRANCH_WIDTH), BF16)] * 2,
        compiler_params=_params(2),
        name="mem_kv",
    )(mem, w_mem_kv)


GATE_W = N_BRANCHES * D_MODEL
COL_FOX_Q = GATE_W
COL_FOX_K = COL_FOX_Q + BRANCH_WIDTH
COL_FOX_V = COL_FOX_K + BRANCH_WIDTH
COL_QMEM = COL_FOX_V + BRANCH_WIDTH
COL_CQ = COL_QMEM + BRANCH_WIDTH
COL_CKV = COL_CQ + MLA_Q_RANK
MAIN_W = COL_CKV + MLA_KV_RANK
SMALL_W = 3 * LANES
GATE_CHUNK = 512
N_SPLIT = 3


def _split3(x):
    hi = x.astype(BF16)
    r = x - hi.astype(F32)
    mid = r.astype(BF16)
    lo = (r - mid.astype(F32)).astype(BF16)
    return hi, mid, lo


def _mixer_inputs_kernel(*refs, entry_norm, tiles_per_seq):
    if entry_norm:
        x_ref, lng_ref, lnb_ref, *refs = refs
    else:
        x_ref, *refs = refs
    (wm_ref, ws_ref, bf_ref, pq_ref, pk_ref, oq_ref, ok_ref, ct_ref, st_ref, gq_ref, gkv_ref,
     wq_ref, wk_ref, wvt_ref, mk_ref, mv_ref,
     gates_ref, oc_ref, fqa_ref, fka_ref, fvt_ref, mqa_ref, mka_ref, mvt_ref, *rest) = refs
    carry_ref = rest[-1]
    tm = x_ref.shape[0]

    h = x_ref[...]
    if entry_norm:
        h = _layer_norm(h, lng_ref[...], lnb_ref[...])
        rest[0][...] = h
    xb = h.astype(BF16)

    def mm(c0, n):
        return _dot(xb, wm_ref[:, c0:c0 + n])

    for c in range(0, GATE_W, GATE_CHUNK):
        g = mm(c, GATE_CHUNK)
        gates_ref[:, c:c + GATE_CHUNK] = (1.0 / (1.0 + jnp.exp(-g))).astype(BF16)

    small = _dot(xb, ws_ref[...])

    @pl.when(pl.program_id(0) % tiles_per_seq == 0)
    def _():
        carry_ref[...] = jnp.zeros_like(carry_ref)

    x = small[:, 2 * LANES:] + bf_ref[...]
    logf = jnp.minimum(x, 0.0) - jnp.log1p(jnp.exp(-jnp.abs(x)))
    row = lax.broadcasted_iota(jnp.int32, (tm, tm), 0)
    col = lax.broadcasted_iota(jnp.int32, (tm, tm), 1)
    tri = jnp.where(col <= row, 1.0, 0.0).astype(BF16)
    hi, mid, lo = _split3(logf)
    cum = _dot(tri, hi) + _dot(tri, mid) + _dot(tri, lo) + carry_ref[...]
    carry_ref[...] = cum[tm - 1:tm, :]

    lane = lax.broadcasted_iota(jnp.int32, (tm, LANES), 1)
    chi, cmid, clo = _split3(cum * LOG2E)
    pieces = jnp.where(lane < FOX_HEADS, chi,
                       jnp.where(lane < 2 * FOX_HEADS, cmid,
                                 jnp.where(lane < 3 * FOX_HEADS, clo, jnp.zeros_like(chi))))
    left = lane < HALF
    heads_per_dot = MXU_WIDTH // HALF
    for h0 in range(0, FOX_HEADS, heads_per_dot):
        c0 = h0 * HALF
        qp2 = mm(COL_FOX_Q + c0, MXU_WIDTH) * (FOX_HEAD_DIM ** -0.5 * LOG2E)
        kp2 = mm(COL_FOX_K + c0, MXU_WIDTH)
        grp4 = slice(h0 * LANES, (h0 + heads_per_dot) * LANES)
        q_spare = _dot(pieces, pq_ref[:, grp4]) + oq_ref[:, grp4]
        k_spare = _dot(pieces, pk_ref[:, grp4]) + ok_ref[:, grp4]
        for i in range(heads_per_dot):
            head = h0 + i
            pair = slice((i // 2) * LANES, (i // 2 + 1) * LANES)
            loc = slice(i * LANES, (i + 1) * LANES)
            grp = slice(head * LANES, (head + 1) * LANES)
            own = left if head % 2 == 0 else jnp.logical_not(left)
            fqa_ref[:, grp] = jnp.where(own, qp2[:, pair], q_spare[:, loc]).astype(BF16)
            fka_ref[:, grp] = jnp.where(own, kp2[:, pair], k_spare[:, loc]).astype(BF16)
    fvt_ref[0] = mm(COL_FOX_V, BRANCH_WIDTH).T.astype(BF16)

    qm = mm(COL_QMEM, BRANCH_WIDTH).astype(BF16)
    for hm in range(MEM_HEADS):
        grp = slice(hm * MEM_HEAD_DIM, (hm + 1) * MEM_HEAD_DIM)
        s = _dot_nt(qm[:, grp], mk_ref[0, 0, :, grp])
        e = jnp.exp2(s - jnp.max(s, axis=1, keepdims=True))
        o = _dot(e.astype(BF16), mv_ref[0, 0, :, grp])
        oc_ref[:, grp] = (o * (1.0 / jnp.sum(e, axis=1, keepdims=True))).astype(BF16)

    ct = ct_ref[...]
    st = st_ref[...]
    groups_per_dot = MXU_WIDTH // LANES
    ct2 = jnp.concatenate([ct] * groups_per_dot, axis=1)
    st2 = jnp.concatenate([st] * groups_per_dot, axis=1)
    scale = (MLA_NOPE_DIM + MLA_ROPE_DIM) ** -0.5 * LOG2E
    xq = _rms_norm(mm(COL_CQ, MLA_Q_RANK), gq_ref[...]).astype(BF16)
    xkv = _rms_norm(mm(COL_CKV, MLA_KV_RANK), gkv_ref[...]).astype(BF16)
    kpe = small[:, :LANES] * ct + small[:, LANES:2 * LANES] * st
    kpe2 = jnp.concatenate([kpe] * groups_per_dot, axis=1)
    for c in range(0, WIDE, MXU_WIDTH):
        q = _dot(xq, wq_ref[:, c:c + MXU_WIDTH]) * ct2 \
            + _dot(xq, wq_ref[:, WIDE + c:WIDE + c + MXU_WIDTH]) * st2
        mqa_ref[:, c:c + MXU_WIDTH] = (q * scale).astype(BF16)
        mka_ref[:, c:c + MXU_WIDTH] = (_dot(xkv, wk_ref[:, c:c + MXU_WIDTH]) + kpe2).astype(BF16)
    mvt_ref[0] = _dot_nt(wvt_ref[...], xkv).astype(BF16)


def _fox_placement():
    pq = np.zeros((LANES, WIDE), np.float32)
    pk = np.zeros((LANES, WIDE), np.float32)
    oq = np.zeros((1, WIDE), np.float32)
    ok = np.zeros((1, WIDE), np.float32)
    for h in range(FOX_HEADS):
        base = h * LANES + (HALF if h % 2 == 0 else 0)
        for j in range(N_SPLIT):
            pq[j * FOX_HEADS + h, base + j] = 1.0
            ok[0, base + j] = 1.0
            oq[0, base + N_SPLIT + j] = 1.0
            pk[j * FOX_HEADS + h, base + N_SPLIT + j] = -1.0
    return (jnp.asarray(pq, BF16), jnp.asarray(pk, BF16), jnp.asarray(oq), jnp.asarray(ok))


def _mixer_inputs(h, ln_in, w_main, w_small, b_forget3, ctab, stab, g_cq, g_ckv, wq, wk, wvt,
                  mk, mv, layer, batch, seq):
    t = h.shape[0]
    nt = seq // ROW_TILE
    n_mem = mk.shape[2]
    entry_norm = ln_in is not None
    pq, pk, oq, ok = _fox_placement()
    vt_spec = pl.BlockSpec((1, BRANCH_WIDTH, ROW_TILE), lambda i: (i // nt, 0, i % nt))
    mem_spec = pl.BlockSpec((1, 1, n_mem, BRANCH_WIDTH), lambda i: (layer, i // nt, 0, 0))

    operands = [h]
    in_specs = [_row_spec(ROW_TILE, D_MODEL)]
    if entry_norm:
        operands += [ln_in[0].reshape(1, -1), ln_in[1].reshape(1, -1)]
        in_specs += [_full_spec((1, D_MODEL))] * 2
    operands += [w_main, w_small, b_forget3, pq, pk, oq, ok, ctab, stab,
                 g_cq.reshape(1, -1), g_ckv.reshape(1, -1), wq, wk, wvt, mk, mv]
    in_specs += [_full_spec((D_MODEL, MAIN_W)), _full_spec((D_MODEL, SMALL_W)), _full_spec((1, LANES)),
                 _full_spec((LANES, WIDE)), _full_spec((LANES, WIDE)), _full_spec((1, WIDE)),
                 _full_spec((1, WIDE)), _row_spec(ROW_TILE, LANES), _row_spec(ROW_TILE, LANES),
                 _full_spec((1, MLA_Q_RANK)), _full_spec((1, MLA_KV_RANK)),
                 _full_spec((MLA_Q_RANK, 2 * WIDE)), _full_spec((MLA_KV_RANK, WIDE)),
                 _full_spec((BRANCH_WIDTH, MLA_KV_RANK)), mem_spec, mem_spec]

    vt_shape = jax.ShapeDtypeStruct((batch, BRANCH_WIDTH, seq), BF16)
    out_specs = [_row_spec(ROW_TILE, GATE_W), _row_spec(ROW_TILE, BRANCH_WIDTH),
                 _row_spec(ROW_TILE, WIDE), _row_spec(ROW_TILE, WIDE), vt_spec,
                 _row_spec(ROW_TILE, WIDE), _row_spec(ROW_TILE, WIDE), vt_spec]
    out_shape = [jax.ShapeDtypeStruct((t, GATE_W), BF16), jax.ShapeDtypeStruct((t, BRANCH_WIDTH), BF16),
                 jax.ShapeDtypeStruct((t, WIDE), BF16), jax.ShapeDtypeStruct((t, WIDE), BF16), vt_shape,
                 jax.ShapeDtypeStruct((t, WIDE), BF16), jax.ShapeDtypeStruct((t, WIDE), BF16), vt_shape]
    if entry_norm:
        out_specs.append(_row_spec(ROW_TILE, D_MODEL))
        out_shape.append(jax.ShapeDtypeStruct((t, D_MODEL), F32))
    return pl.pallas_call(
        functools.partial(_mixer_inputs_kernel, entry_norm=entry_norm, tiles_per_seq=nt),
        grid=(t // ROW_TILE,),
        in_specs=in_specs,
        out_specs=out_specs,
        out_shape=out_shape,
        scratch_shapes=[pltpu.VMEM((1, LANES), F32)],
        compiler_params=_params(1),
        name="mixer_inputs",
    )(*operands)


def _attn_kernel(q_ref, k_ref, vt_ref, o_ref, qt_ref, sa_ref, sb_ref, m_ref, acc_ref,
                 *, granule_shift):
    tq = q_ref.shape[1]
    tk = tq
    qi = pl.program_id(1)
    heads = range(N_GROUPS)

    for h in heads:
        qt_ref[h] = q_ref[0, :, h * LANES:(h + 1) * LANES].astype(F32).T.astype(BF16)
    m_ref[...] = jnp.full(m_ref.shape, NEG_INF, F32)
    acc_ref[...] = jnp.zeros(acc_ref.shape, F32)
    ones_rows = jnp.ones((SUM_ROWS, tk), BF16)

    def scores(j, s_ref, h):
        ks = pl.multiple_of(j * tk, tk)
        s_ref[h] = _dot(k_ref[0, pl.ds(ks, tk), h * LANES:(h + 1) * LANES], qt_ref[h])

    def update(j, s_ref, h, masked):
        ks = pl.multiple_of(j * tk, tk)
        s = s_ref[h]
        if masked:
            key = lax.broadcasted_iota(jnp.int32, (tk, tq), 0)
            qry = lax.broadcasted_iota(jnp.int32, (tk, tq), 1)
            s = jnp.where((key >> granule_shift) <= (qry >> granule_shift), s, NEG_INF)
        m = m_ref[h]
        m_new = jnp.maximum(m, jnp.max(s, axis=0, keepdims=True))
        alpha = jnp.exp2(m - m_new)
        e = jnp.exp2(s - m_new)
        m_ref[h] = m_new
        v_t = jnp.concatenate([vt_ref[0, h * HALF:(h + 1) * HALF, pl.ds(ks, tk)], ones_rows],
                              axis=0)
        acc_ref[h] = alpha * acc_ref[h] + _dot(v_t, e.astype(BF16))

    def tile_step(j, cur_ref, masked, next_ref=None):
        if next_ref is not None:
            for h in range(SCORE_LEAD):
                scores(j + 1, next_ref, h)
        for h in heads:
            update(j, cur_ref, h, masked)
            if next_ref is not None and h + SCORE_LEAD < N_GROUPS:
                scores(j + 1, next_ref, h + SCORE_LEAD)

    for h in heads:
        scores(0, sa_ref, h)

    def two_tiles(i, _):
        tile_step(2 * i, sa_ref, False, sb_ref)
        tile_step(2 * i + 1, sb_ref, False, sa_ref)
        return 0

    lax.fori_loop(0, qi // 2, two_tiles, 0)
    j_last = 2 * (qi // 2)

    @pl.when(qi % 2 == 1)
    def _():
        tile_step(j_last, sa_ref, False, sb_ref)
        tile_step(j_last + 1, sb_ref, True)

    @pl.when(qi % 2 == 0)
    def _():
        tile_step(j_last, sa_ref, True)

    for pair in range(N_GROUPS // 2):
        halves = []
        for h in (2 * pair, 2 * pair + 1):
            halves.append(acc_ref[h, :HALF, :] * (1.0 / acc_ref[h, HALF:HALF + 1, :]))
        o_t = jnp.concatenate(halves, axis=0)
        o_ref[0, :, pair * LANES:(pair + 1) * LANES] = o_t.T.astype(BF16)


def _causal_attention(q_aug, k_aug, v_t, batch, seq, granule):
    tile = ATTN_TILE
    q3 = q_aug.reshape(batch, seq, WIDE)
    k3 = k_aug.reshape(batch, seq, WIDE)
    out = pl.pallas_call(
        functools.partial(_attn_kernel, granule_shift=granule.bit_length() - 1),
        grid=(batch, seq // tile),
        in_specs=[pl.BlockSpec((1, tile, WIDE), lambda b, i: (b, i, 0)),
                  pl.BlockSpec((1, seq, WIDE), lambda b, i: (b, 0, 0)),
                  pl.BlockSpec((1, BRANCH_WIDTH, seq), lambda b, i: (b, 0, 0))],
        out_specs=pl.BlockSpec((1, tile, BRANCH_WIDTH), lambda b, i: (b, i, 0)),
        out_shape=jax.ShapeDtypeStruct((batch, seq, BRANCH_WIDTH), BF16),
        scratch_shapes=[pltpu.VMEM((N_GROUPS, LANES, tile), BF16),
                        pltpu.VMEM((N_GROUPS, tile, tile), F32),
                        pltpu.VMEM((N_GROUPS, tile, tile), F32),
                        pltpu.VMEM((N_GROUPS, 1, tile), F32),
                        pltpu.VMEM((N_GROUPS, HALF + SUM_ROWS, tile), F32)],
        compiler_params=_params(2),
        name="causal_attention_g%d" % granule,
    )(q3, k3, v_t)
    return out.reshape(batch * seq, BRANCH_WIDTH)


FF_CHUNK = 1024


def _merge_ffn_kernel(oa_ref, ob_ref, oc_ref, gates_ref, h_ref, wbr_ref, wout_ref, g1_ref, b1_ref,
                      w1_ref, w2_ref, g2_ref, b2_ref, o_ref, *, alpha):
    merged = None
    for n, br_ref in enumerate((oa_ref, ob_ref, oc_ref)):
        gate = gates_ref[:, n * D_MODEL:(n + 1) * D_MODEL].astype(F32)
        term = gate * _dot(br_ref[...], wbr_ref[n])
        merged = term if merged is None else merged + term
    y = _dot(merged.astype(BF16), wout_ref[...])
    h1 = _layer_norm(alpha * h_ref[...] + y, g1_ref[...], b1_ref[...])
    xb = h1.astype(BF16)
    acc = alpha * h1
    for c in range(0, D_FF, FF_CHUNK):
        u = jnp.maximum(_dot(xb, w1_ref[:, c:c + FF_CHUNK]), 0.0)
        acc = acc + _dot((u * u).astype(BF16), w2_ref[c:c + FF_CHUNK, :])
    o_ref[...] = _layer_norm(acc, g2_ref[...], b2_ref[...])


def _merge_ffn(oa, ob, oc, gates, h, w_br, w_out, g1, b1, w1, w2, g2, b2, alpha):
    t = h.shape[0]
    br = _row_spec(ROW_TILE, BRANCH_WIDTH)
    vec = _full_spec((1, D_MODEL))
    return pl.pallas_call(
        functools.partial(_merge_ffn_kernel, alpha=alpha),
        grid=(t // ROW_TILE,),
        in_specs=[br, br, br, _row_spec(ROW_TILE, GATE_W), _row_spec(ROW_TILE, D_MODEL),
                  _full_spec((N_BRANCHES, BRANCH_WIDTH, D_MODEL)), _full_spec((D_MODEL, D_MODEL)),
                  vec, vec, _full_spec((D_MODEL, D_FF)), _full_spec((D_FF, D_MODEL)), vec, vec],
        out_specs=_row_spec(ROW_TILE, D_MODEL),
        out_shape=jax.ShapeDtypeStruct((t, D_MODEL), F32),
        compiler_params=_params(1),
        name="merge_ffn",
    )(oa, ob, oc, gates, h, w_br, w_out, g1.reshape(1, -1), b1.reshape(1, -1), w1, w2,
      g2.reshape(1, -1), b2.reshape(1, -1))


def _rotate_half_cols(w):
    half = w.shape[-1] // 2
    return jnp.concatenate([-w[..., half:], w[..., :half]], axis=-1)


def _layer_weights(w_in, b_forget, w_uq, w_ukv):
    depth = w_in.shape[0]
    fox_w = 3 * BRANCH_WIDTH
    splits = np.cumsum([fox_w, FOX_HEADS, MLA_Q_RANK, MLA_KV_RANK, MLA_ROPE_DIM, BRANCH_WIDTH])
    fox, f_log, c_q, c_kv, k_rope, q_mem, gate = jnp.split(w_in, [int(s) for s in splits], axis=-1)
    w_main = jnp.concatenate([gate, fox, q_mem, c_q, c_kv], axis=-1).astype(BF16)

    def rope_group(w):
        return jnp.pad(w, ((0, 0), (0, 0), (MLA_NOPE_DIM, LANES - MLA_NOPE_DIM - MLA_ROPE_DIM)))

    f_group = jnp.pad(jnp.concatenate([f_log] * N_SPLIT, axis=-1),
                      ((0, 0), (0, 0), (0, LANES - N_SPLIT * FOX_HEADS)))
    w_small = jnp.concatenate(
        [rope_group(k_rope), rope_group(_rotate_half_cols(k_rope)), f_group], axis=-1).astype(BF16)
    b3 = jnp.pad(jnp.concatenate([b_forget] * N_SPLIT, axis=-1),
                 ((0, 0), (0, LANES - N_SPLIT * FOX_HEADS))).reshape(depth, 1, LANES)

    dq = MLA_NOPE_DIM + MLA_ROPE_DIM
    wq = w_uq.reshape(depth, MLA_Q_RANK, MLA_HEADS, dq)
    wq_a = jnp.pad(wq, ((0, 0), (0, 0), (0, 0), (0, LANES - dq)))
    wq_r = jnp.pad(_rotate_half_cols(wq[..., MLA_NOPE_DIM:]),
                   ((0, 0), (0, 0), (0, 0), (MLA_NOPE_DIM, LANES - dq)))
    wq_full = jnp.concatenate([wq_a.reshape(depth, MLA_Q_RANK, -1),
                               wq_r.reshape(depth, MLA_Q_RANK, -1)], axis=-1).astype(BF16)
    wkv = w_ukv.reshape(depth, MLA_KV_RANK, MLA_HEADS, MLA_NOPE_DIM + MLA_V_DIM)
    wk = jnp.pad(wkv[..., :MLA_NOPE_DIM], ((0, 0), (0, 0), (0, 0), (0, LANES - MLA_NOPE_DIM)))
    wk = wk.reshape(depth, MLA_KV_RANK, -1).astype(BF16)
    wv_t = jnp.swapaxes(wkv[..., MLA_NOPE_DIM:].reshape(depth, MLA_KV_RANK, -1), 1, 2).astype(BF16)
    return w_main, w_small, b3, wq_full, wk, wv_t


def kernel(x, mem, positions, ln_in_g, ln_in_b, w_in, b_forget, w_uq, g_cq, w_ukv, g_ckv,
           w_mem_kv, w_br, w_out, ln1_g, ln1_b, w_ff1, w_ff2, ln2_g, ln2_b):
    batch, seq, _ = x.shape
    depth = w_in.shape[0]
    assert seq % ATTN_TILE == 0 and seq % ROW_TILE == 0
    alpha = float((2 * depth) ** 0.25)

    w_main, w_small, b3, wq, wk, wv_t = _layer_weights(w_in, b_forget, w_uq, w_ukv)
    w_br_b = w_br.astype(BF16)
    w_out_b = w_out.astype(BF16)
    w_ff1_b = w_ff1.astype(BF16)
    w_ff2_b = w_ff2.astype(BF16)

    ctab, stab = _rope_tables(positions)
    mk, mv = _mem_kv(mem, w_mem_kv.astype(BF16))
    h = x.reshape(batch * seq, D_MODEL)
    for l in range(depth):
        outs = _mixer_inputs(h, (ln_in_g, ln_in_b) if l == 0 else None, w_main[l], w_small[l], b3[l],
                             ctab, stab, g_cq[l], g_ckv[l], wq[l], wk[l], wv_t[l], mk, mv, l,
                             batch, seq)
        gates, o_c, fq_aug, fk_aug, fv_t, mq_aug, mk_aug, mv_t = outs[:8]
        if l == 0:
            h = outs[8]
        o_a = _causal_attention(fq_aug, fk_aug, fv_t, batch, seq, 1)
        o_b = _causal_attention(mq_aug, mk_aug, mv_t, batch, seq, CHUNK)
        h = _merge_ffn(o_a, o_b, o_c, gates, h, w_br_b[l], w_out_b[l], ln1_g[l], ln1_b[l],
                       w_ff1_b[l], w_ff2_b[l], ln2_g[l], ln2_b[l], alpha)
    return h.reshape(batch, seq, D_MODEL)
```

```python
import functools
import math

import numpy as np
import jax
import jax.numpy as jnp
from jax import lax
from jax.experimental import pallas as pl
from jax.experimental.pallas import tpu as pltpu

D_MODEL = 1024
CHUNK = 64
FOX_HEADS = 8
FOX_HEAD_DIM = 64
MLA_HEADS = 8
MLA_NOPE_DIM = 64
MLA_ROPE_DIM = 32
MLA_V_DIM = 64
MLA_Q_RANK = 384
MLA_KV_RANK = 256
ROPE_BASE = 10000.0
MEM_HEADS = 4
MEM_HEAD_DIM = 128
N_BRANCHES = 3
BRANCH_WIDTH = 512
D_FF = 4 * D_MODEL
LN_EPS = 1e-5
RMS_EPS = 1e-6
NEG_INF = -1e30
LOG2E = math.log2(math.e)

LANES = 128
HALF = 64
N_GROUPS = 8
WIDE = N_GROUPS * LANES
MXU_WIDTH = 256
VMEM_LIMIT = 56 * 1024 * 1024

ROW_TILE = 512
ATTN_TILE = 256
SCORE_LEAD = 2
SUM_ROWS = 16

F32 = jnp.float32
BF16 = jnp.bfloat16


def _params(n_axes):
    return pltpu.CompilerParams(
        dimension_semantics=("arbitrary",) * n_axes, vmem_limit_bytes=VMEM_LIMIT)


def _dot(a, b):
    return jnp.dot(a, b, preferred_element_type=F32)


def _dot_nt(a, b):
    return lax.dot_general(a, b, (((1,), (1,)), ((), ())), preferred_element_type=F32)


def _layer_norm(z, g, b):
    mu = jnp.mean(z, axis=-1, keepdims=True)
    d = z - mu
    var = jnp.mean(d * d, axis=-1, keepdims=True)
    return d * lax.rsqrt(var + LN_EPS) * g + b


def _rms_norm(z, g):
    return z * lax.rsqrt(jnp.mean(z * z, axis=-1, keepdims=True) + RMS_EPS) * g


def _row_spec(tile, width, col_block=0):
    return pl.BlockSpec((tile, width), lambda i: (i, col_block))


def _full_spec(shape):
    return pl.BlockSpec(shape, lambda *_: (0,) * len(shape), pipeline_mode=pl.Buffered(1))


ROPE_HALF = MLA_ROPE_DIM // 2


def _rope_table_kernel(pos_ref, invf_ref, c_ref, slo_ref, shi_ref):
    ang = pos_ref[...].astype(F32) * invf_ref[...]
    lane = lax.broadcasted_iota(jnp.int32, ang.shape, 1)
    lo = (lane >= MLA_NOPE_DIM) & (lane < MLA_NOPE_DIM + ROPE_HALF)
    hi = (lane >= MLA_NOPE_DIM + ROPE_HALF) & (lane < MLA_NOPE_DIM + MLA_ROPE_DIM)
    sin = jnp.sin(ang)
    c_ref[...] = jnp.where(lane < MLA_NOPE_DIM, 1.0, jnp.where(lo | hi, jnp.cos(ang), 0.0))
    slo_ref[...] = jnp.where(lo, -sin, 0.0)
    shi_ref[...] = jnp.where(hi, sin, 0.0)


def _rotary(x, ct, s_lo, s_hi):
    width = x.shape[1]
    return (x * ct + pltpu.roll(x, width - ROPE_HALF, 1) * s_lo
            + pltpu.roll(x, ROPE_HALF, 1) * s_hi)


def _rope_tables(positions):
    t = positions.size
    inv_freq = ROPE_BASE ** (-jnp.arange(0, MLA_ROPE_DIM, 2, dtype=F32) / MLA_ROPE_DIM)
    invf = jnp.zeros((1, LANES), F32).at[0, MLA_NOPE_DIM:MLA_NOPE_DIM + MLA_ROPE_DIM].set(
        jnp.concatenate([inv_freq, inv_freq]))
    return pl.pallas_call(
        _rope_table_kernel,
        grid=(t // ROW_TILE,),
        in_specs=[_row_spec(ROW_TILE, 1), _full_spec((1, LANES))],
        out_specs=[_row_spec(ROW_TILE, LANES)] * 3,
        out_shape=[jax.ShapeDtypeStruct((t, LANES), F32)] * 3,
        compiler_params=_params(1),
        name="rope_tables",
    )(positions.reshape(t, 1), invf)


def _mem_kv_kernel(mem_ref, w_ref, k_ref, v_ref):
    kv = _dot(mem_ref[0].astype(BF16), w_ref[0])
    k_ref[0, 0] = (kv[:, :BRANCH_WIDTH] * (MEM_HEAD_DIM ** -0.5 * LOG2E)).astype(BF16)
    v_ref[0, 0] = kv[:, BRANCH_WIDTH:].astype(BF16)


def _mem_kv(mem, w_mem_kv):
    depth = w_mem_kv.shape[0]
    batch, n_mem, _ = mem.shape
    return pl.pallas_call(
        _mem_kv_kernel,
        grid=(depth, batch),
        in_specs=[pl.BlockSpec((1, n_mem, D_MODEL), lambda l, b: (b, 0, 0)),
                  pl.BlockSpec((1, D_MODEL, 2 * BRANCH_WIDTH), lambda l, b: (l, 0, 0))],
        out_specs=[pl.BlockSpec((1, 1, n_mem, BRANCH_WIDTH), lambda l, b: (l, b, 0, 0))] * 2,
        out_shape=[jax.ShapeDtypeStruct((depth, batch, n_mem, BRANCH_WIDTH), BF16)] * 2,
        compiler_params=_params(2),
        name="mem_kv",
    )(mem, w_mem_kv)


GATE_W = N_BRANCHES * D_MODEL
COL_FOX_Q = GATE_W
COL_FOX_K = COL_FOX_Q + BRANCH_WIDTH
COL_FOX_V = COL_FOX_K + BRANCH_WIDTH
COL_QMEM = COL_FOX_V + BRANCH_WIDTH
COL_CQ = COL_QMEM + BRANCH_WIDTH
COL_CKV = COL_CQ + MLA_Q_RANK
MAIN_W = COL_CKV + MLA_KV_RANK
GATE_CHUNK = 512
N_SPLIT = 3
FORGET_LANE = MLA_NOPE_DIM + MLA_ROPE_DIM


def _split3(x):
    hi = x.astype(BF16)
    r = x - hi.astype(F32)
    mid = r.astype(BF16)
    lo = (r - mid.astype(F32)).astype(BF16)
    return hi, mid, lo


def _mixer_inputs_kernel(*refs, entry_norm, tiles_per_seq):
    if entry_norm:
        x_ref, lng_ref, lnb_ref, *refs = refs
    else:
        x_ref, *refs = refs
    (wm_ref, ws_ref, bf_ref, pq_ref, pk_ref, oq_ref, ok_ref, ct_ref, slo_ref, shi_ref, gq_ref, gkv_ref,
     wq_ref, wk_ref, wvt_ref, mk_ref, mv_ref,
     gates_ref, oc_ref, fqt_ref, fka_ref, fvt_ref, mqt_ref, mka_ref, mvt_ref, *rest) = refs
    carry_ref = rest[-1]
    tm = x_ref.shape[0]

    @pl.when(pl.program_id(0) % tiles_per_seq == 0)
    def _():
        carry_ref[...] = jnp.zeros_like(carry_ref)

    h = x_ref[...]
    if entry_norm:
        h = _layer_norm(h, lng_ref[...], lnb_ref[...])
        rest[0][...] = h
    xb = h.astype(BF16)

    def mm(c0, n):
        return _dot(xb, wm_ref[:, c0:c0 + n])

    gate_cols = iter(range(0, GATE_W, GATE_CHUNK))

    def emit_gates(n_chunks=1):
        for _ in range(n_chunks):
            c = next(gate_cols, None)
            if c is not None:
                g = mm(c, GATE_CHUNK)
                gates_ref[:, c:c + GATE_CHUNK] = (1.0 / (1.0 + jnp.exp(-g))).astype(BF16)

    small = _dot(xb, ws_ref[...])
    xq = _rms_norm(mm(COL_CQ, MLA_Q_RANK), gq_ref[...]).astype(BF16)
    xkv = _rms_norm(mm(COL_CKV, MLA_KV_RANK), gkv_ref[...]).astype(BF16)

    x = small + bf_ref[...]
    logf = jnp.minimum(x, 0.0) - jnp.log1p(jnp.exp(-jnp.abs(x)))
    row = lax.broadcasted_iota(jnp.int32, (tm, tm), 0)
    col = lax.broadcasted_iota(jnp.int32, (tm, tm), 1)
    tri = jnp.where(col <= row, 1.0, 0.0).astype(BF16)
    hi, mid, lo = _split3(logf)
    cum = _dot(tri, hi) + _dot(tri, mid) + _dot(tri, lo) + carry_ref[...]
    carry_ref[...] = cum[tm - 1:tm, :]
    emit_gates()

    lane = lax.broadcasted_iota(jnp.int32, (tm, LANES), 1)
    span = lane - FORGET_LANE
    chi, cmid, clo = _split3(cum * LOG2E)
    pieces = jnp.where((span >= 0) & (span < FOX_HEADS), chi,
                       jnp.where((span >= FOX_HEADS) & (span < 2 * FOX_HEADS), cmid,
                                 jnp.where((span >= 2 * FOX_HEADS) & (span < 3 * FOX_HEADS), clo,
                                           jnp.zeros_like(chi))))
    left = lane < HALF
    heads_per_dot = MXU_WIDTH // HALF
    for h0 in range(0, FOX_HEADS, heads_per_dot):
        c0 = h0 * HALF
        qp2 = mm(COL_FOX_Q + c0, MXU_WIDTH) * (FOX_HEAD_DIM ** -0.5 * LOG2E)
        kp2 = mm(COL_FOX_K + c0, MXU_WIDTH)
        grp4 = slice(h0 * LANES, (h0 + heads_per_dot) * LANES)
        q_spare = _dot(pieces, pq_ref[:, grp4]) + oq_ref[:, grp4]
        k_spare = _dot(pieces, pk_ref[:, grp4]) + ok_ref[:, grp4]
        for i in range(heads_per_dot):
            head = h0 + i
            pair = slice((i // 2) * LANES, (i // 2 + 1) * LANES)
            loc = slice(i * LANES, (i + 1) * LANES)
            grp = slice(head * LANES, (head + 1) * LANES)
            own = left if head % 2 == 0 else jnp.logical_not(left)
            fqt_ref[0, grp, :] = jnp.where(own, qp2[:, pair], q_spare[:, loc]).T.astype(BF16)
            fka_ref[:, grp] = jnp.where(own, kp2[:, pair], k_spare[:, loc]).astype(BF16)
        emit_gates()
    fvt_ref[0] = mm(COL_FOX_V, BRANCH_WIDTH).T.astype(BF16)

    ct, s_lo, s_hi = ct_ref[...], slo_ref[...], shi_ref[...]
    groups_per_dot = MXU_WIDTH // LANES
    ct2, s_lo2, s_hi2 = (jnp.concatenate([tab] * groups_per_dot, axis=1) for tab in (ct, s_lo, s_hi))
    scale = (MLA_NOPE_DIM + MLA_ROPE_DIM) ** -0.5 * LOG2E
    kpe = _rotary(small, ct, s_lo, s_hi)
    kpe2 = jnp.concatenate([kpe] * groups_per_dot, axis=1)
    for n, c in enumerate(range(0, WIDE, MXU_WIDTH)):
        q = _rotary(_dot(xq, wq_ref[:, c:c + MXU_WIDTH]), ct2, s_lo2, s_hi2) * scale
        mqt_ref[0, c:c + MXU_WIDTH, :] = q.T.astype(BF16)
        mka_ref[:, c:c + MXU_WIDTH] = (_dot(xkv, wk_ref[:, c:c + MXU_WIDTH]) + kpe2).astype(BF16)
        if n % 2 == 1:
            emit_gates()
    mvt_ref[0] = _dot_nt(wvt_ref[...], xkv).astype(BF16)

    qm = mm(COL_QMEM, BRANCH_WIDTH).astype(BF16)
    for hm in range(MEM_HEADS):
        grp = slice(hm * MEM_HEAD_DIM, (hm + 1) * MEM_HEAD_DIM)
        s = _dot_nt(qm[:, grp], mk_ref[0, 0, :, grp])
        e = jnp.exp2(s - jnp.max(s, axis=1, keepdims=True))
        o = _dot(e.astype(BF16), mv_ref[0, 0, :, grp])
        oc_ref[:, grp] = (o * (1.0 / jnp.sum(e, axis=1, keepdims=True))).astype(BF16)
    emit_gates(GATE_W // GATE_CHUNK)


def _fox_placement():
    pq = np.zeros((LANES, WIDE), np.float32)
    pk = np.zeros((LANES, WIDE), np.float32)
    oq = np.zeros((1, WIDE), np.float32)
    ok = np.zeros((1, WIDE), np.float32)
    for h in range(FOX_HEADS):
        base = h * LANES + (HALF if h % 2 == 0 else 0)
        for j in range(N_SPLIT):
            src = FORGET_LANE + j * FOX_HEADS + h
            pq[src, base + j] = 1.0
            ok[0, base + j] = 1.0
            oq[0, base + N_SPLIT + j] = 1.0
            pk[src, base + N_SPLIT + j] = -1.0
    return (jnp.asarray(pq, BF16), jnp.asarray(pk, BF16), jnp.asarray(oq), jnp.asarray(ok))


def _mixer_inputs(h, ln_in, w_main, w_small, b_forget3, rope_tabs, g_cq, g_ckv, wq, wk, wvt,
                  mk, mv, layer, batch, seq):
    t = h.shape[0]
    nt = seq // ROW_TILE
    n_mem = mk.shape[2]
    entry_norm = ln_in is not None
    pq, pk, oq, ok = _fox_placement()
    mem_spec = pl.BlockSpec((1, 1, n_mem, BRANCH_WIDTH), lambda i: (layer, i // nt, 0, 0))

    def transposed(rows):
        return (pl.BlockSpec((1, rows, ROW_TILE), lambda i: (i // nt, 0, i % nt)),
                jax.ShapeDtypeStruct((batch, rows, seq), BF16))

    def row_major(width):
        return _row_spec(ROW_TILE, width), jax.ShapeDtypeStruct((t, width), BF16)

    operands = [h]
    in_specs = [_row_spec(ROW_TILE, D_MODEL)]
    if entry_norm:
        operands += [ln_in[0].reshape(1, -1), ln_in[1].reshape(1, -1)]
        in_specs += [_full_spec((1, D_MODEL))] * 2
    operands += [w_main, w_small, b_forget3, pq, pk, oq, ok, *rope_tabs,
                 g_cq.reshape(1, -1), g_ckv.reshape(1, -1), wq, wk, wvt, mk, mv]
    in_specs += [_full_spec((D_MODEL, MAIN_W)), _full_spec((D_MODEL, LANES)), _full_spec((1, LANES)),
                 _full_spec((LANES, WIDE)), _full_spec((LANES, WIDE)), _full_spec((1, WIDE)),
                 _full_spec((1, WIDE))] + [_row_spec(ROW_TILE, LANES)] * 3 + [
                 _full_spec((1, MLA_Q_RANK)), _full_spec((1, MLA_KV_RANK)),
                 _full_spec((MLA_Q_RANK, WIDE)), _full_spec((MLA_KV_RANK, WIDE)),
                 _full_spec((BRANCH_WIDTH, MLA_KV_RANK)), mem_spec, mem_spec]

    outs = [row_major(GATE_W), row_major(BRANCH_WIDTH),
            transposed(WIDE), row_major(WIDE), transposed(BRANCH_WIDTH),
            transposed(WIDE), row_major(WIDE), transposed(BRANCH_WIDTH)]
    out_specs = [spec for spec, _ in outs]
    out_shape = [shape for _, shape in outs]
    if entry_norm:
        out_specs.append(_row_spec(ROW_TILE, D_MODEL))
        out_shape.append(jax.ShapeDtypeStruct((t, D_MODEL), F32))
    return pl.pallas_call(
        functools.partial(_mixer_inputs_kernel, entry_norm=entry_norm, tiles_per_seq=nt),
        grid=(t // ROW_TILE,),
        in_specs=in_specs,
        out_specs=out_specs,
        out_shape=out_shape,
        scratch_shapes=[pltpu.VMEM((1, LANES), F32)],
        compiler_params=_params(1),
        name="mixer_inputs",
    )(*operands)


def _attn_kernel(qt_ref, k_ref, vt_ref, o_ref, sa_ref, sb_ref, m_ref, acc_ref, *, granule_shift):
    tq = qt_ref.shape[2]
    tk = tq
    qi = pl.program_id(1)
    heads = range(N_GROUPS)

    m_ref[...] = jnp.full(m_ref.shape, NEG_INF, F32)
    acc_ref[...] = jnp.zeros(acc_ref.shape, F32)
    ones_rows = jnp.ones((SUM_ROWS, tk), BF16)

    def scores(j, s_ref, h):
        ks = pl.multiple_of(j * tk, tk)
        grp = slice(h * LANES, (h + 1) * LANES)
        s_ref[h] = _dot(k_ref[0, pl.ds(ks, tk), grp], qt_ref[0, grp, :])

    def update(j, s_ref, h, masked):
        ks = pl.multiple_of(j * tk, tk)
        s = s_ref[h]
        if masked:
            key = lax.broadcasted_iota(jnp.int32, (tk, tq), 0)
            qry = lax.broadcasted_iota(jnp.int32, (tk, tq), 1)
            s = jnp.where((key >> granule_shift) <= (qry >> granule_shift), s, NEG_INF)
        m = m_ref[h]
        m_new = jnp.maximum(m, jnp.max(s, axis=0, keepdims=True))
        alpha = jnp.exp2(m - m_new)
        e = jnp.exp2(s - m_new)
        m_ref[h] = m_new
        v_t = jnp.concatenate([vt_ref[0, h * HALF:(h + 1) * HALF, pl.ds(ks, tk)], ones_rows],
                              axis=0)
        acc_ref[h] = alpha * acc_ref[h] + _dot(v_t, e.astype(BF16))

    def tile_step(j, cur_ref, masked, next_ref=None):
        if next_ref is not None:
            for h in range(SCORE_LEAD):
                scores(j + 1, next_ref, h)
        for h in heads:
            update(j, cur_ref, h, masked)
            if next_ref is not None and h + SCORE_LEAD < N_GROUPS:
                scores(j + 1, next_ref, h + SCORE_LEAD)

    for h in heads:
        scores(0, sa_ref, h)

    def two_tiles(i, _):
        tile_step(2 * i, sa_ref, False, sb_ref)
        tile_step(2 * i + 1, sb_ref, False, sa_ref)
        return 0

    lax.fori_loop(0, qi // 2, two_tiles, 0)
    j_last = 2 * (qi // 2)

    @pl.when(qi % 2 == 1)
    def _():
        tile_step(j_last, sa_ref, False, sb_ref)
        tile_step(j_last + 1, sb_ref, True)

    @pl.when(qi % 2 == 0)
    def _():
        tile_step(j_last, sa_ref, True)

    for pair in range(N_GROUPS // 2):
        halves = []
        for h in (2 * pair, 2 * pair + 1):
            halves.append(acc_ref[h, :HALF, :] * (1.0 / acc_ref[h, HALF:HALF + 1, :]))
        o_t = jnp.concatenate(halves, axis=0)
        o_ref[0, :, pair * LANES:(pair + 1) * LANES] = o_t.T.astype(BF16)


def _causal_attention(q_t, k_aug, v_t, batch, seq, granule):
    tile = ATTN_TILE
    k3 = k_aug.reshape(batch, seq, WIDE)
    out = pl.pallas_call(
        functools.partial(_attn_kernel, granule_shift=granule.bit_length() - 1),
        grid=(batch, seq // tile),
        in_specs=[pl.BlockSpec((1, WIDE, tile), lambda b, i: (b, 0, i)),
                  pl.BlockSpec((1, seq, WIDE), lambda b, i: (b, 0, 0)),
                  pl.BlockSpec((1, BRANCH_WIDTH, seq), lambda b, i: (b, 0, 0))],
        out_specs=pl.BlockSpec((1, tile, BRANCH_WIDTH), lambda b, i: (b, i, 0)),
        out_shape=jax.ShapeDtypeStruct((batch, seq, BRANCH_WIDTH), BF16),
        scratch_shapes=[pltpu.VMEM((N_GROUPS, tile, tile), F32),
                        pltpu.VMEM((N_GROUPS, tile, tile), F32),
                        pltpu.VMEM((N_GROUPS, 1, tile), F32),
                        pltpu.VMEM((N_GROUPS, HALF + SUM_ROWS, tile), F32)],
        compiler_params=_params(2),
        name="causal_attention_g%d" % granule,
    )(q_t, k3, v_t)
    return out.reshape(batch * seq, BRANCH_WIDTH)


FF_CHUNK = 1024


def _merge_ffn_kernel(oa_ref, ob_ref, oc_ref, gates_ref, h_ref, wbr_ref, wout_ref, g1_ref, b1_ref,
                      w1_ref, w2_ref, g2_ref, b2_ref, o_ref, *, alpha):
    merged = None
    for n, br_ref in enumerate((oa_ref, ob_ref, oc_ref)):
        gate = gates_ref[:, n * D_MODEL:(n + 1) * D_MODEL].astype(F32)
        term = gate * _dot(br_ref[...], wbr_ref[n])
        merged = term if merged is None else merged + term
    y = _dot(merged.astype(BF16), wout_ref[...])
    h1 = _layer_norm(alpha * h_ref[...] + y, g1_ref[...], b1_ref[...])
    xb = h1.astype(BF16)
    acc = alpha * h1
    for c in range(0, D_FF, FF_CHUNK):
        u = jnp.maximum(_dot(xb, w1_ref[:, c:c + FF_CHUNK]), 0.0)
        acc = acc + _dot((u * u).astype(BF16), w2_ref[c:c + FF_CHUNK, :])
    o_ref[...] = _layer_norm(acc, g2_ref[...], b2_ref[...])


def _merge_ffn(oa, ob, oc, gates, h, w_br, w_out, g1, b1, w1, w2, g2, b2, alpha):
    t = h.shape[0]
    br = _row_spec(ROW_TILE, BRANCH_WIDTH)
    vec = _full_spec((1, D_MODEL))
    return pl.pallas_call(
        functools.partial(_merge_ffn_kernel, alpha=alpha),
        grid=(t // ROW_TILE,),
        in_specs=[br, br, br, _row_spec(ROW_TILE, GATE_W), _row_spec(ROW_TILE, D_MODEL),
                  _full_spec((N_BRANCHES, BRANCH_WIDTH, D_MODEL)), _full_spec((D_MODEL, D_MODEL)),
                  vec, vec, _full_spec((D_MODEL, D_FF)), _full_spec((D_FF, D_MODEL)), vec, vec],
        out_specs=_row_spec(ROW_TILE, D_MODEL),
        out_shape=jax.ShapeDtypeStruct((t, D_MODEL), F32),
        compiler_params=_params(1),
        name="merge_ffn",
    )(oa, ob, oc, gates, h, w_br, w_out, g1.reshape(1, -1), b1.reshape(1, -1), w1, w2,
      g2.reshape(1, -1), b2.reshape(1, -1))


def _layer_weights(w_in, b_forget, w_uq, w_ukv):
    depth = w_in.shape[0]
    fox_w = 3 * BRANCH_WIDTH
    tail = LANES - FORGET_LANE - N_SPLIT * FOX_HEADS
    splits = np.cumsum([fox_w, FOX_HEADS, MLA_Q_RANK, MLA_KV_RANK, MLA_ROPE_DIM, BRANCH_WIDTH])
    fox, f_log, c_q, c_kv, k_rope, q_mem, gate = jnp.split(
        w_in.astype(BF16), [int(s) for s in splits], axis=-1)
    w_main = jnp.concatenate([gate, fox, q_mem, c_q, c_kv], axis=-1)
    w_small = jnp.pad(jnp.concatenate([k_rope] + [f_log] * N_SPLIT, axis=-1),
                      ((0, 0), (0, 0), (MLA_NOPE_DIM, tail)))
    b3 = jnp.pad(jnp.concatenate([b_forget] * N_SPLIT, axis=-1),
                 ((0, 0), (FORGET_LANE, tail))).reshape(depth, 1, LANES)

    dq = MLA_NOPE_DIM + MLA_ROPE_DIM
    wq = w_uq.astype(BF16).reshape(depth, MLA_Q_RANK, MLA_HEADS, dq)
    wq = jnp.pad(wq, ((0, 0), (0, 0), (0, 0), (0, LANES - dq))).reshape(depth, MLA_Q_RANK, -1)
    wkv = w_ukv.astype(BF16).reshape(depth, MLA_KV_RANK, MLA_HEADS, MLA_NOPE_DIM + MLA_V_DIM)
    wk = jnp.pad(wkv[..., :MLA_NOPE_DIM], ((0, 0), (0, 0), (0, 0), (0, LANES - MLA_NOPE_DIM)))
    wk = wk.reshape(depth, MLA_KV_RANK, -1)
    wv_t = jnp.swapaxes(wkv[..., MLA_NOPE_DIM:].reshape(depth, MLA_KV_RANK, -1), 1, 2)
    return w_main, w_small, b3, wq, wk, wv_t


def kernel(x, mem, positions, ln_in_g, ln_in_b, w_in, b_forget, w_uq, g_cq, w_ukv, g_ckv,
           w_mem_kv, w_br, w_out, ln1_g, ln1_b, w_ff1, w_ff2, ln2_g, ln2_b):
    batch, seq, _ = x.shape
    depth = w_in.shape[0]
    assert seq % ATTN_TILE == 0 and seq % ROW_TILE == 0
    alpha = float((2 * depth) ** 0.25)

    w_main, w_small, b3, wq, wk, wv_t = _layer_weights(w_in, b_forget, w_uq, w_ukv)
    w_br_b = w_br.astype(BF16)
    w_out_b = w_out.astype(BF16)
    w_ff1_b = w_ff1.astype(BF16)
    w_ff2_b = w_ff2.astype(BF16)

    rope_tabs = _rope_tables(positions)
    mk, mv = _mem_kv(mem, w_mem_kv.astype(BF16))
    h = x.reshape(batch * seq, D_MODEL)
    for l in range(depth):
        outs = _mixer_inputs(h, (ln_in_g, ln_in_b) if l == 0 else None, w_main[l], w_small[l], b3[l],
                             rope_tabs, g_cq[l], g_ckv[l], wq[l], wk[l], wv_t[l], mk, mv, l,
                             batch, seq)
        gates, o_c, fq_t, fk_aug, fv_t, mq_t, mk_aug, mv_t = outs[:8]
        if l == 0:
            h = outs[8]
        o_a = _causal_attention(fq_t, fk_aug, fv_t, batch, seq, 1)
        o_b = _causal_attention(mq_t, mk_aug, mv_t, batch, seq, CHUNK)
        h = _merge_ffn(o_a, o_b, o_c, gates, h, w_br_b[l], w_out_b[l], ln1_g[l], ln1_b[l],
                       w_ff1_b[l], w_ff2_b[l], ln2_g[l], ln2_b[l], alpha)
    return h.reshape(batch, seq, D_MODEL)
```

```python
import functools
import math

import numpy as np
import jax
import jax.numpy as jnp
from jax import lax
from jax.experimental import pallas as pl
from jax.experimental.pallas import tpu as pltpu

D_MODEL = 1024
CHUNK = 64
FOX_HEADS = 8
FOX_HEAD_DIM = 64
MLA_HEADS = 8
MLA_NOPE_DIM = 64
MLA_ROPE_DIM = 32
MLA_V_DIM = 64
MLA_Q_RANK = 384
MLA_KV_RANK = 256
ROPE_BASE = 10000.0
MEM_HEADS = 4
MEM_HEAD_DIM = 128
N_BRANCHES = 3
BRANCH_WIDTH = 512
D_FF = 4 * D_MODEL
LN_EPS = 1e-5
RMS_EPS = 1e-6
NEG_INF = -1e30
LOG2E = math.log2(math.e)

LANES = 128
HALF = 64
N_GROUPS = 8
WIDE = N_GROUPS * LANES
MXU_WIDTH = 256
VMEM_LIMIT = 56 * 1024 * 1024

ROW_TILE = 512
ATTN_TILE = 256
SCORE_LEAD = 2
SUM_ROWS = 16

F32 = jnp.float32
BF16 = jnp.bfloat16


def _params(n_axes):
    return pltpu.CompilerParams(
        dimension_semantics=("arbitrary",) * n_axes, vmem_limit_bytes=VMEM_LIMIT)


def _dot(a, b):
    return jnp.dot(a, b, preferred_element_type=F32)


def _dot_nt(a, b):
    return lax.dot_general(a, b, (((1,), (1,)), ((), ())), preferred_element_type=F32)


def _layer_norm(z, g, b):
    mu = jnp.mean(z, axis=-1, keepdims=True)
    d = z - mu
    var = jnp.mean(d * d, axis=-1, keepdims=True)
    return d * lax.rsqrt(var + LN_EPS) * g + b


def _rms_norm(z, g):
    return z * lax.rsqrt(jnp.mean(z * z, axis=-1, keepdims=True) + RMS_EPS) * g


def _row_spec(tile, width, col_block=0):
    return pl.BlockSpec((tile, width), lambda i: (i, col_block))


def _full_spec(shape):
    return pl.BlockSpec(shape, lambda *_: (0,) * len(shape), pipeline_mode=pl.Buffered(1))


ROPE_HALF = MLA_ROPE_DIM // 2


def _rope_table_kernel(pos_ref, invf_ref, c_ref, slo_ref, shi_ref):
    ang = pos_ref[...].astype(F32) * invf_ref[...]
    lane = lax.broadcasted_iota(jnp.int32, ang.shape, 1)
    lo = (lane >= MLA_NOPE_DIM) & (lane < MLA_NOPE_DIM + ROPE_HALF)
    hi = (lane >= MLA_NOPE_DIM + ROPE_HALF) & (lane < MLA_NOPE_DIM + MLA_ROPE_DIM)
    sin = jnp.sin(ang)
    c_ref[...] = jnp.where(lane < MLA_NOPE_DIM, 1.0, jnp.where(lo | hi, jnp.cos(ang), 0.0))
    slo_ref[...] = jnp.where(lo, -sin, 0.0)
    shi_ref[...] = jnp.where(hi, sin, 0.0)


def _rotary(x, ct, s_lo, s_hi):
    width = x.shape[1]
    return (x * ct + pltpu.roll(x, width - ROPE_HALF, 1) * s_lo
            + pltpu.roll(x, ROPE_HALF, 1) * s_hi)


def _rope_tables(positions):
    t = positions.size
    inv_freq = ROPE_BASE ** (-jnp.arange(0, MLA_ROPE_DIM, 2, dtype=F32) / MLA_ROPE_DIM)
    invf = jnp.zeros((1, LANES), F32).at[0, MLA_NOPE_DIM:MLA_NOPE_DIM + MLA_ROPE_DIM].set(
        jnp.concatenate([inv_freq, inv_freq]))
    return pl.pallas_call(
        _rope_table_kernel,
        grid=(t // ROW_TILE,),
        in_specs=[_row_spec(ROW_TILE, 1), _full_spec((1, LANES))],
        out_specs=[_row_spec(ROW_TILE, LANES)] * 3,
        out_shape=[jax.ShapeDtypeStruct((t, LANES), F32)] * 3,
        compiler_params=_params(1),
        name="rope_tables",
    )(positions.reshape(t, 1), invf)


def _mem_kv_kernel(mem_ref, w_ref, k_ref, v_ref):
    kv = _dot(mem_ref[0].astype(BF16), w_ref[0])
    k_ref[0, 0] = (kv[:, :BRANCH_WIDTH] * (MEM_HEAD_DIM ** -0.5 * LOG2E)).astype(BF16)
    v_ref[0, 0] = kv[:, BRANCH_WIDTH:].astype(BF16)


def _mem_kv(mem, w_mem_kv):
    depth = w_mem_kv.shape[0]
    batch, n_mem, _ = mem.shape
    return pl.pallas_call(
        _mem_kv_kernel,
        grid=(depth, batch),
        in_specs=[pl.BlockSpec((1, n_mem, D_MODEL), lambda l, b: (b, 0, 0)),
                  pl.BlockSpec((1, D_MODEL, 2 * BRANCH_WIDTH), lambda l, b: (l, 0, 0))],
        out_specs=[pl.BlockSpec((1, 1, n_mem, BRANCH_WIDTH), lambda l, b: (l, b, 0, 0))] * 2,
        out_shape=[jax.ShapeDtypeStruct((depth, batch, n_mem, BRANCH_WIDTH), BF16)] * 2,
        compiler_params=_params(2),
        name="mem_kv",
    )(mem, w_mem_kv)


GATE_W = N_BRANCHES * D_MODEL
COL_FOX_Q = 0
COL_FOX_K = COL_FOX_Q + BRANCH_WIDTH
COL_FOX_V = COL_FOX_K + BRANCH_WIDTH
COL_CQ = COL_FOX_V + BRANCH_WIDTH
COL_CKV = COL_CQ + MLA_Q_RANK
COL_QMEM = COL_CKV + MLA_KV_RANK
COL_GATE = COL_QMEM + BRANCH_WIDTH
MAIN_W = COL_GATE + GATE_W
GATE_CHUNK = 512
N_SPLIT = 3
FORGET_LANE = MLA_NOPE_DIM + MLA_ROPE_DIM


def _split3(x):
    hi = x.astype(BF16)
    r = x - hi.astype(F32)
    mid = r.astype(BF16)
    lo = (r - mid.astype(F32)).astype(BF16)
    return hi, mid, lo


def _mixer_inputs_kernel(*refs, entry_norm, tiles_per_seq):
    if entry_norm:
        x_ref, lng_ref, lnb_ref, *refs = refs
    else:
        x_ref, *refs = refs
    (wm_ref, ws_ref, bf_ref, pq_ref, pk_ref, oq_ref, ok_ref, ct_ref, slo_ref, shi_ref, gq_ref, gkv_ref,
     wq_ref, wk_ref, wvt_ref, mk_ref, mv_ref,
     gates_ref, oc_ref, fqt_ref, fka_ref, fvt_ref, mqt_ref, mka_ref, mvt_ref, *rest) = refs
    carry_ref = rest[-1]
    tm = x_ref.shape[0]

    @pl.when(pl.program_id(0) % tiles_per_seq == 0)
    def _():
        carry_ref[...] = jnp.zeros_like(carry_ref)

    h = x_ref[...]
    if entry_norm:
        h = _layer_norm(h, lng_ref[...], lnb_ref[...])
        rest[0][...] = h
    xb = h.astype(BF16)

    def mm(c0, n):
        return _dot(xb, wm_ref[:, c0:c0 + n])

    gate_cols = iter(range(0, GATE_W, GATE_CHUNK))

    def emit_gates(n_chunks=1):
        for _ in range(n_chunks):
            c = next(gate_cols, None)
            if c is not None:
                g = mm(COL_GATE + c, GATE_CHUNK)
                gates_ref[:, c:c + GATE_CHUNK] = (1.0 / (1.0 + jnp.exp(-g))).astype(BF16)

    small = _dot(xb, ws_ref[...])
    xq = _rms_norm(mm(COL_CQ, MLA_Q_RANK), gq_ref[...]).astype(BF16)
    xkv = _rms_norm(mm(COL_CKV, MLA_KV_RANK), gkv_ref[...]).astype(BF16)

    x = small + bf_ref[...]
    logf = jnp.minimum(x, 0.0) - jnp.log1p(jnp.exp(-jnp.abs(x)))
    row = lax.broadcasted_iota(jnp.int32, (tm, tm), 0)
    col = lax.broadcasted_iota(jnp.int32, (tm, tm), 1)
    tri = jnp.where(col <= row, 1.0, 0.0).astype(BF16)
    hi, mid, lo = _split3(logf)
    cum = _dot(tri, hi) + _dot(tri, mid) + _dot(tri, lo) + carry_ref[...]
    carry_ref[...] = cum[tm - 1:tm, :]
    emit_gates()

    lane = lax.broadcasted_iota(jnp.int32, (tm, LANES), 1)
    span = lane - FORGET_LANE
    chi, cmid, clo = _split3(cum * LOG2E)
    pieces = jnp.where((span >= 0) & (span < FOX_HEADS), chi,
                       jnp.where((span >= FOX_HEADS) & (span < 2 * FOX_HEADS), cmid,
                                 jnp.where((span >= 2 * FOX_HEADS) & (span < 3 * FOX_HEADS), clo,
                                           jnp.zeros_like(chi))))
    left = lane < HALF
    heads_per_dot = MXU_WIDTH // HALF
    for h0 in range(0, FOX_HEADS, heads_per_dot):
        c0 = h0 * HALF
        qp2 = mm(COL_FOX_Q + c0, MXU_WIDTH) * (FOX_HEAD_DIM ** -0.5 * LOG2E)
        kp2 = mm(COL_FOX_K + c0, MXU_WIDTH)
        grp4 = slice(h0 * LANES, (h0 + heads_per_dot) * LANES)
        q_spare = _dot(pieces, pq_ref[:, grp4]) + oq_ref[:, grp4]
        k_spare = _dot(pieces, pk_ref[:, grp4]) + ok_ref[:, grp4]
        for i in range(heads_per_dot):
            head = h0 + i
            pair = slice((i // 2) * LANES, (i // 2 + 1) * LANES)
            loc = slice(i * LANES, (i + 1) * LANES)
            grp = slice(head * LANES, (head + 1) * LANES)
            own = left if head % 2 == 0 else jnp.logical_not(left)
            fqt_ref[0, grp, :] = jnp.where(own, qp2[:, pair], q_spare[:, loc]).T.astype(BF16)
            fka_ref[:, grp] = jnp.where(own, kp2[:, pair], k_spare[:, loc]).astype(BF16)
        emit_gates()
    fvt_ref[0] = mm(COL_FOX_V, BRANCH_WIDTH).T.astype(BF16)

    ct, s_lo, s_hi = ct_ref[...], slo_ref[...], shi_ref[...]
    groups_per_dot = MXU_WIDTH // LANES
    ct2, s_lo2, s_hi2 = (jnp.concatenate([tab] * groups_per_dot, axis=1) for tab in (ct, s_lo, s_hi))
    scale = (MLA_NOPE_DIM + MLA_ROPE_DIM) ** -0.5 * LOG2E
    kpe = _rotary(small, ct, s_lo, s_hi)
    kpe2 = jnp.concatenate([kpe] * groups_per_dot, axis=1)
    for n, c in enumerate(range(0, WIDE, MXU_WIDTH)):
        q = _rotary(_dot(xq, wq_ref[:, c:c + MXU_WIDTH]), ct2, s_lo2, s_hi2) * scale
        mqt_ref[0, c:c + MXU_WIDTH, :] = q.T.astype(BF16)
        mka_ref[:, c:c + MXU_WIDTH] = (_dot(xkv, wk_ref[:, c:c + MXU_WIDTH]) + kpe2).astype(BF16)
        if n % 2 == 1:
            emit_gates()
    mvt_ref[0] = _dot_nt(wvt_ref[...], xkv).astype(BF16)

    qm = mm(COL_QMEM, BRANCH_WIDTH).astype(BF16)
    for hm in range(MEM_HEADS):
        grp = slice(hm * MEM_HEAD_DIM, (hm + 1) * MEM_HEAD_DIM)
        s = _dot_nt(qm[:, grp], mk_ref[0, 0, :, grp])
        e = jnp.exp2(s - jnp.max(s, axis=1, keepdims=True))
        o = _dot(e.astype(BF16), mv_ref[0, 0, :, grp])
        oc_ref[:, grp] = (o * (1.0 / jnp.sum(e, axis=1, keepdims=True))).astype(BF16)
    emit_gates(GATE_W // GATE_CHUNK)


def _fox_placement():
    pq = np.zeros((LANES, WIDE), np.float32)
    pk = np.zeros((LANES, WIDE), np.float32)
    oq = np.zeros((1, WIDE), np.float32)
    ok = np.zeros((1, WIDE), np.float32)
    for h in range(FOX_HEADS):
        base = h * LANES + (HALF if h % 2 == 0 else 0)
        for j in range(N_SPLIT):
            src = FORGET_LANE + j * FOX_HEADS + h
            pq[src, base + j] = 1.0
            ok[0, base + j] = 1.0
            oq[0, base + N_SPLIT + j] = 1.0
            pk[src, base + N_SPLIT + j] = -1.0
    return (jnp.asarray(pq, BF16), jnp.asarray(pk, BF16), jnp.asarray(oq), jnp.asarray(ok))


def _mixer_inputs(h, ln_in, w_main, w_small, b_forget3, rope_tabs, g_cq, g_ckv, wq, wk, wvt,
                  mk, mv, layer, batch, seq):
    t = h.shape[0]
    nt = seq // ROW_TILE
    n_mem = mk.shape[2]
    entry_norm = ln_in is not None
    pq, pk, oq, ok = _fox_placement()
    mem_spec = pl.BlockSpec((1, 1, n_mem, BRANCH_WIDTH), lambda i: (layer, i // nt, 0, 0))

    def transposed(rows):
        return (pl.BlockSpec((1, rows, ROW_TILE), lambda i: (i // nt, 0, i % nt)),
                jax.ShapeDtypeStruct((batch, rows, seq), BF16))

    def row_major(width):
        return _row_spec(ROW_TILE, width), jax.ShapeDtypeStruct((t, width), BF16)

    operands = [h]
    in_specs = [_row_spec(ROW_TILE, D_MODEL)]
    if entry_norm:
        operands += [ln_in[0].reshape(1, -1), ln_in[1].reshape(1, -1)]
        in_specs += [_full_spec((1, D_MODEL))] * 2
    operands += [w_main, w_small, b_forget3, pq, pk, oq, ok, *rope_tabs,
                 g_cq.reshape(1, -1), g_ckv.reshape(1, -1), wq, wk, wvt, mk, mv]
    in_specs += [_full_spec((D_MODEL, MAIN_W)), _full_spec((D_MODEL, LANES)), _full_spec((1, LANES)),
                 _full_spec((LANES, WIDE)), _full_spec((LANES, WIDE)), _full_spec((1, WIDE)),
                 _full_spec((1, WIDE))] + [_row_spec(ROW_TILE, LANES)] * 3 + [
                 _full_spec((1, MLA_Q_RANK)), _full_spec((1, MLA_KV_RANK)),
                 _full_spec((MLA_Q_RANK, WIDE)), _full_spec((MLA_KV_RANK, WIDE)),
                 _full_spec((BRANCH_WIDTH, MLA_KV_RANK)), mem_spec, mem_spec]

    outs = [row_major(GATE_W), row_major(BRANCH_WIDTH),
            transposed(WIDE), row_major(WIDE), transposed(BRANCH_WIDTH),
            transposed(WIDE), row_major(WIDE), transposed(BRANCH_WIDTH)]
    out_specs = [spec for spec, _ in outs]
    out_shape = [shape for _, shape in outs]
    if entry_norm:
        out_specs.append(_row_spec(ROW_TILE, D_MODEL))
        out_shape.append(jax.ShapeDtypeStruct((t, D_MODEL), F32))
    return pl.pallas_call(
        functools.partial(_mixer_inputs_kernel, entry_norm=entry_norm, tiles_per_seq=nt),
        grid=(t // ROW_TILE,),
        in_specs=in_specs,
        out_specs=out_specs,
        out_shape=out_shape,
        scratch_shapes=[pltpu.VMEM((1, LANES), F32)],
        compiler_params=_params(1),
        name="mixer_inputs",
    )(*operands)


def _attn_kernel(qt_ref, k_ref, vt_ref, o_ref, sa_ref, sb_ref, m_ref, acc_ref, *, granule_shift):
    tq = tk = sa_ref.shape[1]
    n_q_tiles = qt_ref.shape[2] // tq

    def q_tile(qi, _):
        _attn_q_tile(qi, n_q_tiles, qt_ref, k_ref, vt_ref, o_ref, sa_ref, sb_ref, m_ref, acc_ref,
                     tq=tq, tk=tk, granule_shift=granule_shift)
        return 0

    for h in range(N_GROUPS):
        sa_ref[h] = _dot(k_ref[0, :tk, h * LANES:(h + 1) * LANES], qt_ref[0, h * LANES:(h + 1) * LANES, :tq])
    lax.fori_loop(0, n_q_tiles, q_tile, 0)


def _attn_q_tile(qi, n_q_tiles, qt_ref, k_ref, vt_ref, o_ref, sa_ref, sb_ref, m_ref, acc_ref,
                 *, tq, tk, granule_shift):
    heads = range(N_GROUPS)
    q0 = pl.multiple_of(qi * tq, tq)

    m_ref[...] = jnp.full(m_ref.shape, NEG_INF, F32)
    acc_ref[...] = jnp.zeros(acc_ref.shape, F32)
    ones_rows = jnp.ones((SUM_ROWS, tk), BF16)

    def scores(j, s_ref, h, q_start=q0):
        ks = pl.multiple_of(j * tk, tk)
        grp = slice(h * LANES, (h + 1) * LANES)
        s_ref[h] = _dot(k_ref[0, pl.ds(ks, tk), grp], qt_ref[0, grp, pl.ds(q_start, tq)])

    def update(j, s_ref, h, masked):
        ks = pl.multiple_of(j * tk, tk)
        s = s_ref[h]
        if masked:
            key = lax.broadcasted_iota(jnp.int32, (tk, tq), 0)
            qry = lax.broadcasted_iota(jnp.int32, (tk, tq), 1)
            s = jnp.where((key >> granule_shift) <= (qry >> granule_shift), s, NEG_INF)
        m = m_ref[h]
        m_new = jnp.maximum(m, jnp.max(s, axis=0, keepdims=True))
        alpha = jnp.exp2(m - m_new)
        e = jnp.exp2(s - m_new)
        m_ref[h] = m_new
        v_t = jnp.concatenate([vt_ref[0, h * HALF:(h + 1) * HALF, pl.ds(ks, tk)], ones_rows],
                              axis=0)
        acc_ref[h] = alpha * acc_ref[h] + _dot(v_t, e.astype(BF16))

    def tile_step(j, cur_ref, masked, next_ref=None):
        if next_ref is not None:
            for h in range(SCORE_LEAD):
                scores(j + 1, next_ref, h)
        for h in heads:
            update(j, cur_ref, h, masked)
            if next_ref is not None and h + SCORE_LEAD < N_GROUPS:
                scores(j + 1, next_ref, h + SCORE_LEAD)

    def two_tiles(i, _):
        tile_step(2 * i, sa_ref, False, sb_ref)
        tile_step(2 * i + 1, sb_ref, False, sa_ref)
        return 0

    lax.fori_loop(0, qi // 2, two_tiles, 0)
    j_last = 2 * (qi // 2)

    @pl.when(qi % 2 == 1)
    def _():
        tile_step(j_last, sa_ref, False, sb_ref)
        tile_step(j_last + 1, sb_ref, True)

    @pl.when(qi % 2 == 0)
    def _():
        tile_step(j_last, sa_ref, True)

    q_next = pl.multiple_of(jnp.minimum(qi + 1, n_q_tiles - 1) * tq, tq)
    for h in heads:
        scores(0, sa_ref, h, q_next)

    for pair in range(N_GROUPS // 2):
        halves = []
        for h in (2 * pair, 2 * pair + 1):
            halves.append(acc_ref[h, :HALF, :] * (1.0 / acc_ref[h, HALF:HALF + 1, :]))
        o_t = jnp.concatenate(halves, axis=0)
        o_ref[0, pl.ds(q0, tq), pair * LANES:(pair + 1) * LANES] = o_t.T.astype(BF16)


def _causal_attention(q_t, k_aug, v_t, batch, seq, granule):
    tile = ATTN_TILE
    k3 = k_aug.reshape(batch, seq, WIDE)
    out = pl.pallas_call(
        functools.partial(_attn_kernel, granule_shift=granule.bit_length() - 1),
        grid=(batch,),
        in_specs=[pl.BlockSpec((1, WIDE, seq), lambda b: (b, 0, 0)),
                  pl.BlockSpec((1, seq, WIDE), lambda b: (b, 0, 0)),
                  pl.BlockSpec((1, BRANCH_WIDTH, seq), lambda b: (b, 0, 0))],
        out_specs=pl.BlockSpec((1, seq, BRANCH_WIDTH), lambda b: (b, 0, 0)),
        out_shape=jax.ShapeDtypeStruct((batch, seq, BRANCH_WIDTH), BF16),
        scratch_shapes=[pltpu.VMEM((N_GROUPS, tile, tile), F32),
                        pltpu.VMEM((N_GROUPS, tile, tile), F32),
                        pltpu.VMEM((N_GROUPS, 1, tile), F32),
                        pltpu.VMEM((N_GROUPS, HALF + SUM_ROWS, tile), F32)],
        compiler_params=_params(1),
        name="causal_attention_g%d" % granule,
    )(q_t, k3, v_t)
    return out.reshape(batch * seq, BRANCH_WIDTH)


FF_CHUNK = 1024


def _merge_ffn_kernel(oa_ref, ob_ref, oc_ref, gates_ref, h_ref, wbr_ref, wout_ref, g1_ref, b1_ref,
                      w1_ref, w2_ref, g2_ref, b2_ref, o_ref, *, alpha):
    merged = None
    for n, br_ref in enumerate((oa_ref, ob_ref, oc_ref)):
        gate = gates_ref[:, n * D_MODEL:(n + 1) * D_MODEL].astype(F32)
        term = gate * _dot(br_ref[...], wbr_ref[n])
        merged = term if merged is None else merged + term
    y = _dot(merged.astype(BF16), wout_ref[...])
    h1 = _layer_norm(alpha * h_ref[...] + y, g1_ref[...], b1_ref[...])
    xb = h1.astype(BF16)
    acc = alpha * h1
    for c in range(0, D_FF, FF_CHUNK):
        u = jnp.maximum(_dot(xb, w1_ref[:, c:c + FF_CHUNK]), 0.0)
        acc = acc + _dot((u * u).astype(BF16), w2_ref[c:c + FF_CHUNK, :])
    o_ref[...] = _layer_norm(acc, g2_ref[...], b2_ref[...])


def _merge_ffn(oa, ob, oc, gates, h, w_br, w_out, g1, b1, w1, w2, g2, b2, alpha):
    t = h.shape[0]
    br = _row_spec(ROW_TILE, BRANCH_WIDTH)
    vec = _full_spec((1, D_MODEL))
    return pl.pallas_call(
        functools.partial(_merge_ffn_kernel, alpha=alpha),
        grid=(t // ROW_TILE,),
        in_specs=[br, br, br, _row_spec(ROW_TILE, GATE_W), _row_spec(ROW_TILE, D_MODEL),
                  _full_spec((N_BRANCHES, BRANCH_WIDTH, D_MODEL)), _full_spec((D_MODEL, D_MODEL)),
                  vec, vec, _full_spec((D_MODEL, D_FF)), _full_spec((D_FF, D_MODEL)), vec, vec],
        out_specs=_row_spec(ROW_TILE, D_MODEL),
        out_shape=jax.ShapeDtypeStruct((t, D_MODEL), F32),
        compiler_params=_params(1),
        name="merge_ffn",
    )(oa, ob, oc, gates, h, w_br, w_out, g1.reshape(1, -1), b1.reshape(1, -1), w1, w2,
      g2.reshape(1, -1), b2.reshape(1, -1))


def _layer_weights(w_in, b_forget, w_uq, w_ukv):
    depth = w_in.shape[0]
    fox_w = 3 * BRANCH_WIDTH
    tail = LANES - FORGET_LANE - N_SPLIT * FOX_HEADS
    w_in_b = w_in.astype(BF16)
    f0 = fox_w
    f1 = f0 + FOX_HEADS
    r0 = f1 + MLA_Q_RANK + MLA_KV_RANK
    r1 = r0 + MLA_ROPE_DIM
    f_log, k_rope = w_in_b[..., f0:f1], w_in_b[..., r0:r1]
    w_main = jnp.concatenate([w_in_b[..., :f0], w_in_b[..., f1:r0], w_in_b[..., r1:]], axis=-1)
    w_small = jnp.pad(jnp.concatenate([k_rope] + [f_log] * N_SPLIT, axis=-1),
                      ((0, 0), (0, 0), (MLA_NOPE_DIM, tail)))
    b3 = jnp.pad(jnp.concatenate([b_forget] * N_SPLIT, axis=-1),
                 ((0, 0), (FORGET_LANE, tail))).reshape(depth, 1, LANES)

    dq = MLA_NOPE_DIM + MLA_ROPE_DIM
    wq = w_uq.astype(BF16).reshape(depth, MLA_Q_RANK, MLA_HEADS, dq)
    wq = jnp.pad(wq, ((0, 0), (0, 0), (0, 0), (0, LANES - dq))).reshape(depth, MLA_Q_RANK, -1)
    wkv = w_ukv.astype(BF16).reshape(depth, MLA_KV_RANK, MLA_HEADS, MLA_NOPE_DIM + MLA_V_DIM)
    wk = jnp.pad(wkv[..., :MLA_NOPE_DIM], ((0, 0), (0, 0), (0, 0), (0, LANES - MLA_NOPE_DIM)))
    wk = wk.reshape(depth, MLA_KV_RANK, -1)
    wv_t = jnp.swapaxes(wkv[..., MLA_NOPE_DIM:].reshape(depth, MLA_KV_RANK, -1), 1, 2)
    return w_main, w_small, b3, wq, wk, wv_t


def kernel(x, mem, positions, ln_in_g, ln_in_b, w_in, b_forget, w_uq, g_cq, w_ukv, g_ckv,
           w_mem_kv, w_br, w_out, ln1_g, ln1_b, w_ff1, w_ff2, ln2_g, ln2_b):
    batch, seq, _ = x.shape
    depth = w_in.shape[0]
    assert seq % ATTN_TILE == 0 and seq % ROW_TILE == 0
    alpha = float((2 * depth) ** 0.25)

    w_main, w_small, b3, wq, wk, wv_t = _layer_weights(w_in, b_forget, w_uq, w_ukv)
    w_br_b = w_br.astype(BF16)
    w_out_b = w_out.astype(BF16)
    w_ff1_b = w_ff1.astype(BF16)
    w_ff2_b = w_ff2.astype(BF16)

    rope_tabs = _rope_tables(positions)
    mk, mv = _mem_kv(mem, w_mem_kv.astype(BF16))
    h = x.reshape(batch * seq, D_MODEL)
    for l in range(depth):
        outs = _mixer_inputs(h, (ln_in_g, ln_in_b) if l == 0 else None, w_main[l], w_small[l], b3[l],
                             rope_tabs, g_cq[l], g_ckv[l], wq[l], wk[l], wv_t[l], mk, mv, l,
                             batch, seq)
        gates, o_c, fq_t, fk_aug, fv_t, mq_t, mk_aug, mv_t = outs[:8]
        if l == 0:
            h = outs[8]
        o_a = _causal_attention(fq_t, fk_aug, fv_t, batch, seq, 1)
        o_b = _causal_attention(mq_t, mk_aug, mv_t, batch, seq, CHUNK)
        h = _merge_ffn(o_a, o_b, o_c, gates, h, w_br_b[l], w_out_b[l], ln1_g[l], ln1_b[l],
                       w_ff1_b[l], w_ff2_b[l], ln2_g[l], ln2_b[l], alpha)
    return h.reshape(batch, seq, D_MODEL)
```

```python
import functools
import math

import numpy as np
import jax
import jax.numpy as jnp
from jax import lax
from jax.experimental import pallas as pl
from jax.experimental.pallas import tpu as pltpu

D_MODEL = 1024
CHUNK = 64
FOX_HEADS = 8
FOX_HEAD_DIM = 64
MLA_HEADS = 8
MLA_NOPE_DIM = 64
MLA_ROPE_DIM = 32
MLA_V_DIM = 64
MLA_Q_RANK = 384
MLA_KV_RANK = 256
ROPE_BASE = 10000.0
MEM_HEADS = 4
MEM_HEAD_DIM = 128
N_BRANCHES = 3
BRANCH_WIDTH = 512
D_FF = 4 * D_MODEL
LN_EPS = 1e-5
RMS_EPS = 1e-6
NEG_INF = -1e30
LOG2E = math.log2(math.e)

LANES = 128
HALF = 64
N_GROUPS = 8
WIDE = N_GROUPS * LANES
MXU_WIDTH = 256
VMEM_LIMIT = 56 * 1024 * 1024

ROW_TILE = 512
ATTN_TILE = 256
SCORE_LEAD = 2
SUM_ROWS = 16

F32 = jnp.float32
BF16 = jnp.bfloat16


def _params(n_axes):
    return pltpu.CompilerParams(
        dimension_semantics=("arbitrary",) * n_axes, vmem_limit_bytes=VMEM_LIMIT)


def _dot(a, b):
    return jnp.dot(a, b, preferred_element_type=F32)


def _dot_nt(a, b):
    return lax.dot_general(a, b, (((1,), (1,)), ((), ())), preferred_element_type=F32)


def _layer_norm(z, g, b):
    mu = jnp.mean(z, axis=-1, keepdims=True)
    d = z - mu
    var = jnp.mean(d * d, axis=-1, keepdims=True)
    return d * lax.rsqrt(var + LN_EPS) * g + b


def _rms_norm(z, g):
    return z * lax.rsqrt(jnp.mean(z * z, axis=-1, keepdims=True) + RMS_EPS) * g


def _row_spec(tile, width, col_block=0):
    return pl.BlockSpec((tile, width), lambda i: (i, col_block))


def _full_spec(shape):
    return pl.BlockSpec(shape, lambda *_: (0,) * len(shape), pipeline_mode=pl.Buffered(1))


ROPE_HALF = MLA_ROPE_DIM // 2


def _rope_table_kernel(pos_ref, invf_ref, c_ref, slo_ref, shi_ref):
    ang = pos_ref[...].astype(F32) * invf_ref[...]
    lane = lax.broadcasted_iota(jnp.int32, ang.shape, 1)
    lo = (lane >= MLA_NOPE_DIM) & (lane < MLA_NOPE_DIM + ROPE_HALF)
    hi = (lane >= MLA_NOPE_DIM + ROPE_HALF) & (lane < MLA_NOPE_DIM + MLA_ROPE_DIM)
    sin = jnp.sin(ang)
    c_ref[...] = jnp.where(lane < MLA_NOPE_DIM, 1.0, jnp.where(lo | hi, jnp.cos(ang), 0.0))
    slo_ref[...] = jnp.where(lo, -sin, 0.0)
    shi_ref[...] = jnp.where(hi, sin, 0.0)


def _rotary(x, ct, s_lo, s_hi):
    width = x.shape[1]
    return (x * ct + pltpu.roll(x, width - ROPE_HALF, 1) * s_lo
            + pltpu.roll(x, ROPE_HALF, 1) * s_hi)


def _rope_tables(positions):
    t = positions.size
    inv_freq = ROPE_BASE ** (-jnp.arange(0, MLA_ROPE_DIM, 2, dtype=F32) / MLA_ROPE_DIM)
    invf = jnp.zeros((1, LANES), F32).at[0, MLA_NOPE_DIM:MLA_NOPE_DIM + MLA_ROPE_DIM].set(
        jnp.concatenate([inv_freq, inv_freq]))
    return pl.pallas_call(
        _rope_table_kernel,
        grid=(t // ROW_TILE,),
        in_specs=[_row_spec(ROW_TILE, 1), _full_spec((1, LANES))],
        out_specs=[_row_spec(ROW_TILE, LANES)] * 3,
        out_shape=[jax.ShapeDtypeStruct((t, LANES), F32)] * 3,
        compiler_params=_params(1),
        name="rope_tables",
    )(positions.reshape(t, 1), invf)


def _mem_kv_kernel(mem_ref, w_ref, k_ref, v_ref):
    kv = _dot(mem_ref[0].astype(BF16), w_ref[0])
    k_ref[0, 0] = (kv[:, :BRANCH_WIDTH] * (MEM_HEAD_DIM ** -0.5 * LOG2E)).astype(BF16)
    v_ref[0, 0] = kv[:, BRANCH_WIDTH:].astype(BF16)


def _mem_kv(mem, w_mem_kv):
    depth = w_mem_kv.shape[0]
    batch, n_mem, _ = mem.shape
    return pl.pallas_call(
        _mem_kv_kernel,
        grid=(depth, batch),
        in_specs=[pl.BlockSpec((1, n_mem, D_MODEL), lambda l, b: (b, 0, 0)),
                  pl.BlockSpec((1, D_MODEL, 2 * BRANCH_WIDTH), lambda l, b: (l, 0, 0))],
        out_specs=[pl.BlockSpec((1, 1, n_mem, BRANCH_WIDTH), lambda l, b: (l, b, 0, 0))] * 2,
        out_shape=[jax.ShapeDtypeStruct((depth, batch, n_mem, BRANCH_WIDTH), BF16)] * 2,
        compiler_params=_params(2),
        name="mem_kv",
    )(mem, w_mem_kv)


GATE_W = N_BRANCHES * D_MODEL
COL_FOX_Q = 0
COL_FOX_K = COL_FOX_Q + BRANCH_WIDTH
COL_FOX_V = COL_FOX_K + BRANCH_WIDTH
COL_CQ = COL_FOX_V + BRANCH_WIDTH
COL_CKV = COL_CQ + MLA_Q_RANK
COL_QMEM = COL_CKV + MLA_KV_RANK
COL_GATE = COL_QMEM + BRANCH_WIDTH
MAIN_W = COL_GATE + GATE_W
GATE_CHUNK = 512
N_SPLIT = 3
FORGET_LANE = MLA_NOPE_DIM + MLA_ROPE_DIM


def _split3(x):
    hi = x.astype(BF16)
    r = x - hi.astype(F32)
    mid = r.astype(BF16)
    lo = (r - mid.astype(F32)).astype(BF16)
    return hi, mid, lo


def _mixer_inputs_kernel(*refs, entry_norm, tiles_per_seq):
    if entry_norm:
        x_ref, lng_ref, lnb_ref, *refs = refs
    else:
        x_ref, *refs = refs
    (wm_ref, ws_ref, bf_ref, pq_ref, pk_ref, oq_ref, ok_ref, ct_ref, slo_ref, shi_ref, gq_ref, gkv_ref,
     wq_ref, wk_ref, wvt_ref, mk_ref, mv_ref,
     gates_ref, oc_ref, fqt_ref, fka_ref, fvt_ref, mqt_ref, mka_ref, mvt_ref, *rest) = refs
    carry_ref = rest[-1]
    tm = x_ref.shape[0]

    @pl.when(pl.program_id(0) % tiles_per_seq == 0)
    def _():
        carry_ref[...] = jnp.zeros_like(carry_ref)

    h = x_ref[...]
    if entry_norm:
        h = _layer_norm(h, lng_ref[...], lnb_ref[...])
        rest[0][...] = h
    xb = h.astype(BF16)

    def mm(c0, n):
        return _dot(xb, wm_ref[:, c0:c0 + n])

    gate_cols = iter(range(0, GATE_W, GATE_CHUNK))

    def emit_gates(n_chunks=1):
        for _ in range(n_chunks):
            c = next(gate_cols, None)
            if c is not None:
                g = mm(COL_GATE + c, GATE_CHUNK)
                gates_ref[:, c:c + GATE_CHUNK] = (1.0 / (1.0 + jnp.exp(-g))).astype(BF16)

    small = _dot(xb, ws_ref[...])
    xq = _rms_norm(mm(COL_CQ, MLA_Q_RANK), gq_ref[...]).astype(BF16)
    xkv = _rms_norm(mm(COL_CKV, MLA_KV_RANK), gkv_ref[...]).astype(BF16)

    x = small + bf_ref[...]
    logf = jnp.minimum(x, 0.0) - jnp.log1p(jnp.exp(-jnp.abs(x)))
    row = lax.broadcasted_iota(jnp.int32, (tm, tm), 0)
    col = lax.broadcasted_iota(jnp.int32, (tm, tm), 1)
    tri = jnp.where(col <= row, 1.0, 0.0).astype(BF16)
    hi, mid, lo = _split3(logf)
    cum = _dot(tri, hi) + _dot(tri, mid) + _dot(tri, lo) + carry_ref[...]
    carry_ref[...] = cum[tm - 1:tm, :]
    emit_gates()

    lane = lax.broadcasted_iota(jnp.int32, (tm, LANES), 1)
    span = lane - FORGET_LANE
    chi, cmid, clo = _split3(cum * LOG2E)
    pieces = jnp.where((span >= 0) & (span < FOX_HEADS), chi,
                       jnp.where((span >= FOX_HEADS) & (span < 2 * FOX_HEADS), cmid,
                                 jnp.where((span >= 2 * FOX_HEADS) & (span < 3 * FOX_HEADS), clo,
                                           jnp.zeros_like(chi))))
    left = lane < HALF
    heads_per_dot = MXU_WIDTH // HALF
    for h0 in range(0, FOX_HEADS, heads_per_dot):
        c0 = h0 * HALF
        qp2 = mm(COL_FOX_Q + c0, MXU_WIDTH) * (FOX_HEAD_DIM ** -0.5 * LOG2E)
        kp2 = mm(COL_FOX_K + c0, MXU_WIDTH)
        grp4 = slice(h0 * LANES, (h0 + heads_per_dot) * LANES)
        q_spare = _dot(pieces, pq_ref[:, grp4]) + oq_ref[:, grp4]
        k_spare = _dot(pieces, pk_ref[:, grp4]) + ok_ref[:, grp4]
        for i in range(heads_per_dot):
            head = h0 + i
            pair = slice((i // 2) * LANES, (i // 2 + 1) * LANES)
            loc = slice(i * LANES, (i + 1) * LANES)
            grp = slice(head * LANES, (head + 1) * LANES)
            own = left if head % 2 == 0 else jnp.logical_not(left)
            fqt_ref[0, grp, :] = jnp.where(own, qp2[:, pair], q_spare[:, loc]).T.astype(BF16)
            fka_ref[:, grp] = jnp.where(own, kp2[:, pair], k_spare[:, loc]).astype(BF16)
        emit_gates()
    fvt_ref[0] = mm(COL_FOX_V, BRANCH_WIDTH).T.astype(BF16)

    ct, s_lo, s_hi = ct_ref[...], slo_ref[...], shi_ref[...]
    groups_per_dot = MXU_WIDTH // LANES
    ct2, s_lo2, s_hi2 = (jnp.concatenate([tab] * groups_per_dot, axis=1) for tab in (ct, s_lo, s_hi))
    scale = (MLA_NOPE_DIM + MLA_ROPE_DIM) ** -0.5 * LOG2E
    kpe = _rotary(small, ct, s_lo, s_hi)
    kpe2 = jnp.concatenate([kpe] * groups_per_dot, axis=1)
    for n, c in enumerate(range(0, WIDE, MXU_WIDTH)):
        q = _rotary(_dot(xq, wq_ref[:, c:c + MXU_WIDTH]), ct2, s_lo2, s_hi2) * scale
        mqt_ref[0, c:c + MXU_WIDTH, :] = q.T.astype(BF16)
        mka_ref[:, c:c + MXU_WIDTH] = (_dot(xkv, wk_ref[:, c:c + MXU_WIDTH]) + kpe2).astype(BF16)
        if n % 2 == 1:
            emit_gates()
    mvt_ref[0] = _dot_nt(wvt_ref[...], xkv).astype(BF16)

    qm = mm(COL_QMEM, BRANCH_WIDTH).astype(BF16)
    for hm in range(MEM_HEADS):
        grp = slice(hm * MEM_HEAD_DIM, (hm + 1) * MEM_HEAD_DIM)
        s = _dot_nt(qm[:, grp], mk_ref[0, 0, :, grp])
        e = jnp.exp2(s - jnp.max(s, axis=1, keepdims=True))
        o = _dot(e.astype(BF16), mv_ref[0, 0, :, grp])
        oc_ref[:, grp] = (o * (1.0 / jnp.sum(e, axis=1, keepdims=True))).astype(BF16)
    emit_gates(GATE_W // GATE_CHUNK)


def _fox_placement():
    pq = np.zeros((LANES, WIDE), np.float32)
    pk = np.zeros((LANES, WIDE), np.float32)
    oq = np.zeros((1, WIDE), np.float32)
    ok = np.zeros((1, WIDE), np.float32)
    for h in range(FOX_HEADS):
        base = h * LANES + (HALF if h % 2 == 0 else 0)
        for j in range(N_SPLIT):
            src = FORGET_LANE + j * FOX_HEADS + h
            pq[src, base + j] = 1.0
            ok[0, base + j] = 1.0
            oq[0, base + N_SPLIT + j] = 1.0
            pk[src, base + N_SPLIT + j] = -1.0
    return (jnp.asarray(pq, BF16), jnp.asarray(pk, BF16), jnp.asarray(oq), jnp.asarray(ok))


def _mixer_inputs(h, ln_in, w_main, w_small, b_forget3, rope_tabs, g_cq, g_ckv, wq, wk, wvt,
                  mk, mv, layer, batch, seq):
    t = h.shape[0]
    nt = seq // ROW_TILE
    n_mem = mk.shape[2]
    entry_norm = ln_in is not None
    pq, pk, oq, ok = _fox_placement()
    mem_spec = pl.BlockSpec((1, 1, n_mem, BRANCH_WIDTH), lambda i: (layer, i // nt, 0, 0))

    def transposed(rows):
        return (pl.BlockSpec((1, rows, ROW_TILE), lambda i: (i // nt, 0, i % nt)),
                jax.ShapeDtypeStruct((batch, rows, seq), BF16))

    def row_major(width):
        return _row_spec(ROW_TILE, width), jax.ShapeDtypeStruct((t, width), BF16)

    operands = [h]
    in_specs = [_row_spec(ROW_TILE, D_MODEL)]
    if entry_norm:
        operands += [ln_in[0].reshape(1, -1), ln_in[1].reshape(1, -1)]
        in_specs += [_full_spec((1, D_MODEL))] * 2
    operands += [w_main, w_small, b_forget3, pq, pk, oq, ok, *rope_tabs,
                 g_cq.reshape(1, -1), g_ckv.reshape(1, -1), wq, wk, wvt, mk, mv]
    in_specs += [_full_spec((D_MODEL, MAIN_W)), _full_spec((D_MODEL, LANES)), _full_spec((1, LANES)),
                 _full_spec((LANES, WIDE)), _full_spec((LANES, WIDE)), _full_spec((1, WIDE)),
                 _full_spec((1, WIDE))] + [_row_spec(ROW_TILE, LANES)] * 3 + [
                 _full_spec((1, MLA_Q_RANK)), _full_spec((1, MLA_KV_RANK)),
                 _full_spec((MLA_Q_RANK, WIDE)), _full_spec((MLA_KV_RANK, WIDE)),
                 _full_spec((BRANCH_WIDTH, MLA_KV_RANK)), mem_spec, mem_spec]

    outs = [row_major(GATE_W), row_major(BRANCH_WIDTH),
            transposed(WIDE), row_major(WIDE), transposed(BRANCH_WIDTH),
            transposed(WIDE), row_major(WIDE), transposed(BRANCH_WIDTH)]
    out_specs = [spec for spec, _ in outs]
    out_shape = [shape for _, shape in outs]
    if entry_norm:
        out_specs.append(_row_spec(ROW_TILE, D_MODEL))
        out_shape.append(jax.ShapeDtypeStruct((t, D_MODEL), F32))
    return pl.pallas_call(
        functools.partial(_mixer_inputs_kernel, entry_norm=entry_norm, tiles_per_seq=nt),
        grid=(t // ROW_TILE,),
        in_specs=in_specs,
        out_specs=out_specs,
        out_shape=out_shape,
        scratch_shapes=[pltpu.VMEM((1, LANES), F32)],
        compiler_params=_params(1),
        name="mixer_inputs",
    )(*operands)


def _attn_kernel(qt_ref, k_ref, vt_ref, o_ref, sa_ref, sb_ref, ta_ref, tb_ref, m_ref, acc_ref,
                 *, granule_shift):
    tq = tk = sa_ref.shape[1]
    n_q_tiles = qt_ref.shape[2] // tq
    heads = range(N_GROUPS)
    buf_a = (sa_ref, ta_ref)
    buf_b = (sb_ref, tb_ref)
    ones_rows = jnp.ones((SUM_ROWS, tk), BF16)

    def start_of(tile_index, size):
        start = tile_index * size
        return start if isinstance(start, int) else pl.multiple_of(start, size)

    def scores(qi, j, buf, h, masked):
        q0 = start_of(qi, tq)
        ks = start_of(j, tk)
        grp = slice(h * LANES, (h + 1) * LANES)
        s = _dot(k_ref[0, pl.ds(ks, tk), grp], qt_ref[0, grp, pl.ds(q0, tq)])
        if masked:
            key = lax.broadcasted_iota(jnp.int32, (tk, tq), 0)
            qry = lax.broadcasted_iota(jnp.int32, (tk, tq), 1)
            s = jnp.where((key >> granule_shift) <= (qry >> granule_shift), s, NEG_INF)
        buf[0][h] = s
        buf[1][h] = jnp.max(s, axis=0, keepdims=True)

    def update(j, buf, h):
        ks = start_of(j, tk)
        m = m_ref[h]
        m_new = jnp.maximum(m, buf[1][h])
        alpha = jnp.exp2(m - m_new)
        e = jnp.exp2(buf[0][h] - m_new)
        m_ref[h] = m_new
        v_t = jnp.concatenate([vt_ref[0, h * HALF:(h + 1) * HALF, pl.ds(ks, tk)], ones_rows],
                              axis=0)
        acc_ref[h] = alpha * acc_ref[h] + _dot(v_t, e.astype(BF16))

    def tile_step(qi, j, cur, nxt=None, next_masked=False):
        if nxt is not None:
            for h in range(SCORE_LEAD):
                scores(qi, j + 1, nxt, h, next_masked)
        for h in heads:
            update(j, cur, h)
            if nxt is not None and h + SCORE_LEAD < N_GROUPS:
                scores(qi, j + 1, nxt, h + SCORE_LEAD, next_masked)

    def start_tile():
        m_ref[...] = jnp.full(m_ref.shape, NEG_INF, F32)
        acc_ref[...] = jnp.zeros(acc_ref.shape, F32)

    def finish_tile(qi, next_qi):
        for h in heads:
            scores(next_qi, 0, buf_a, h, False)
        q0 = start_of(qi, tq)
        for pair in range(N_GROUPS // 2):
            halves = []
            for h in (2 * pair, 2 * pair + 1):
                halves.append(acc_ref[h, :HALF, :] * (1.0 / acc_ref[h, HALF:HALF + 1, :]))
            o_t = jnp.concatenate(halves, axis=0)
            o_ref[0, pl.ds(q0, tq), pair * LANES:(pair + 1) * LANES] = o_t.T.astype(BF16)

    start_tile()
    for h in heads:
        scores(0, 0, buf_a, h, True)
    tile_step(0, 0, buf_a)
    finish_tile(0, min(1, n_q_tiles - 1))

    def q_tile(qi, _):
        start_tile()

        def two_tiles(i, _):
            tile_step(qi, 2 * i, buf_a, buf_b)
            tile_step(qi, 2 * i + 1, buf_b, buf_a)
            return 0

        n_pairs = (qi - 1) // 2
        lax.fori_loop(0, n_pairs, two_tiles, 0)
        j = 2 * n_pairs

        @pl.when(qi % 2 == 1)
        def _():
            tile_step(qi, j, buf_a, buf_b, next_masked=True)
            tile_step(qi, j + 1, buf_b)

        @pl.when(qi % 2 == 0)
        def _():
            tile_step(qi, j, buf_a, buf_b)
            tile_step(qi, j + 1, buf_b, buf_a, next_masked=True)
            tile_step(qi, j + 2, buf_a)

        finish_tile(qi, jnp.minimum(qi + 1, n_q_tiles - 1))
        return 0

    lax.fori_loop(1, n_q_tiles, q_tile, 0)


def _causal_attention(q_t, k_aug, v_t, batch, seq, granule):
    tile = ATTN_TILE
    k3 = k_aug.reshape(batch, seq, WIDE)
    out = pl.pallas_call(
        functools.partial(_attn_kernel, granule_shift=granule.bit_length() - 1),
        grid=(batch,),
        in_specs=[pl.BlockSpec((1, WIDE, seq), lambda b: (b, 0, 0)),
                  pl.BlockSpec((1, seq, WIDE), lambda b: (b, 0, 0)),
                  pl.BlockSpec((1, BRANCH_WIDTH, seq), lambda b: (b, 0, 0))],
        out_specs=pl.BlockSpec((1, seq, BRANCH_WIDTH), lambda b: (b, 0, 0)),
        out_shape=jax.ShapeDtypeStruct((batch, seq, BRANCH_WIDTH), BF16),
        scratch_shapes=[pltpu.VMEM((N_GROUPS, tile, tile), F32),
                        pltpu.VMEM((N_GROUPS, tile, tile), F32),
                        pltpu.VMEM((N_GROUPS, 1, tile), F32),
                        pltpu.VMEM((N_GROUPS, 1, tile), F32),
                        pltpu.VMEM((N_GROUPS, 1, tile), F32),
                        pltpu.VMEM((N_GROUPS, HALF + SUM_ROWS, tile), F32)],
        compiler_params=_params(1),
        name="causal_attention_g%d" % granule,
    )(q_t, k3, v_t)
    return out.reshape(batch * seq, BRANCH_WIDTH)


FF_CHUNK = 1024


def _merge_ffn_kernel(oa_ref, ob_ref, oc_ref, gates_ref, h_ref, wbr_ref, wout_ref, g1_ref, b1_ref,
                      w1_ref, w2_ref, g2_ref, b2_ref, o_ref, *, alpha):
    merged = None
    for n, br_ref in enumerate((oa_ref, ob_ref, oc_ref)):
        gate = gates_ref[:, n * D_MODEL:(n + 1) * D_MODEL].astype(F32)
        term = gate * _dot(br_ref[...], wbr_ref[n])
        merged = term if merged is None else merged + term
    y = _dot(merged.astype(BF16), wout_ref[...])
    h1 = _layer_norm(alpha * h_ref[...] + y, g1_ref[...], b1_ref[...])
    xb = h1.astype(BF16)
    acc = alpha * h1
    for c in range(0, D_FF, FF_CHUNK):
        u = jnp.maximum(_dot(xb, w1_ref[:, c:c + FF_CHUNK]), 0.0)
        acc = acc + _dot((u * u).astype(BF16), w2_ref[c:c + FF_CHUNK, :])
    o_ref[...] = _layer_norm(acc, g2_ref[...], b2_ref[...])


def _merge_ffn(oa, ob, oc, gates, h, w_br, w_out, g1, b1, w1, w2, g2, b2, alpha):
    t = h.shape[0]
    br = _row_spec(ROW_TILE, BRANCH_WIDTH)
    vec = _full_spec((1, D_MODEL))
    return pl.pallas_call(
        functools.partial(_merge_ffn_kernel, alpha=alpha),
        grid=(t // ROW_TILE,),
        in_specs=[br, br, br, _row_spec(ROW_TILE, GATE_W), _row_spec(ROW_TILE, D_MODEL),
                  _full_spec((N_BRANCHES, BRANCH_WIDTH, D_MODEL)), _full_spec((D_MODEL, D_MODEL)),
                  vec, vec, _full_spec((D_MODEL, D_FF)), _full_spec((D_FF, D_MODEL)), vec, vec],
        out_specs=_row_spec(ROW_TILE, D_MODEL),
        out_shape=jax.ShapeDtypeStruct((t, D_MODEL), F32),
        compiler_params=_params(1),
        name="merge_ffn",
    )(oa, ob, oc, gates, h, w_br, w_out, g1.reshape(1, -1), b1.reshape(1, -1), w1, w2,
      g2.reshape(1, -1), b2.reshape(1, -1))


def _layer_weights(w_in, b_forget, w_uq, w_ukv):
    depth = w_in.shape[0]
    fox_w = 3 * BRANCH_WIDTH
    tail = LANES - FORGET_LANE - N_SPLIT * FOX_HEADS
    w_in_b = w_in.astype(BF16)
    f0 = fox_w
    f1 = f0 + FOX_HEADS
    r0 = f1 + MLA_Q_RANK + MLA_KV_RANK
    r1 = r0 + MLA_ROPE_DIM
    f_log, k_rope = w_in_b[..., f0:f1], w_in_b[..., r0:r1]
    w_main = jnp.concatenate([w_in_b[..., :f0], w_in_b[..., f1:r0], w_in_b[..., r1:]], axis=-1)
    w_small = jnp.pad(jnp.concatenate([k_rope] + [f_log] * N_SPLIT, axis=-1),
                      ((0, 0), (0, 0), (MLA_NOPE_DIM, tail)))
    b3 = jnp.pad(jnp.concatenate([b_forget] * N_SPLIT, axis=-1),
                 ((0, 0), (FORGET_LANE, tail))).reshape(depth, 1, LANES)

    dq = MLA_NOPE_DIM + MLA_ROPE_DIM
    wq = w_uq.astype(BF16).reshape(depth, MLA_Q_RANK, MLA_HEADS, dq)
    wq = jnp.pad(wq, ((0, 0), (0, 0), (0, 0), (0, LANES - dq))).reshape(depth, MLA_Q_RANK, -1)
    wkv = w_ukv.astype(BF16).reshape(depth, MLA_KV_RANK, MLA_HEADS, MLA_NOPE_DIM + MLA_V_DIM)
    wk = jnp.pad(wkv[..., :MLA_NOPE_DIM], ((0, 0), (0, 0), (0, 0), (0, LANES - MLA_NOPE_DIM)))
    wk = wk.reshape(depth, MLA_KV_RANK, -1)
    wv_t = jnp.swapaxes(wkv[..., MLA_NOPE_DIM:].reshape(depth, MLA_KV_RANK, -1), 1, 2)
    return w_main, w_small, b3, wq, wk, wv_t


def kernel(x, mem, positions, ln_in_g, ln_in_b, w_in, b_forget, w_uq, g_cq, w_ukv, g_ckv,
           w_mem_kv, w_br, w_out, ln1_g, ln1_b, w_ff1, w_ff2, ln2_g, ln2_b):
    batch, seq, _ = x.shape
    depth = w_in.shape[0]
    assert seq % ATTN_TILE == 0 and seq % ROW_TILE == 0
    alpha = float((2 * depth) ** 0.25)

    w_main, w_small, b3, wq, wk, wv_t = _layer_weights(w_in, b_forget, w_uq, w_ukv)
    w_br_b = w_br.astype(BF16)
    w_out_b = w_out.astype(BF16)
    w_ff1_b = w_ff1.astype(BF16)
    w_ff2_b = w_ff2.astype(BF16)

    rope_tabs = _rope_tables(positions)
    mk, mv = _mem_kv(mem, w_mem_kv.astype(BF16))
    h = x.reshape(batch * seq, D_MODEL)
    for l in range(depth):
        outs = _mixer_inputs(h, (ln_in_g, ln_in_b) if l == 0 else None, w_main[l], w_small[l], b3[l],
                             rope_tabs, g_cq[l], g_ckv[l], wq[l], wk[l], wv_t[l], mk, mv, l,
                             batch, seq)
        gates, o_c, fq_t, fk_aug, fv_t, mq_t, mk_aug, mv_t = outs[:8]
        if l == 0:
            h = outs[8]
        o_a = _causal_attention(fq_t, fk_aug, fv_t, batch, seq, 1)
        o_b = _causal_attention(mq_t, mk_aug, mv_t, batch, seq, CHUNK)
        h = _merge_ffn(o_a, o_b, o_c, gates, h, w_br_b[l], w_out_b[l], ln1_g[l], ln1_b[l],
                       w_ff1_b[l], w_ff2_b[l], ln2_g[l], ln2_b[l], alpha)
    return h.reshape(batch, seq, D_MODEL)
```

```python
import functools
import math

import numpy as np
import jax
import jax.numpy as jnp
from jax import lax
from jax.experimental import pallas as pl
from jax.experimental.pallas import tpu as pltpu

D_MODEL = 1024
CHUNK = 64
FOX_HEADS = 8
FOX_HEAD_DIM = 64
MLA_HEADS = 8
MLA_NOPE_DIM = 64
MLA_ROPE_DIM = 32
MLA_V_DIM = 64
MLA_Q_RANK = 384
MLA_KV_RANK = 256
ROPE_BASE = 10000.0
MEM_HEADS = 4
MEM_HEAD_DIM = 128
N_BRANCHES = 3
BRANCH_WIDTH = 512
D_FF = 4 * D_MODEL
LN_EPS = 1e-5
RMS_EPS = 1e-6
NEG_INF = -1e30
LOG2E = math.log2(math.e)

LANES = 128
HALF = 64
N_GROUPS = 8
WIDE = N_GROUPS * LANES
MXU_WIDTH = 256
VMEM_LIMIT = 56 * 1024 * 1024

ROW_TILE = 512
ATTN_TILE = 256
SCORE_LEAD = 2
SUM_ROWS = 16

F32 = jnp.float32
BF16 = jnp.bfloat16


def _params(n_axes):
    return pltpu.CompilerParams(
        dimension_semantics=("arbitrary",) * n_axes, vmem_limit_bytes=VMEM_LIMIT)


def _dot(a, b):
    return jnp.dot(a, b, preferred_element_type=F32)


def _dot_nt(a, b):
    return lax.dot_general(a, b, (((1,), (1,)), ((), ())), preferred_element_type=F32)


def _layer_norm(z, g, b):
    mu = jnp.mean(z, axis=-1, keepdims=True)
    d = z - mu
    var = jnp.mean(d * d, axis=-1, keepdims=True)
    return d * lax.rsqrt(var + LN_EPS) * g + b


def _rms_norm(z, g):
    return z * lax.rsqrt(jnp.mean(z * z, axis=-1, keepdims=True) + RMS_EPS) * g


def _row_spec(tile, width, col_block=0):
    return pl.BlockSpec((tile, width), lambda i: (i, col_block))


def _full_spec(shape):
    return pl.BlockSpec(shape, lambda *_: (0,) * len(shape), pipeline_mode=pl.Buffered(1))


ROPE_HALF = MLA_ROPE_DIM // 2


def _rope_table_kernel(pos_ref, invf_ref, c_ref, slo_ref, shi_ref):
    ang = pos_ref[...].astype(F32) * invf_ref[...]
    lane = lax.broadcasted_iota(jnp.int32, ang.shape, 1)
    lo = (lane >= MLA_NOPE_DIM) & (lane < MLA_NOPE_DIM + ROPE_HALF)
    hi = (lane >= MLA_NOPE_DIM + ROPE_HALF) & (lane < MLA_NOPE_DIM + MLA_ROPE_DIM)
    sin = jnp.sin(ang)
    c_ref[...] = jnp.where(lane < MLA_NOPE_DIM, 1.0, jnp.where(lo | hi, jnp.cos(ang), 0.0))
    slo_ref[...] = jnp.where(lo, -sin, 0.0)
    shi_ref[...] = jnp.where(hi, sin, 0.0)


def _rotary(x, ct, s_lo, s_hi):
    width = x.shape[1]
    return (x * ct + pltpu.roll(x, width - ROPE_HALF, 1) * s_lo
            + pltpu.roll(x, ROPE_HALF, 1) * s_hi)


def _rope_tables(positions):
    t = positions.size
    inv_freq = ROPE_BASE ** (-jnp.arange(0, MLA_ROPE_DIM, 2, dtype=F32) / MLA_ROPE_DIM)
    invf = jnp.zeros((1, LANES), F32).at[0, MLA_NOPE_DIM:MLA_NOPE_DIM + MLA_ROPE_DIM].set(
        jnp.concatenate([inv_freq, inv_freq]))
    return pl.pallas_call(
        _rope_table_kernel,
        grid=(t // ROW_TILE,),
        in_specs=[_row_spec(ROW_TILE, 1), _full_spec((1, LANES))],
        out_specs=[_row_spec(ROW_TILE, LANES)] * 3,
        out_shape=[jax.ShapeDtypeStruct((t, LANES), F32)] * 3,
        compiler_params=_params(1),
        name="rope_tables",
    )(positions.reshape(t, 1), invf)


MEM_BATCH_GROUP = 4


def _mem_kv_kernel(mem_ref, w_ref, k_ref, v_ref):
    group, n_mem, _ = mem_ref.shape
    rows = mem_ref[...].reshape(group * n_mem, D_MODEL).astype(BF16)
    kv = _dot(rows, w_ref[0].astype(BF16))
    k = (kv[:, :BRANCH_WIDTH] * (MEM_HEAD_DIM ** -0.5 * LOG2E)).astype(BF16)
    k_ref[0] = k.reshape(group, n_mem, BRANCH_WIDTH)
    v_ref[0] = kv[:, BRANCH_WIDTH:].astype(BF16).reshape(group, n_mem, BRANCH_WIDTH)


def _mem_kv(mem, w_mem_kv):
    depth = w_mem_kv.shape[0]
    batch, n_mem, _ = mem.shape
    group = math.gcd(batch, MEM_BATCH_GROUP)
    return pl.pallas_call(
        _mem_kv_kernel,
        grid=(batch // group, depth),
        in_specs=[pl.BlockSpec((group, n_mem, D_MODEL), lambda b, l: (b, 0, 0)),
                  pl.BlockSpec((1, D_MODEL, 2 * BRANCH_WIDTH), lambda b, l: (l, 0, 0))],
        out_specs=[pl.BlockSpec((1, group, n_mem, BRANCH_WIDTH), lambda b, l: (l, b, 0, 0))] * 2,
        out_shape=[jax.ShapeDtypeStruct((depth, batch, n_mem, BRANCH_WIDTH), BF16)] * 2,
        compiler_params=_params(2),
        name="mem_kv",
    )(mem, w_mem_kv)


GATE_W = N_BRANCHES * D_MODEL
COL_FOX_Q = 0
COL_FOX_K = COL_FOX_Q + BRANCH_WIDTH
COL_FOX_V = COL_FOX_K + BRANCH_WIDTH
COL_CQ = COL_FOX_V + BRANCH_WIDTH
COL_CKV = COL_CQ + MLA_Q_RANK
COL_QMEM = COL_CKV + MLA_KV_RANK
COL_GATE = COL_QMEM + BRANCH_WIDTH
MAIN_W = COL_GATE + GATE_W
GATE_CHUNK = 512
N_SPLIT = 3
FORGET_LANE = MLA_NOPE_DIM + MLA_ROPE_DIM


def _split3(x):
    hi = x.astype(BF16)
    r = x - hi.astype(F32)
    mid = r.astype(BF16)
    lo = (r - mid.astype(F32)).astype(BF16)
    return hi, mid, lo


def _mixer_inputs_kernel(*refs, entry_norm, tiles_per_seq):
    if entry_norm:
        x_ref, lng_ref, lnb_ref, *refs = refs
    else:
        x_ref, *refs = refs
    (wfox_ref, wmla_ref, wtail_ref, ws_ref, bf_ref, pq_ref, pk_ref, oq_ref, ok_ref,
     ct_ref, slo_ref, shi_ref, gq_ref, gkv_ref, wq_ref, wk_ref, wvt_ref, mk_ref, mv_ref,
     gates_ref, oc_ref, fqt_ref, fka_ref, fvt_ref, mqt_ref, mka_ref, mvt_ref, *rest) = refs
    carry_ref = rest[-1]
    tm = x_ref.shape[0]

    @pl.when(pl.program_id(0) % tiles_per_seq == 0)
    def _():
        carry_ref[...] = jnp.zeros_like(carry_ref)

    h = x_ref[...]
    if entry_norm:
        h = _layer_norm(h, lng_ref[...], lnb_ref[...])
        rest[0][...] = h
    xb = h.astype(BF16)

    def mm(c0, n):
        for w_ref, first in ((wtail_ref, COL_QMEM), (wmla_ref, COL_CQ), (wfox_ref, COL_FOX_Q)):
            if c0 >= first:
                return _dot(xb, w_ref[:, c0 - first:c0 - first + n])

    gate_cols = iter(range(0, GATE_W, GATE_CHUNK))

    def emit_gates(n_chunks=1):
        for _ in range(n_chunks):
            c = next(gate_cols, None)
            if c is not None:
                g = mm(COL_GATE + c, GATE_CHUNK)
                gates_ref[:, c:c + GATE_CHUNK] = (1.0 / (1.0 + jnp.exp(-g))).astype(BF16)

    small = _dot(xb, ws_ref[...])
    xq = _rms_norm(mm(COL_CQ, MLA_Q_RANK), gq_ref[...]).astype(BF16)
    xkv = _rms_norm(mm(COL_CKV, MLA_KV_RANK), gkv_ref[...]).astype(BF16)

    x = small + bf_ref[...]
    logf = jnp.minimum(x, 0.0) - jnp.log1p(jnp.exp(-jnp.abs(x)))
    row = lax.broadcasted_iota(jnp.int32, (tm, tm), 0)
    col = lax.broadcasted_iota(jnp.int32, (tm, tm), 1)
    tri = jnp.where(col <= row, 1.0, 0.0).astype(BF16)
    hi, mid, lo = _split3(logf)
    cum = _dot(tri, hi) + _dot(tri, mid) + _dot(tri, lo) + carry_ref[...]
    carry_ref[...] = cum[tm - 1:tm, :]
    emit_gates()

    lane = lax.broadcasted_iota(jnp.int32, (tm, LANES), 1)
    span = lane - FORGET_LANE
    chi, cmid, clo = _split3(cum * LOG2E)
    pieces = jnp.where((span >= 0) & (span < FOX_HEADS), chi,
                       jnp.where((span >= FOX_HEADS) & (span < 2 * FOX_HEADS), cmid,
                                 jnp.where((span >= 2 * FOX_HEADS) & (span < 3 * FOX_HEADS), clo,
                                           jnp.zeros_like(chi))))
    left = lane < HALF
    heads_per_dot = MXU_WIDTH // HALF
    for h0 in range(0, FOX_HEADS, heads_per_dot):
        c0 = h0 * HALF
        qp2 = mm(COL_FOX_Q + c0, MXU_WIDTH) * (FOX_HEAD_DIM ** -0.5 * LOG2E)
        kp2 = mm(COL_FOX_K + c0, MXU_WIDTH)
        grp4 = slice(h0 * LANES, (h0 + heads_per_dot) * LANES)
        q_spare = _dot(pieces, pq_ref[:, grp4]) + oq_ref[:, grp4]
        k_spare = _dot(pieces, pk_ref[:, grp4]) + ok_ref[:, grp4]
        for i in range(heads_per_dot):
            head = h0 + i
            pair = slice((i // 2) * LANES, (i // 2 + 1) * LANES)
            loc = slice(i * LANES, (i + 1) * LANES)
            grp = slice(head * LANES, (head + 1) * LANES)
            own = left if head % 2 == 0 else jnp.logical_not(left)
            fqt_ref[0, grp, :] = jnp.where(own, qp2[:, pair], q_spare[:, loc]).T.astype(BF16)
            fka_ref[:, grp] = jnp.where(own, kp2[:, pair], k_spare[:, loc]).astype(BF16)
        emit_gates()
    fvt_ref[0] = mm(COL_FOX_V, BRANCH_WIDTH).T.astype(BF16)

    ct, s_lo, s_hi = ct_ref[...], slo_ref[...], shi_ref[...]
    groups_per_dot = MXU_WIDTH // LANES
    ct2, s_lo2, s_hi2 = (jnp.concatenate([tab] * groups_per_dot, axis=1) for tab in (ct, s_lo, s_hi))
    scale = (MLA_NOPE_DIM + MLA_ROPE_DIM) ** -0.5 * LOG2E
    kpe = _rotary(small, ct, s_lo, s_hi)
    kpe2 = jnp.concatenate([kpe] * groups_per_dot, axis=1)
    for n, c in enumerate(range(0, WIDE, MXU_WIDTH)):
        q = _rotary(_dot(xq, wq_ref[:, c:c + MXU_WIDTH]), ct2, s_lo2, s_hi2) * scale
        mqt_ref[0, c:c + MXU_WIDTH, :] = q.T.astype(BF16)
        mka_ref[:, c:c + MXU_WIDTH] = (_dot(xkv, wk_ref[:, c:c + MXU_WIDTH]) + kpe2).astype(BF16)
        if n % 2 == 1:
            emit_gates()
    mvt_ref[0] = _dot_nt(wvt_ref[...], xkv).astype(BF16)

    qm = mm(COL_QMEM, BRANCH_WIDTH).astype(BF16)
    for hm in range(MEM_HEADS):
        grp = slice(hm * MEM_HEAD_DIM, (hm + 1) * MEM_HEAD_DIM)
        s = _dot_nt(qm[:, grp], mk_ref[0, 0, :, grp])
        e = jnp.exp2(s - jnp.max(s, axis=1, keepdims=True))
        o = _dot(e.astype(BF16), mv_ref[0, 0, :, grp])
        oc_ref[:, grp] = (o * (1.0 / jnp.sum(e, axis=1, keepdims=True))).astype(BF16)
    emit_gates(GATE_W // GATE_CHUNK)


def _fox_placement():
    pq = np.zeros((LANES, WIDE), np.float32)
    pk = np.zeros((LANES, WIDE), np.float32)
    oq = np.zeros((1, WIDE), np.float32)
    ok = np.zeros((1, WIDE), np.float32)
    for h in range(FOX_HEADS):
        base = h * LANES + (HALF if h % 2 == 0 else 0)
        for j in range(N_SPLIT):
            src = FORGET_LANE + j * FOX_HEADS + h
            pq[src, base + j] = 1.0
            ok[0, base + j] = 1.0
            oq[0, base + N_SPLIT + j] = 1.0
            pk[src, base + N_SPLIT + j] = -1.0
    return (jnp.asarray(pq, BF16), jnp.asarray(pk, BF16), jnp.asarray(oq), jnp.asarray(ok))


def _mixer_inputs(h, ln_in, w_main, w_small, b_forget3, rope_tabs, g_cq, g_ckv, wq, wk, wvt,
                  mk, mv, layer, batch, seq):
    t = h.shape[0]
    nt = seq // ROW_TILE
    n_mem = mk.shape[2]
    entry_norm = ln_in is not None
    pq, pk, oq, ok = _fox_placement()
    mem_spec = pl.BlockSpec((1, 1, n_mem, BRANCH_WIDTH), lambda i: (layer, i // nt, 0, 0))

    def transposed(rows):
        return (pl.BlockSpec((1, rows, ROW_TILE), lambda i: (i // nt, 0, i % nt)),
                jax.ShapeDtypeStruct((batch, rows, seq), BF16))

    def row_major(width):
        return _row_spec(ROW_TILE, width), jax.ShapeDtypeStruct((t, width), BF16)

    operands = [h]
    in_specs = [_row_spec(ROW_TILE, D_MODEL)]
    if entry_norm:
        operands += [ln_in[0].reshape(1, -1), ln_in[1].reshape(1, -1)]
        in_specs += [_full_spec((1, D_MODEL))] * 2
    operands += [*w_main, w_small, b_forget3, pq, pk, oq, ok, *rope_tabs,
                 g_cq.reshape(1, -1), g_ckv.reshape(1, -1), wq, wk, wvt, mk, mv]
    in_specs += [_full_spec((D_MODEL, COL_CQ - COL_FOX_Q)), _full_spec((D_MODEL, COL_QMEM - COL_CQ)),
                 _full_spec((D_MODEL, MAIN_W - COL_QMEM)), _full_spec((D_MODEL, LANES)),
                 _full_spec((1, LANES)),
                 _full_spec((LANES, WIDE)), _full_spec((LANES, WIDE)), _full_spec((1, WIDE)),
                 _full_spec((1, WIDE))] + [_row_spec(ROW_TILE, LANES)] * 3 + [
                 _full_spec((1, MLA_Q_RANK)), _full_spec((1, MLA_KV_RANK)),
                 _full_spec((MLA_Q_RANK, WIDE)), _full_spec((MLA_KV_RANK, WIDE)),
                 _full_spec((BRANCH_WIDTH, MLA_KV_RANK)), mem_spec, mem_spec]

    outs = [row_major(GATE_W), row_major(BRANCH_WIDTH),
            transposed(WIDE), row_major(WIDE), transposed(BRANCH_WIDTH),
            transposed(WIDE), row_major(WIDE), transposed(BRANCH_WIDTH)]
    out_specs = [spec for spec, _ in outs]
    out_shape = [shape for _, shape in outs]
    if entry_norm:
        out_specs.append(_row_spec(ROW_TILE, D_MODEL))
        out_shape.append(jax.ShapeDtypeStruct((t, D_MODEL), F32))
    return pl.pallas_call(
        functools.partial(_mixer_inputs_kernel, entry_norm=entry_norm, tiles_per_seq=nt),
        grid=(t // ROW_TILE,),
        in_specs=in_specs,
        out_specs=out_specs,
        out_shape=out_shape,
        scratch_shapes=[pltpu.VMEM((1, LANES), F32)],
        compiler_params=_params(1),
        name="mixer_inputs",
    )(*operands)


def _attn_kernel(qt_ref, k_ref, vt_ref, o_ref, sa_ref, sb_ref, ta_ref, tb_ref, m_ref, acc_ref,
                 *, granule_shift):
    tq = tk = sa_ref.shape[1]
    n_q_tiles = qt_ref.shape[2] // tq
    heads = range(N_GROUPS)
    buf_a = (sa_ref, ta_ref)
    buf_b = (sb_ref, tb_ref)
    ones_rows = jnp.ones((SUM_ROWS, tk), BF16)

    def start_of(tile_index, size):
        start = tile_index * size
        return start if isinstance(start, int) else pl.multiple_of(start, size)

    def scores(qi, j, buf, h, masked):
        q0 = start_of(qi, tq)
        ks = start_of(j, tk)
        grp = slice(h * LANES, (h + 1) * LANES)
        s = _dot(k_ref[0, pl.ds(ks, tk), grp], qt_ref[0, grp, pl.ds(q0, tq)])
        if masked:
            key = lax.broadcasted_iota(jnp.int32, (tk, tq), 0)
            qry = lax.broadcasted_iota(jnp.int32, (tk, tq), 1)
            s = jnp.where((key >> granule_shift) <= (qry >> granule_shift), s, NEG_INF)
        buf[0][h] = s
        buf[1][h] = jnp.max(s, axis=0, keepdims=True)

    def update(j, buf, h):
        ks = start_of(j, tk)
        m = m_ref[h]
        m_new = jnp.maximum(m, buf[1][h])
        alpha = jnp.exp2(m - m_new)
        e = jnp.exp2(buf[0][h] - m_new)
        m_ref[h] = m_new
        v_t = jnp.concatenate([vt_ref[0, h * HALF:(h + 1) * HALF, pl.ds(ks, tk)], ones_rows],
                              axis=0)
        acc_ref[h] = alpha * acc_ref[h] + _dot(v_t, e.astype(BF16))

    def tile_step(qi, j, cur, nxt=None, next_masked=False):
        if nxt is not None:
            for h in range(SCORE_LEAD):
                scores(qi, j + 1, nxt, h, next_masked)
        for h in heads:
            update(j, cur, h)
            if nxt is not None and h + SCORE_LEAD < N_GROUPS:
                scores(qi, j + 1, nxt, h + SCORE_LEAD, next_masked)

    def start_tile():
        m_ref[...] = jnp.full(m_ref.shape, NEG_INF, F32)
        acc_ref[...] = jnp.zeros(acc_ref.shape, F32)

    def finish_tile(qi, next_qi):
        for h in heads:
            scores(next_qi, 0, buf_a, h, False)
        q0 = start_of(qi, tq)
        for pair in range(N_GROUPS // 2):
            halves = []
            for h in (2 * pair, 2 * pair + 1):
                halves.append(acc_ref[h, :HALF, :] * (1.0 / acc_ref[h, HALF:HALF + 1, :]))
            o_t = jnp.concatenate(halves, axis=0)
            o_ref[0, pl.ds(q0, tq), pair * LANES:(pair + 1) * LANES] = o_t.T.astype(BF16)

    start_tile()
    for h in heads:
        scores(0, 0, buf_a, h, True)
    tile_step(0, 0, buf_a)
    finish_tile(0, min(1, n_q_tiles - 1))

    def q_tile(qi, _):
        start_tile()

        def two_tiles(i, _):
            tile_step(qi, 2 * i, buf_a, buf_b)
            tile_step(qi, 2 * i + 1, buf_b, buf_a)
            return 0

        n_pairs = (qi - 1) // 2
        lax.fori_loop(0, n_pairs, two_tiles, 0)
        j = 2 * n_pairs

        @pl.when(qi % 2 == 1)
        def _():
            tile_step(qi, j, buf_a, buf_b, next_masked=True)
            tile_step(qi, j + 1, buf_b)

        @pl.when(qi % 2 == 0)
        def _():
            tile_step(qi, j, buf_a, buf_b)
            tile_step(qi, j + 1, buf_b, buf_a, next_masked=True)
            tile_step(qi, j + 2, buf_a)

        finish_tile(qi, jnp.minimum(qi + 1, n_q_tiles - 1))
        return 0

    lax.fori_loop(1, n_q_tiles, q_tile, 0)


def _causal_attention(q_t, k_aug, v_t, batch, seq, granule):
    tile = ATTN_TILE
    k3 = k_aug.reshape(batch, seq, WIDE)
    out = pl.pallas_call(
        functools.partial(_attn_kernel, granule_shift=granule.bit_length() - 1),
        grid=(batch,),
        in_specs=[pl.BlockSpec((1, WIDE, seq), lambda b: (b, 0, 0)),
                  pl.BlockSpec((1, seq, WIDE), lambda b: (b, 0, 0)),
                  pl.BlockSpec((1, BRANCH_WIDTH, seq), lambda b: (b, 0, 0))],
        out_specs=pl.BlockSpec((1, seq, BRANCH_WIDTH), lambda b: (b, 0, 0)),
        out_shape=jax.ShapeDtypeStruct((batch, seq, BRANCH_WIDTH), BF16),
        scratch_shapes=[pltpu.VMEM((N_GROUPS, tile, tile), F32),
                        pltpu.VMEM((N_GROUPS, tile, tile), F32),
                        pltpu.VMEM((N_GROUPS, 1, tile), F32),
                        pltpu.VMEM((N_GROUPS, 1, tile), F32),
                        pltpu.VMEM((N_GROUPS, 1, tile), F32),
                        pltpu.VMEM((N_GROUPS, HALF + SUM_ROWS, tile), F32)],
        compiler_params=_params(1),
        name="causal_attention_g%d" % granule,
    )(q_t, k3, v_t)
    return out.reshape(batch * seq, BRANCH_WIDTH)


FF_CHUNK = 1024


def _merge_ffn_kernel(oa_ref, ob_ref, oc_ref, gates_ref, h_ref, wbr_ref, wout_ref, g1_ref, b1_ref,
                      w1_ref, w2_ref, g2_ref, b2_ref, o_ref, *, alpha):
    merged = None
    for n, br_ref in enumerate((oa_ref, ob_ref, oc_ref)):
        gate = gates_ref[:, n * D_MODEL:(n + 1) * D_MODEL].astype(F32)
        term = gate * _dot(br_ref[...], wbr_ref[n])
        merged = term if merged is None else merged + term
    y = _dot(merged.astype(BF16), wout_ref[...])
    h1 = _layer_norm(alpha * h_ref[...] + y, g1_ref[...], b1_ref[...])
    xb = h1.astype(BF16)
    acc = alpha * h1
    for c in range(0, D_FF, FF_CHUNK):
        u = jnp.maximum(_dot(xb, w1_ref[:, c:c + FF_CHUNK]), 0.0)
        acc = acc + _dot((u * u).astype(BF16), w2_ref[c:c + FF_CHUNK, :])
    o_ref[...] = _layer_norm(acc, g2_ref[...], b2_ref[...])


def _merge_ffn(oa, ob, oc, gates, h, w_br, w_out, g1, b1, w1, w2, g2, b2, alpha):
    t = h.shape[0]
    br = _row_spec(ROW_TILE, BRANCH_WIDTH)
    vec = _full_spec((1, D_MODEL))
    return pl.pallas_call(
        functools.partial(_merge_ffn_kernel, alpha=alpha),
        grid=(t // ROW_TILE,),
        in_specs=[br, br, br, _row_spec(ROW_TILE, GATE_W), _row_spec(ROW_TILE, D_MODEL),
                  _full_spec((N_BRANCHES, BRANCH_WIDTH, D_MODEL)), _full_spec((D_MODEL, D_MODEL)),
                  vec, vec, _full_spec((D_MODEL, D_FF)), _full_spec((D_FF, D_MODEL)), vec, vec],
        out_specs=_row_spec(ROW_TILE, D_MODEL),
        out_shape=jax.ShapeDtypeStruct((t, D_MODEL), F32),
        compiler_params=_params(1),
        name="merge_ffn",
    )(oa, ob, oc, gates, h, w_br, w_out, g1.reshape(1, -1), b1.reshape(1, -1), w1, w2,
      g2.reshape(1, -1), b2.reshape(1, -1))


def _layer_weights(w_in, b_forget, w_uq, w_ukv):
    depth = w_in.shape[0]
    fox_w = 3 * BRANCH_WIDTH
    tail = LANES - FORGET_LANE - N_SPLIT * FOX_HEADS
    f0 = fox_w
    f1 = f0 + FOX_HEADS
    r0 = f1 + MLA_Q_RANK + MLA_KV_RANK
    r1 = r0 + MLA_ROPE_DIM
    f_log, k_rope = w_in[..., f0:f1], w_in[..., r0:r1]
    w_main = (w_in[..., :f0].astype(BF16), w_in[..., f1:r0].astype(BF16), w_in[..., r1:].astype(BF16))
    w_small = jnp.pad(jnp.concatenate([k_rope] + [f_log] * N_SPLIT, axis=-1).astype(BF16),
                      ((0, 0), (0, 0), (MLA_NOPE_DIM, tail)))
    b3 = jnp.pad(jnp.concatenate([b_forget] * N_SPLIT, axis=-1),
                 ((0, 0), (FORGET_LANE, tail))).reshape(depth, 1, LANES)

    dq = MLA_NOPE_DIM + MLA_ROPE_DIM
    wq = w_uq.astype(BF16).reshape(depth, MLA_Q_RANK, MLA_HEADS, dq)
    wq = jnp.pad(wq, ((0, 0), (0, 0), (0, 0), (0, LANES - dq))).reshape(depth, MLA_Q_RANK, -1)
    wkv = w_ukv.astype(BF16).reshape(depth, MLA_KV_RANK, MLA_HEADS, MLA_NOPE_DIM + MLA_V_DIM)
    wk = jnp.pad(wkv[..., :MLA_NOPE_DIM], ((0, 0), (0, 0), (0, 0), (0, LANES - MLA_NOPE_DIM)))
    wk = wk.reshape(depth, MLA_KV_RANK, -1)
    wv_t = jnp.swapaxes(wkv[..., MLA_NOPE_DIM:].reshape(depth, MLA_KV_RANK, -1), 1, 2)
    return w_main, w_small, b3, wq, wk, wv_t


def kernel(x, mem, positions, ln_in_g, ln_in_b, w_in, b_forget, w_uq, g_cq, w_ukv, g_ckv,
           w_mem_kv, w_br, w_out, ln1_g, ln1_b, w_ff1, w_ff2, ln2_g, ln2_b):
    batch, seq, _ = x.shape
    depth = w_in.shape[0]
    assert seq % ATTN_TILE == 0 and seq % ROW_TILE == 0
    alpha = float((2 * depth) ** 0.25)

    w_main, w_small, b3, wq, wk, wv_t = _layer_weights(w_in, b_forget, w_uq, w_ukv)
    w_br_b = w_br.astype(BF16)
    w_out_b = w_out.astype(BF16)
    w_ff1_b = w_ff1.astype(BF16)
    w_ff2_b = w_ff2.astype(BF16)

    rope_tabs = _rope_tables(positions)
    mk, mv = _mem_kv(mem, w_mem_kv)
    h = x.reshape(batch * seq, D_MODEL)
    for l in range(depth):
        outs = _mixer_inputs(h, (ln_in_g, ln_in_b) if l == 0 else None,
                             tuple(w[l] for w in w_main), w_small[l], b3[l],
                             rope_tabs, g_cq[l], g_ckv[l], wq[l], wk[l], wv_t[l], mk, mv, l,
                             batch, seq)
        gates, o_c, fq_t, fk_aug, fv_t, mq_t, mk_aug, mv_t = outs[:8]
        if l == 0:
            h = outs[8]
        o_a = _causal_attention(fq_t, fk_aug, fv_t, batch, seq, 1)
        o_b = _causal_attention(mq_t, mk_aug, mv_t, batch, seq, CHUNK)
        h = _merge_ffn(o_a, o_b, o_c, gates, h, w_br_b[l], w_out_b[l], ln1_g[l], ln1_b[l],
                       w_ff1_b[l], w_ff2_b[l], ln2_g[l], ln2_b[l], alpha)
    return h.reshape(batch, seq, D_MODEL)
```

```python
import functools
import math

import numpy as np
import jax
import jax.numpy as jnp
from jax import lax
from jax.experimental import pallas as pl
from jax.experimental.pallas import tpu as pltpu

D_MODEL = 1024
CHUNK = 64
FOX_HEADS = 8
FOX_HEAD_DIM = 64
MLA_HEADS = 8
MLA_NOPE_DIM = 64
MLA_ROPE_DIM = 32
MLA_V_DIM = 64
MLA_Q_RANK = 384
MLA_KV_RANK = 256
ROPE_BASE = 10000.0
MEM_HEADS = 4
MEM_HEAD_DIM = 128
N_BRANCHES = 3
BRANCH_WIDTH = 512
D_FF = 4 * D_MODEL
LN_EPS = 1e-5
RMS_EPS = 1e-6
NEG_INF = -1e30
LOG2E = math.log2(math.e)

LANES = 128
HALF = 64
N_GROUPS = 8
WIDE = N_GROUPS * LANES
MXU_WIDTH = 256
VMEM_LIMIT = 56 * 1024 * 1024

ROW_TILE = 512
ATTN_TILE = 256
SCORE_LEAD = 2
SUM_ROWS = 16

F32 = jnp.float32
BF16 = jnp.bfloat16


def _params(n_axes):
    return pltpu.CompilerParams(
        dimension_semantics=("arbitrary",) * n_axes, vmem_limit_bytes=VMEM_LIMIT)


def _dot(a, b):
    return jnp.dot(a, b, preferred_element_type=F32)


def _dot_nt(a, b):
    return lax.dot_general(a, b, (((1,), (1,)), ((), ())), preferred_element_type=F32)


def _layer_norm(z, g, b):
    mu = jnp.mean(z, axis=-1, keepdims=True)
    d = z - mu
    var = jnp.mean(d * d, axis=-1, keepdims=True)
    return d * lax.rsqrt(var + LN_EPS) * g + b


def _rms_norm(z, g):
    return z * lax.rsqrt(jnp.mean(z * z, axis=-1, keepdims=True) + RMS_EPS) * g


def _row_spec(tile, width, col_block=0):
    return pl.BlockSpec((tile, width), lambda i: (i, col_block))


def _full_spec(shape):
    return pl.BlockSpec(shape, lambda *_: (0,) * len(shape), pipeline_mode=pl.Buffered(1))


ROPE_HALF = MLA_ROPE_DIM // 2


def _rope_table_kernel(pos_ref, invf_ref, c_ref, slo_ref, shi_ref):
    ang = pos_ref[...].astype(F32) * invf_ref[...]
    lane = lax.broadcasted_iota(jnp.int32, ang.shape, 1)
    lo = (lane >= MLA_NOPE_DIM) & (lane < MLA_NOPE_DIM + ROPE_HALF)
    hi = (lane >= MLA_NOPE_DIM + ROPE_HALF) & (lane < MLA_NOPE_DIM + MLA_ROPE_DIM)
    sin = jnp.sin(ang)
    c_ref[...] = jnp.where(lane < MLA_NOPE_DIM, 1.0, jnp.where(lo | hi, jnp.cos(ang), 0.0))
    slo_ref[...] = jnp.where(lo, -sin, 0.0)
    shi_ref[...] = jnp.where(hi, sin, 0.0)


def _rotary(x, ct, s_lo, s_hi):
    width = x.shape[1]
    return (x * ct + pltpu.roll(x, width - ROPE_HALF, 1) * s_lo
            + pltpu.roll(x, ROPE_HALF, 1) * s_hi)


def _rope_tables(positions):
    t = positions.size
    inv_freq = ROPE_BASE ** (-jnp.arange(0, MLA_ROPE_DIM, 2, dtype=F32) / MLA_ROPE_DIM)
    invf = jnp.zeros((1, LANES), F32).at[0, MLA_NOPE_DIM:MLA_NOPE_DIM + MLA_ROPE_DIM].set(
        jnp.concatenate([inv_freq, inv_freq]))
    return pl.pallas_call(
        _rope_table_kernel,
        grid=(t // ROW_TILE,),
        in_specs=[_row_spec(ROW_TILE, 1), _full_spec((1, LANES))],
        out_specs=[_row_spec(ROW_TILE, LANES)] * 3,
        out_shape=[jax.ShapeDtypeStruct((t, LANES), F32)] * 3,
        compiler_params=_params(1),
        name="rope_tables",
    )(positions.reshape(t, 1), invf)


MEM_BATCH_GROUP = 4


def _mem_kv_kernel(mem_ref, w_ref, k_ref, v_ref):
    group, n_mem, _ = mem_ref.shape
    rows = mem_ref[...].reshape(group * n_mem, D_MODEL).astype(BF16)
    kv = _dot(rows, w_ref[0].astype(BF16))
    k = (kv[:, :BRANCH_WIDTH] * (MEM_HEAD_DIM ** -0.5 * LOG2E)).astype(BF16)
    k_ref[0] = k.reshape(group, n_mem, BRANCH_WIDTH)
    v_ref[0] = kv[:, BRANCH_WIDTH:].astype(BF16).reshape(group, n_mem, BRANCH_WIDTH)


def _mem_kv(mem, w_mem_kv):
    depth = w_mem_kv.shape[0]
    batch, n_mem, _ = mem.shape
    group = math.gcd(batch, MEM_BATCH_GROUP)
    return pl.pallas_call(
        _mem_kv_kernel,
        grid=(batch // group, depth),
        in_specs=[pl.BlockSpec((group, n_mem, D_MODEL), lambda b, l: (b, 0, 0)),
                  pl.BlockSpec((1, D_MODEL, 2 * BRANCH_WIDTH), lambda b, l: (l, 0, 0))],
        out_specs=[pl.BlockSpec((1, group, n_mem, BRANCH_WIDTH), lambda b, l: (l, b, 0, 0))] * 2,
        out_shape=[jax.ShapeDtypeStruct((depth, batch, n_mem, BRANCH_WIDTH), BF16)] * 2,
        compiler_params=_params(2),
        name="mem_kv",
    )(mem, w_mem_kv)


GATE_W = N_BRANCHES * D_MODEL
COL_FOX_Q = 0
COL_FOX_K = COL_FOX_Q + BRANCH_WIDTH
COL_FOX_V = COL_FOX_K + BRANCH_WIDTH
COL_CQ = COL_FOX_V + BRANCH_WIDTH
COL_CKV = COL_CQ + MLA_Q_RANK
COL_QMEM = COL_CKV + MLA_KV_RANK
COL_GATE = COL_QMEM + BRANCH_WIDTH
MAIN_W = COL_GATE + GATE_W
GATE_CHUNK = 512
N_SPLIT = 3
FORGET_LANE = MLA_NOPE_DIM + MLA_ROPE_DIM


def _split3(x):
    hi = x.astype(BF16)
    r = x - hi.astype(F32)
    mid = r.astype(BF16)
    lo = (r - mid.astype(F32)).astype(BF16)
    return hi, mid, lo


def _mixer_inputs_kernel(*refs, entry_norm, tiles_per_seq):
    if entry_norm:
        x_ref, lng_ref, lnb_ref, *refs = refs
    else:
        x_ref, *refs = refs
    (wfox_ref, wmla_ref, wtail_ref, ws_ref, bf_ref, pq_ref, pk_ref, oq_ref, ok_ref,
     ct_ref, slo_ref, shi_ref, gq_ref, gkv_ref, wq_ref, wk_ref, wvt_ref, mk_ref, mv_ref,
     gates_ref, oc_ref, fqt_ref, fka_ref, fvt_ref, mqt_ref, mka_ref, mvt_ref, *rest) = refs
    carry_ref = rest[-1]
    tm = x_ref.shape[0]

    @pl.when(pl.program_id(0) % tiles_per_seq == 0)
    def _():
        carry_ref[...] = jnp.zeros_like(carry_ref)

    h = x_ref[...]
    if entry_norm:
        h = _layer_norm(h, lng_ref[...], lnb_ref[...])
        rest[0][...] = h
    xb = h.astype(BF16)

    def mm(c0, n):
        for w_ref, first in ((wtail_ref, COL_QMEM), (wmla_ref, COL_CQ), (wfox_ref, COL_FOX_Q)):
            if c0 >= first:
                return _dot(xb, w_ref[:, c0 - first:c0 - first + n])

    gate_cols = iter(range(0, GATE_W, GATE_CHUNK))

    def emit_gates(n_chunks=1):
        for _ in range(n_chunks):
            c = next(gate_cols, None)
            if c is not None:
                g = mm(COL_GATE + c, GATE_CHUNK)
                gates_ref[:, c:c + GATE_CHUNK] = (0.5 * jnp.tanh(0.5 * g) + 0.5).astype(BF16)

    small = _dot(xb, ws_ref[...])
    xq = _rms_norm(mm(COL_CQ, MLA_Q_RANK), gq_ref[...]).astype(BF16)
    xkv = _rms_norm(mm(COL_CKV, MLA_KV_RANK), gkv_ref[...]).astype(BF16)

    x = small + bf_ref[...]
    logf = jnp.minimum(x, 0.0) - jnp.log1p(jnp.exp(-jnp.abs(x)))
    row = lax.broadcasted_iota(jnp.int32, (tm, tm), 0)
    col = lax.broadcasted_iota(jnp.int32, (tm, tm), 1)
    tri = jnp.where(col <= row, 1.0, 0.0).astype(BF16)
    hi, mid, lo = _split3(logf)
    cum = _dot(tri, hi) + _dot(tri, mid) + _dot(tri, lo) + carry_ref[...]
    carry_ref[...] = cum[tm - 1:tm, :]
    emit_gates()

    lane = lax.broadcasted_iota(jnp.int32, (tm, LANES), 1)
    span = lane - FORGET_LANE
    chi, cmid, clo = _split3(cum * LOG2E)
    pieces = jnp.where((span >= 0) & (span < FOX_HEADS), chi,
                       jnp.where((span >= FOX_HEADS) & (span < 2 * FOX_HEADS), cmid,
                                 jnp.where((span >= 2 * FOX_HEADS) & (span < 3 * FOX_HEADS), clo,
                                           jnp.zeros_like(chi))))
    left = lane < HALF
    heads_per_dot = MXU_WIDTH // HALF
    for h0 in range(0, FOX_HEADS, heads_per_dot):
        c0 = h0 * HALF
        qp2 = mm(COL_FOX_Q + c0, MXU_WIDTH) * (FOX_HEAD_DIM ** -0.5 * LOG2E)
        kp2 = mm(COL_FOX_K + c0, MXU_WIDTH)
        grp4 = slice(h0 * LANES, (h0 + heads_per_dot) * LANES)
        q_spare = _dot(pieces, pq_ref[:, grp4]) + oq_ref[:, grp4]
        k_spare = _dot(pieces, pk_ref[:, grp4]) + ok_ref[:, grp4]
        for i in range(heads_per_dot):
            head = h0 + i
            pair = slice((i // 2) * LANES, (i // 2 + 1) * LANES)
            loc = slice(i * LANES, (i + 1) * LANES)
            grp = slice(head * LANES, (head + 1) * LANES)
            own = left if head % 2 == 0 else jnp.logical_not(left)
            fqt_ref[0, grp, :] = jnp.where(own, qp2[:, pair], q_spare[:, loc]).T.astype(BF16)
            fka_ref[:, grp] = jnp.where(own, kp2[:, pair], k_spare[:, loc]).astype(BF16)
        emit_gates()
    fvt_ref[0] = mm(COL_FOX_V, BRANCH_WIDTH).T.astype(BF16)

    qm = mm(COL_QMEM, BRANCH_WIDTH).astype(BF16)
    for hm in range(MEM_HEADS):
        grp = slice(hm * MEM_HEAD_DIM, (hm + 1) * MEM_HEAD_DIM)
        s = _dot_nt(qm[:, grp], mk_ref[0, 0, :, grp])
        e = jnp.exp2(s - jnp.max(s, axis=1, keepdims=True))
        o = _dot(e.astype(BF16), mv_ref[0, 0, :, grp])
        oc_ref[:, grp] = (o * (1.0 / jnp.sum(e, axis=1, keepdims=True))).astype(BF16)
    emit_gates()

    ct, s_lo, s_hi = ct_ref[...], slo_ref[...], shi_ref[...]
    groups_per_dot = MXU_WIDTH // LANES
    ct2, s_lo2, s_hi2 = (jnp.concatenate([tab] * groups_per_dot, axis=1) for tab in (ct, s_lo, s_hi))
    scale = (MLA_NOPE_DIM + MLA_ROPE_DIM) ** -0.5 * LOG2E
    kpe = _rotary(small, ct, s_lo, s_hi)
    kpe2 = jnp.concatenate([kpe] * groups_per_dot, axis=1)
    for n, c in enumerate(range(0, WIDE, MXU_WIDTH)):
        q = _rotary(_dot(xq, wq_ref[:, c:c + MXU_WIDTH]), ct2, s_lo2, s_hi2) * scale
        mqt_ref[0, c:c + MXU_WIDTH, :] = q.T.astype(BF16)
        mka_ref[:, c:c + MXU_WIDTH] = (_dot(xkv, wk_ref[:, c:c + MXU_WIDTH]) + kpe2).astype(BF16)
        if n == 1:
            emit_gates()
    mvt_ref[0] = _dot_nt(wvt_ref[...], xkv).astype(BF16)
    emit_gates(GATE_W // GATE_CHUNK)


def _fox_placement():
    pq = np.zeros((LANES, WIDE), np.float32)
    pk = np.zeros((LANES, WIDE), np.float32)
    oq = np.zeros((1, WIDE), np.float32)
    ok = np.zeros((1, WIDE), np.float32)
    for h in range(FOX_HEADS):
        base = h * LANES + (HALF if h % 2 == 0 else 0)
        for j in range(N_SPLIT):
            src = FORGET_LANE + j * FOX_HEADS + h
            pq[src, base + j] = 1.0
            ok[0, base + j] = 1.0
            oq[0, base + N_SPLIT + j] = 1.0
            pk[src, base + N_SPLIT + j] = -1.0
    return (jnp.asarray(pq, BF16), jnp.asarray(pk, BF16), jnp.asarray(oq), jnp.asarray(ok))


def _mixer_inputs(h, ln_in, w_main, w_small, b_forget3, rope_tabs, g_cq, g_ckv, wq, wk, wvt,
                  mk, mv, layer, batch, seq):
    t = h.shape[0]
    nt = seq // ROW_TILE
    n_mem = mk.shape[2]
    entry_norm = ln_in is not None
    pq, pk, oq, ok = _fox_placement()
    mem_spec = pl.BlockSpec((1, 1, n_mem, BRANCH_WIDTH), lambda i: (layer, i // nt, 0, 0))

    def transposed(rows):
        return (pl.BlockSpec((1, rows, ROW_TILE), lambda i: (i // nt, 0, i % nt)),
                jax.ShapeDtypeStruct((batch, rows, seq), BF16))

    def row_major(width):
        return _row_spec(ROW_TILE, width), jax.ShapeDtypeStruct((t, width), BF16)

    operands = [h]
    in_specs = [_row_spec(ROW_TILE, D_MODEL)]
    if entry_norm:
        operands += [ln_in[0].reshape(1, -1), ln_in[1].reshape(1, -1)]
        in_specs += [_full_spec((1, D_MODEL))] * 2
    operands += [*w_main, w_small, b_forget3, pq, pk, oq, ok, *rope_tabs,
                 g_cq.reshape(1, -1), g_ckv.reshape(1, -1), wq, wk, wvt, mk, mv]
    in_specs += [_full_spec((D_MODEL, COL_CQ - COL_FOX_Q)), _full_spec((D_MODEL, COL_QMEM - COL_CQ)),
                 _full_spec((D_MODEL, MAIN_W - COL_QMEM)), _full_spec((D_MODEL, LANES)),
                 _full_spec((1, LANES)),
                 _full_spec((LANES, WIDE)), _full_spec((LANES, WIDE)), _full_spec((1, WIDE)),
                 _full_spec((1, WIDE))] + [_row_spec(ROW_TILE, LANES)] * 3 + [
                 _full_spec((1, MLA_Q_RANK)), _full_spec((1, MLA_KV_RANK)),
                 _full_spec((MLA_Q_RANK, WIDE)), _full_spec((MLA_KV_RANK, WIDE)),
                 _full_spec((BRANCH_WIDTH, MLA_KV_RANK)), mem_spec, mem_spec]

    outs = [row_major(GATE_W), row_major(BRANCH_WIDTH),
            transposed(WIDE), row_major(WIDE), transposed(BRANCH_WIDTH),
            transposed(WIDE), row_major(WIDE), transposed(BRANCH_WIDTH)]
    out_specs = [spec for spec, _ in outs]
    out_shape = [shape for _, shape in outs]
    if entry_norm:
        out_specs.append(_row_spec(ROW_TILE, D_MODEL))
        out_shape.append(jax.ShapeDtypeStruct((t, D_MODEL), F32))
    return pl.pallas_call(
        functools.partial(_mixer_inputs_kernel, entry_norm=entry_norm, tiles_per_seq=nt),
        grid=(t // ROW_TILE,),
        in_specs=in_specs,
        out_specs=out_specs,
        out_shape=out_shape,
        scratch_shapes=[pltpu.VMEM((1, LANES), F32)],
        compiler_params=_params(1),
        name="mixer_inputs",
    )(*operands)


def _attn_kernel(qt_ref, k_ref, vt_ref, o_ref, sa_ref, sb_ref, ta_ref, tb_ref, m_ref, acc_ref,
                 *, granule_shift):
    tq = tk = sa_ref.shape[1]
    n_q_tiles = qt_ref.shape[2] // tq
    heads = range(N_GROUPS)
    buf_a = (sa_ref, ta_ref)
    buf_b = (sb_ref, tb_ref)
    ones_rows = jnp.ones((SUM_ROWS, tk), BF16)

    def start_of(tile_index, size):
        start = tile_index * size
        return start if isinstance(start, int) else pl.multiple_of(start, size)

    def scores(qi, j, buf, h, masked):
        q0 = start_of(qi, tq)
        ks = start_of(j, tk)
        grp = slice(h * LANES, (h + 1) * LANES)
        s = _dot(k_ref[0, pl.ds(ks, tk), grp], qt_ref[0, grp, pl.ds(q0, tq)])
        if masked:
            key = lax.broadcasted_iota(jnp.int32, (tk, tq), 0)
            qry = lax.broadcasted_iota(jnp.int32, (tk, tq), 1)
            s = jnp.where((key >> granule_shift) <= (qry >> granule_shift), s, NEG_INF)
        buf[0][h] = s
        buf[1][h] = jnp.max(s, axis=0, keepdims=True)

    def update(j, buf, h):
        ks = start_of(j, tk)
        m = m_ref[h]
        m_new = jnp.maximum(m, buf[1][h])
        alpha = jnp.exp2(m - m_new)
        e = jnp.exp2(buf[0][h] - m_new)
        m_ref[h] = m_new
        v_t = jnp.concatenate([vt_ref[0, h * HALF:(h + 1) * HALF, pl.ds(ks, tk)], ones_rows],
                              axis=0)
        acc_ref[h] = alpha * acc_ref[h] + _dot(v_t, e.astype(BF16))

    def tile_step(qi, j, cur, nxt=None, next_masked=False):
        if nxt is not None:
            for h in range(SCORE_LEAD):
                scores(qi, j + 1, nxt, h, next_masked)
        for h in heads:
            update(j, cur, h)
            if nxt is not None and h + SCORE_LEAD < N_GROUPS:
                scores(qi, j + 1, nxt, h + SCORE_LEAD, next_masked)

    def start_tile():
        m_ref[...] = jnp.full(m_ref.shape, NEG_INF, F32)
        acc_ref[...] = jnp.zeros(acc_ref.shape, F32)

    def finish_tile(qi, next_qi):
        for h in heads:
            scores(next_qi, 0, buf_a, h, False)
        q0 = start_of(qi, tq)
        for pair in range(N_GROUPS // 2):
            halves = []
            for h in (2 * pair, 2 * pair + 1):
                halves.append(acc_ref[h, :HALF, :] * (1.0 / acc_ref[h, HALF:HALF + 1, :]))
            o_t = jnp.concatenate(halves, axis=0)
            o_ref[0, pl.ds(q0, tq), pair * LANES:(pair + 1) * LANES] = o_t.T.astype(BF16)

    start_tile()
    for h in heads:
        scores(0, 0, buf_a, h, True)
    tile_step(0, 0, buf_a)
    finish_tile(0, min(1, n_q_tiles - 1))

    def q_tile(qi, _):
        start_tile()

        def two_tiles(i, _):
            tile_step(qi, 2 * i, buf_a, buf_b)
            tile_step(qi, 2 * i + 1, buf_b, buf_a)
            return 0

        n_pairs = (qi - 1) // 2
        lax.fori_loop(0, n_pairs, two_tiles, 0)
        j = 2 * n_pairs

        @pl.when(qi % 2 == 1)
        def _():
            tile_step(qi, j, buf_a, buf_b, next_masked=True)
            tile_step(qi, j + 1, buf_b)

        @pl.when(qi % 2 == 0)
        def _():
            tile_step(qi, j, buf_a, buf_b)
            tile_step(qi, j + 1, buf_b, buf_a, next_masked=True)
            tile_step(qi, j + 2, buf_a)

        finish_tile(qi, jnp.minimum(qi + 1, n_q_tiles - 1))
        return 0

    lax.fori_loop(1, n_q_tiles, q_tile, 0)


def _causal_attention(q_t, k_aug, v_t, batch, seq, granule):
    tile = ATTN_TILE
    k3 = k_aug.reshape(batch, seq, WIDE)
    out = pl.pallas_call(
        functools.partial(_attn_kernel, granule_shift=granule.bit_length() - 1),
        grid=(batch,),
        in_specs=[pl.BlockSpec((1, WIDE, seq), lambda b: (b, 0, 0)),
                  pl.BlockSpec((1, seq, WIDE), lambda b: (b, 0, 0)),
                  pl.BlockSpec((1, BRANCH_WIDTH, seq), lambda b: (b, 0, 0))],
        out_specs=pl.BlockSpec((1, seq, BRANCH_WIDTH), lambda b: (b, 0, 0)),
        out_shape=jax.ShapeDtypeStruct((batch, seq, BRANCH_WIDTH), BF16),
        scratch_shapes=[pltpu.VMEM((N_GROUPS, tile, tile), F32),
                        pltpu.VMEM((N_GROUPS, tile, tile), F32),
                        pltpu.VMEM((N_GROUPS, 1, tile), F32),
                        pltpu.VMEM((N_GROUPS, 1, tile), F32),
                        pltpu.VMEM((N_GROUPS, 1, tile), F32),
                        pltpu.VMEM((N_GROUPS, HALF + SUM_ROWS, tile), F32)],
        compiler_params=_params(1),
        name="causal_attention_g%d" % granule,
    )(q_t, k3, v_t)
    return out.reshape(batch * seq, BRANCH_WIDTH)


FF_CHUNK = 1024
ROW_PARTS = 2


def _merge_ffn_kernel(oa_ref, ob_ref, oc_ref, gates_ref, h_ref, wbr_ref, wout_ref, g1_ref, b1_ref,
                      w1_ref, w2_ref, g2_ref, b2_ref, o_ref, *, alpha):
    n_parts = ROW_PARTS
    part = o_ref.shape[0] // n_parts
    rows = [slice(p * part, (p + 1) * part) for p in range(n_parts)]

    def merge(r):
        merged = None
        for n, br_ref in enumerate((oa_ref, ob_ref, oc_ref)):
            gate = gates_ref[r, n * D_MODEL:(n + 1) * D_MODEL].astype(F32)
            term = gate * _dot(br_ref[r, :], wbr_ref[n])
            merged = term if merged is None else merged + term
        y = _dot(merged.astype(BF16), wout_ref[...])
        return _layer_norm(alpha * h_ref[r, :] + y, g1_ref[...], b1_ref[...])

    def mlp(h1):
        xb = h1.astype(BF16)
        acc = alpha * h1
        for c in range(0, D_FF, FF_CHUNK):
            u = jnp.maximum(_dot(xb, w1_ref[:, c:c + FF_CHUNK]), 0.0)
            acc = acc + _dot((u * u).astype(BF16), w2_ref[c:c + FF_CHUNK, :])
        return _layer_norm(acc, g2_ref[...], b2_ref[...])

    h1 = [merge(r) for r in rows]
    for r, h1_part in zip(rows, h1):
        o_ref[r, :] = mlp(h1_part)


def _merge_ffn(oa, ob, oc, gates, h, w_br, w_out, g1, b1, w1, w2, g2, b2, alpha):
    t = h.shape[0]
    br = _row_spec(ROW_TILE, BRANCH_WIDTH)
    vec = _full_spec((1, D_MODEL))
    return pl.pallas_call(
        functools.partial(_merge_ffn_kernel, alpha=alpha),
        grid=(t // ROW_TILE,),
        in_specs=[br, br, br, _row_spec(ROW_TILE, GATE_W), _row_spec(ROW_TILE, D_MODEL),
                  _full_spec((N_BRANCHES, BRANCH_WIDTH, D_MODEL)), _full_spec((D_MODEL, D_MODEL)),
                  vec, vec, _full_spec((D_MODEL, D_FF)), _full_spec((D_FF, D_MODEL)), vec, vec],
        out_specs=_row_spec(ROW_TILE, D_MODEL),
        out_shape=jax.ShapeDtypeStruct((t, D_MODEL), F32),
        compiler_params=_params(1),
        name="merge_ffn",
    )(oa, ob, oc, gates, h, w_br, w_out, g1.reshape(1, -1), b1.reshape(1, -1), w1, w2,
      g2.reshape(1, -1), b2.reshape(1, -1))


def _layer_weights(w_in, b_forget, w_uq, w_ukv):
    depth = w_in.shape[0]
    fox_w = 3 * BRANCH_WIDTH
    tail = LANES - FORGET_LANE - N_SPLIT * FOX_HEADS
    f0 = fox_w
    f1 = f0 + FOX_HEADS
    r0 = f1 + MLA_Q_RANK + MLA_KV_RANK
    r1 = r0 + MLA_ROPE_DIM
    f_log, k_rope = w_in[..., f0:f1], w_in[..., r0:r1]
    w_main = (w_in[..., :f0].astype(BF16), w_in[..., f1:r0].astype(BF16), w_in[..., r1:].astype(BF16))
    w_small = jnp.pad(jnp.concatenate([k_rope] + [f_log] * N_SPLIT, axis=-1).astype(BF16),
                      ((0, 0), (0, 0), (MLA_NOPE_DIM, tail)))
    b3 = jnp.pad(jnp.concatenate([b_forget] * N_SPLIT, axis=-1),
                 ((0, 0), (FORGET_LANE, tail))).reshape(depth, 1, LANES)

    dq = MLA_NOPE_DIM + MLA_ROPE_DIM
    wq = w_uq.astype(BF16).reshape(depth, MLA_Q_RANK, MLA_HEADS, dq)
    wq = jnp.pad(wq, ((0, 0), (0, 0), (0, 0), (0, LANES - dq))).reshape(depth, MLA_Q_RANK, -1)
    wkv = w_ukv.astype(BF16).reshape(depth, MLA_KV_RANK, MLA_HEADS, MLA_NOPE_DIM + MLA_V_DIM)
    wk = jnp.pad(wkv[..., :MLA_NOPE_DIM], ((0, 0), (0, 0), (0, 0), (0, LANES - MLA_NOPE_DIM)))
    wk = wk.reshape(depth, MLA_KV_RANK, -1)
    wv_t = jnp.swapaxes(wkv[..., MLA_NOPE_DIM:].reshape(depth, MLA_KV_RANK, -1), 1, 2)
    return w_main, w_small, b3, wq, wk, wv_t


def kernel(x, mem, positions, ln_in_g, ln_in_b, w_in, b_forget, w_uq, g_cq, w_ukv, g_ckv,
           w_mem_kv, w_br, w_out, ln1_g, ln1_b, w_ff1, w_ff2, ln2_g, ln2_b):
    batch, seq, _ = x.shape
    depth = w_in.shape[0]
    assert seq % ATTN_TILE == 0 and seq % ROW_TILE == 0
    alpha = float((2 * depth) ** 0.25)

    w_main, w_small, b3, wq, wk, wv_t = _layer_weights(w_in, b_forget, w_uq, w_ukv)
    w_br_b = w_br.astype(BF16)
    w_out_b = w_out.astype(BF16)
    w_ff1_b = w_ff1.astype(BF16)
    w_ff2_b = w_ff2.astype(BF16)

    rope_tabs = _rope_tables(positions)
    mk, mv = _mem_kv(mem, w_mem_kv)
    h = x.reshape(batch * seq, D_MODEL)
    for l in range(depth):
        outs = _mixer_inputs(h, (ln_in_g, ln_in_b) if l == 0 else None,
                             tuple(w[l] for w in w_main), w_small[l], b3[l],
                             rope_tabs, g_cq[l], g_ckv[l], wq[l], wk[l], wv_t[l], mk, mv, l,
                             batch, seq)
        gates, o_c, fq_t, fk_aug, fv_t, mq_t, mk_aug, mv_t = outs[:8]
        if l == 0:
            h = outs[8]
        o_a = _causal_attention(fq_t, fk_aug, fv_t, batch, seq, 1)
        o_b = _causal_attention(mq_t, mk_aug, mv_t, batch, seq, CHUNK)
        h = _merge_ffn(o_a, o_b, o_c, gates, h, w_br_b[l], w_out_b[l], ln1_g[l], ln1_b[l],
                       w_ff1_b[l], w_ff2_b[l], ln2_g[l], ln2_b[l], alpha)
    return h.reshape(batch, seq, D_MODEL)
```

```python
import functools
import math

import numpy as np
import jax
import jax.numpy as jnp
from jax import lax
from jax.experimental import pallas as pl
from jax.experimental.pallas import tpu as pltpu

D_MODEL = 1024
CHUNK = 64
FOX_HEADS = 8
FOX_HEAD_DIM = 64
MLA_HEADS = 8
MLA_NOPE_DIM = 64
MLA_ROPE_DIM = 32
MLA_V_DIM = 64
MLA_Q_RANK = 384
MLA_KV_RANK = 256
ROPE_BASE = 10000.0
MEM_HEADS = 4
MEM_HEAD_DIM = 128
N_BRANCHES = 3
BRANCH_WIDTH = 512
D_FF = 4 * D_MODEL
LN_EPS = 1e-5
RMS_EPS = 1e-6
NEG_INF = -1e30
LOG2E = math.log2(math.e)

LANES = 128
HALF = 64
N_GROUPS = 8
WIDE = N_GROUPS * LANES
MXU_WIDTH = 256
VMEM_LIMIT = 56 * 1024 * 1024

ROW_TILE = 512
ATTN_TILE = 256
SCORE_LEAD = 2
SUM_ROWS = 16

F32 = jnp.float32
BF16 = jnp.bfloat16


def _params(n_axes):
    return pltpu.CompilerParams(
        dimension_semantics=("arbitrary",) * n_axes, vmem_limit_bytes=VMEM_LIMIT)


def _dot(a, b):
    return jnp.dot(a, b, preferred_element_type=F32)


def _dot_nt(a, b):
    return lax.dot_general(a, b, (((1,), (1,)), ((), ())), preferred_element_type=F32)


def _layer_norm(z, g, b):
    mu = jnp.mean(z, axis=-1, keepdims=True)
    d = z - mu
    var = jnp.mean(d * d, axis=-1, keepdims=True)
    return d * lax.rsqrt(var + LN_EPS) * g + b


def _rms_norm(z, g):
    return z * lax.rsqrt(jnp.mean(z * z, axis=-1, keepdims=True) + RMS_EPS) * g


def _row_spec(tile, width, col_block=0):
    return pl.BlockSpec((tile, width), lambda i: (i, col_block))


def _full_spec(shape):
    return pl.BlockSpec(shape, lambda *_: (0,) * len(shape), pipeline_mode=pl.Buffered(1))


ROPE_HALF = MLA_ROPE_DIM // 2


def _rope_table_kernel(pos_ref, invf_ref, c_ref, slo_ref, shi_ref):
    ang = pos_ref[...].astype(F32) * invf_ref[...]
    lane = lax.broadcasted_iota(jnp.int32, ang.shape, 1)
    lo = (lane >= MLA_NOPE_DIM) & (lane < MLA_NOPE_DIM + ROPE_HALF)
    hi = (lane >= MLA_NOPE_DIM + ROPE_HALF) & (lane < MLA_NOPE_DIM + MLA_ROPE_DIM)
    sin = jnp.sin(ang)
    c_ref[...] = jnp.where(lane < MLA_NOPE_DIM, 1.0, jnp.where(lo | hi, jnp.cos(ang), 0.0))
    slo_ref[...] = jnp.where(lo, -sin, 0.0)
    shi_ref[...] = jnp.where(hi, sin, 0.0)


def _rotary(x, ct, s_lo, s_hi):
    width = x.shape[1]
    return (x * ct + pltpu.roll(x, width - ROPE_HALF, 1) * s_lo
            + pltpu.roll(x, ROPE_HALF, 1) * s_hi)


def _rope_tables(positions):
    t = positions.size
    inv_freq = ROPE_BASE ** (-jnp.arange(0, MLA_ROPE_DIM, 2, dtype=F32) / MLA_ROPE_DIM)
    invf = jnp.zeros((1, LANES), F32).at[0, MLA_NOPE_DIM:MLA_NOPE_DIM + MLA_ROPE_DIM].set(
        jnp.concatenate([inv_freq, inv_freq]))
    return pl.pallas_call(
        _rope_table_kernel,
        grid=(t // ROW_TILE,),
        in_specs=[_row_spec(ROW_TILE, 1), _full_spec((1, LANES))],
        out_specs=[_row_spec(ROW_TILE, LANES)] * 3,
        out_shape=[jax.ShapeDtypeStruct((t, LANES), F32)] * 3,
        compiler_params=_params(1),
        name="rope_tables",
    )(positions.reshape(t, 1), invf)


MEM_BATCH_GROUP = 4


def _mem_kv_kernel(mem_ref, w_ref, k_ref, v_ref):
    group, n_mem, _ = mem_ref.shape
    rows = mem_ref[...].reshape(group * n_mem, D_MODEL).astype(BF16)
    kv = _dot(rows, w_ref[0].astype(BF16))
    k = (kv[:, :BRANCH_WIDTH] * (MEM_HEAD_DIM ** -0.5 * LOG2E)).astype(BF16)
    k_ref[0] = k.reshape(group, n_mem, BRANCH_WIDTH)
    v_ref[0] = kv[:, BRANCH_WIDTH:].astype(BF16).reshape(group, n_mem, BRANCH_WIDTH)


def _mem_kv(mem, w_mem_kv):
    depth = w_mem_kv.shape[0]
    batch, n_mem, _ = mem.shape
    group = math.gcd(batch, MEM_BATCH_GROUP)
    return pl.pallas_call(
        _mem_kv_kernel,
        grid=(batch // group, depth),
        in_specs=[pl.BlockSpec((group, n_mem, D_MODEL), lambda b, l: (b, 0, 0)),
                  pl.BlockSpec((1, D_MODEL, 2 * BRANCH_WIDTH), lambda b, l: (l, 0, 0))],
        out_specs=[pl.BlockSpec((1, group, n_mem, BRANCH_WIDTH), lambda b, l: (l, b, 0, 0))] * 2,
        out_shape=[jax.ShapeDtypeStruct((depth, batch, n_mem, BRANCH_WIDTH), BF16)] * 2,
        compiler_params=_params(2),
        name="mem_kv",
    )(mem, w_mem_kv)


GATE_W = N_BRANCHES * D_MODEL
COL_FOX_Q = 0
COL_FOX_K = COL_FOX_Q + BRANCH_WIDTH
COL_FOX_V = COL_FOX_K + BRANCH_WIDTH
COL_CQ = COL_FOX_V + BRANCH_WIDTH
COL_CKV = COL_CQ + MLA_Q_RANK
COL_QMEM = COL_CKV + MLA_KV_RANK
COL_GATE = COL_QMEM + BRANCH_WIDTH
MAIN_W = COL_GATE + GATE_W
GATE_CHUNK = 256
N_SPLIT = 3
FORGET_LANE = MLA_NOPE_DIM + MLA_ROPE_DIM


def _split3(x):
    hi = x.astype(BF16)
    r = x - hi.astype(F32)
    mid = r.astype(BF16)
    lo = (r - mid.astype(F32)).astype(BF16)
    return hi, mid, lo


def _mixer_inputs_kernel(*refs, entry_norm, tiles_per_seq):
    if entry_norm:
        x_ref, lng_ref, lnb_ref, *refs = refs
    else:
        x_ref, *refs = refs
    (wfox_ref, wmla_ref, wtail_ref, ws_ref, bf_ref, pq_ref, pk_ref, oq_ref, ok_ref,
     ct_ref, slo_ref, shi_ref, gq_ref, gkv_ref, wq_ref, wk_ref, wvt_ref, mk_ref, mv_ref,
     gates_ref, oc_ref, fqt_ref, fka_ref, fvt_ref, mqt_ref, mka_ref, mvt_ref, *rest) = refs
    carry_ref = rest[-1]
    tm = x_ref.shape[0]

    @pl.when(pl.program_id(0) % tiles_per_seq == 0)
    def _():
        carry_ref[...] = jnp.zeros_like(carry_ref)

    h = x_ref[...]
    if entry_norm:
        h = _layer_norm(h, lng_ref[...], lnb_ref[...])
        rest[0][...] = h
    xb = h.astype(BF16)

    def mm(c0, n):
        for w_ref, first in ((wtail_ref, COL_QMEM), (wmla_ref, COL_CQ), (wfox_ref, COL_FOX_Q)):
            if c0 >= first:
                return _dot(xb, w_ref[:, c0 - first:c0 - first + n])

    gate_cols = iter(range(0, GATE_W, GATE_CHUNK))

    def emit_gates(n_chunks=2):
        for _ in range(n_chunks):
            c = next(gate_cols, None)
            if c is not None:
                g = mm(COL_GATE + c, GATE_CHUNK)
                gates_ref[:, c:c + GATE_CHUNK] = (0.5 * jnp.tanh(0.5 * g) + 0.5).astype(BF16)

    small = _dot(xb, ws_ref[...])
    xq = _rms_norm(mm(COL_CQ, MLA_Q_RANK), gq_ref[...]).astype(BF16)
    xkv = _rms_norm(mm(COL_CKV, MLA_KV_RANK), gkv_ref[...]).astype(BF16)

    x = small + bf_ref[...]
    logf = jnp.minimum(x, 0.0) - jnp.log1p(jnp.exp(-jnp.abs(x)))
    row = lax.broadcasted_iota(jnp.int32, (tm, tm), 0)
    col = lax.broadcasted_iota(jnp.int32, (tm, tm), 1)
    tri = jnp.where(col <= row, 1.0, 0.0).astype(BF16)
    hi, mid, lo = _split3(logf)
    cum = _dot(tri, hi) + _dot(tri, mid) + _dot(tri, lo) + carry_ref[...]
    carry_ref[...] = cum[tm - 1:tm, :]
    emit_gates()

    lane = lax.broadcasted_iota(jnp.int32, (tm, LANES), 1)
    span = lane - FORGET_LANE
    chi, cmid, clo = _split3(cum * LOG2E)
    pieces = jnp.where((span >= 0) & (span < FOX_HEADS), chi,
                       jnp.where((span >= FOX_HEADS) & (span < 2 * FOX_HEADS), cmid,
                                 jnp.where((span >= 2 * FOX_HEADS) & (span < 3 * FOX_HEADS), clo,
                                           jnp.zeros_like(chi))))
    left = lane < HALF
    heads_per_dot = MXU_WIDTH // HALF
    for h0 in range(0, FOX_HEADS, heads_per_dot):
        c0 = h0 * HALF
        qp2 = mm(COL_FOX_Q + c0, MXU_WIDTH) * (FOX_HEAD_DIM ** -0.5 * LOG2E)
        kp2 = mm(COL_FOX_K + c0, MXU_WIDTH)
        grp4 = slice(h0 * LANES, (h0 + heads_per_dot) * LANES)
        q_spare = _dot(pieces, pq_ref[:, grp4]) + oq_ref[:, grp4]
        k_spare = _dot(pieces, pk_ref[:, grp4]) + ok_ref[:, grp4]
        for i in range(heads_per_dot):
            head = h0 + i
            pair = slice((i // 2) * LANES, (i // 2 + 1) * LANES)
            loc = slice(i * LANES, (i + 1) * LANES)
            grp = slice(head * LANES, (head + 1) * LANES)
            own = left if head % 2 == 0 else jnp.logical_not(left)
            fqt_ref[0, grp, :] = jnp.where(own, qp2[:, pair], q_spare[:, loc]).T.astype(BF16)
            fka_ref[:, grp] = jnp.where(own, kp2[:, pair], k_spare[:, loc]).astype(BF16)
        emit_gates()
    fvt_ref[0] = mm(COL_FOX_V, BRANCH_WIDTH).T.astype(BF16)

    qm = mm(COL_QMEM, BRANCH_WIDTH).astype(BF16)
    for hm in range(MEM_HEADS):
        grp = slice(hm * MEM_HEAD_DIM, (hm + 1) * MEM_HEAD_DIM)
        s = _dot_nt(qm[:, grp], mk_ref[0, 0, :, grp])
        e = jnp.exp2(s - jnp.max(s, axis=1, keepdims=True))
        o = _dot(e.astype(BF16), mv_ref[0, 0, :, grp])
        oc_ref[:, grp] = (o * (1.0 / jnp.sum(e, axis=1, keepdims=True))).astype(BF16)
    emit_gates()

    ct, s_lo, s_hi = ct_ref[...], slo_ref[...], shi_ref[...]
    groups_per_dot = MXU_WIDTH // LANES
    ct2, s_lo2, s_hi2 = (jnp.concatenate([tab] * groups_per_dot, axis=1) for tab in (ct, s_lo, s_hi))
    scale = (MLA_NOPE_DIM + MLA_ROPE_DIM) ** -0.5 * LOG2E
    kpe = _rotary(small, ct, s_lo, s_hi)
    kpe2 = jnp.concatenate([kpe] * groups_per_dot, axis=1)
    for n, c in enumerate(range(0, WIDE, MXU_WIDTH)):
        q = _rotary(_dot(xq, wq_ref[:, c:c + MXU_WIDTH]), ct2, s_lo2, s_hi2) * scale
        mqt_ref[0, c:c + MXU_WIDTH, :] = q.T.astype(BF16)
        mka_ref[:, c:c + MXU_WIDTH] = (_dot(xkv, wk_ref[:, c:c + MXU_WIDTH]) + kpe2).astype(BF16)
        if n == 1:
            emit_gates(3)
    mvt_ref[0] = _dot_nt(wvt_ref[...], xkv).astype(BF16)
    emit_gates(GATE_W // GATE_CHUNK)


def _fox_placement():
    pq = np.zeros((LANES, WIDE), np.float32)
    pk = np.zeros((LANES, WIDE), np.float32)
    oq = np.zeros((1, WIDE), np.float32)
    ok = np.zeros((1, WIDE), np.float32)
    for h in range(FOX_HEADS):
        base = h * LANES + (HALF if h % 2 == 0 else 0)
        for j in range(N_SPLIT):
            src = FORGET_LANE + j * FOX_HEADS + h
            pq[src, base + j] = 1.0
            ok[0, base + j] = 1.0
            oq[0, base + N_SPLIT + j] = 1.0
            pk[src, base + N_SPLIT + j] = -1.0
    return (jnp.asarray(pq, BF16), jnp.asarray(pk, BF16), jnp.asarray(oq), jnp.asarray(ok))


def _mixer_inputs(h, ln_in, w_main, w_small, b_forget3, rope_tabs, g_cq, g_ckv, wq, wk, wvt,
                  mk, mv, layer, batch, seq):
    t = h.shape[0]
    nt = seq // ROW_TILE
    n_mem = mk.shape[2]
    entry_norm = ln_in is not None
    pq, pk, oq, ok = _fox_placement()
    mem_spec = pl.BlockSpec((1, 1, n_mem, BRANCH_WIDTH), lambda i: (layer, i // nt, 0, 0))

    def transposed(rows):
        return (pl.BlockSpec((1, rows, ROW_TILE), lambda i: (i // nt, 0, i % nt)),
                jax.ShapeDtypeStruct((batch, rows, seq), BF16))

    def row_major(width):
        return _row_spec(ROW_TILE, width), jax.ShapeDtypeStruct((t, width), BF16)

    operands = [h]
    in_specs = [_row_spec(ROW_TILE, D_MODEL)]
    if entry_norm:
        operands += [ln_in[0].reshape(1, -1), ln_in[1].reshape(1, -1)]
        in_specs += [_full_spec((1, D_MODEL))] * 2
    operands += [*w_main, w_small, b_forget3, pq, pk, oq, ok, *rope_tabs,
                 g_cq.reshape(1, -1), g_ckv.reshape(1, -1), wq, wk, wvt, mk, mv]
    in_specs += [_full_spec((D_MODEL, COL_CQ - COL_FOX_Q)), _full_spec((D_MODEL, COL_QMEM - COL_CQ)),
                 _full_spec((D_MODEL, MAIN_W - COL_QMEM)), _full_spec((D_MODEL, LANES)),
                 _full_spec((1, LANES)),
                 _full_spec((LANES, WIDE)), _full_spec((LANES, WIDE)), _full_spec((1, WIDE)),
                 _full_spec((1, WIDE))] + [_row_spec(ROW_TILE, LANES)] * 3 + [
                 _full_spec((1, MLA_Q_RANK)), _full_spec((1, MLA_KV_RANK)),
                 _full_spec((MLA_Q_RANK, WIDE)), _full_spec((MLA_KV_RANK, WIDE)),
                 _full_spec((BRANCH_WIDTH, MLA_KV_RANK)), mem_spec, mem_spec]

    outs = [row_major(GATE_W), row_major(BRANCH_WIDTH),
            transposed(WIDE), row_major(WIDE), transposed(BRANCH_WIDTH),
            transposed(WIDE), row_major(WIDE), transposed(BRANCH_WIDTH)]
    out_specs = [spec for spec, _ in outs]
    out_shape = [shape for _, shape in outs]
    if entry_norm:
        out_specs.append(_row_spec(ROW_TILE, D_MODEL))
        out_shape.append(jax.ShapeDtypeStruct((t, D_MODEL), F32))
    return pl.pallas_call(
        functools.partial(_mixer_inputs_kernel, entry_norm=entry_norm, tiles_per_seq=nt),
        grid=(t // ROW_TILE,),
        in_specs=in_specs,
        out_specs=out_specs,
        out_shape=out_shape,
        scratch_shapes=[pltpu.VMEM((1, LANES), F32)],
        compiler_params=_params(1),
        name="mixer_inputs",
    )(*operands)


def _attn_kernel(qt_ref, k_ref, vt_ref, o_ref, sa_ref, sb_ref, ta_ref, tb_ref, m_ref, acc_ref,
                 *, granule_shift):
    tq = tk = sa_ref.shape[1]
    n_q_tiles = qt_ref.shape[2] // tq
    heads = range(N_GROUPS)
    buf_a = (sa_ref, ta_ref)
    buf_b = (sb_ref, tb_ref)
    ones_rows = jnp.ones((SUM_ROWS, tk), BF16)

    def start_of(tile_index, size):
        start = tile_index * size
        return start if isinstance(start, int) else pl.multiple_of(start, size)

    def scores(qi, j, buf, h, masked):
        q0 = start_of(qi, tq)
        ks = start_of(j, tk)
        grp = slice(h * LANES, (h + 1) * LANES)
        s = _dot(k_ref[0, pl.ds(ks, tk), grp], qt_ref[0, grp, pl.ds(q0, tq)])
        if masked:
            key = lax.broadcasted_iota(jnp.int32, (tk, tq), 0)
            qry = lax.broadcasted_iota(jnp.int32, (tk, tq), 1)
            s = jnp.where((key >> granule_shift) <= (qry >> granule_shift), s, NEG_INF)
        buf[0][h] = s
        buf[1][h] = jnp.max(s, axis=0, keepdims=True)

    def update(j, buf, h):
        ks = start_of(j, tk)
        m = m_ref[h]
        m_new = jnp.maximum(m, buf[1][h])
        alpha = jnp.exp2(m - m_new)
        e = jnp.exp2(buf[0][h] - m_new)
        m_ref[h] = m_new
        v_t = jnp.concatenate([vt_ref[0, h * HALF:(h + 1) * HALF, pl.ds(ks, tk)], ones_rows],
                              axis=0)
        acc_ref[h] = alpha * acc_ref[h] + _dot(v_t, e.astype(BF16))

    def tile_step(qi, j, cur, nxt=None, next_masked=False):
        if nxt is not None:
            for h in range(SCORE_LEAD):
                scores(qi, j + 1, nxt, h, next_masked)
        for h in heads:
            update(j, cur, h)
            if nxt is not None and h + SCORE_LEAD < N_GROUPS:
                scores(qi, j + 1, nxt, h + SCORE_LEAD, next_masked)

    def start_tile():
        m_ref[...] = jnp.full(m_ref.shape, NEG_INF, F32)
        acc_ref[...] = jnp.zeros(acc_ref.shape, F32)

    def finish_tile(qi, next_qi=None):
        if next_qi is not None:
            for h in heads:
                scores(next_qi, 0, buf_a, h, False)
        q0 = start_of(qi, tq)
        for pair in range(N_GROUPS // 2):
            halves = []
            for h in (2 * pair, 2 * pair + 1):
                halves.append(acc_ref[h, :HALF, :] * (1.0 / acc_ref[h, HALF:HALF + 1, :]))
            o_t = jnp.concatenate(halves, axis=0)
            o_ref[0, pl.ds(q0, tq), pair * LANES:(pair + 1) * LANES] = o_t.T.astype(BF16)

    start_tile()
    for h in heads:
        scores(0, 0, buf_a, h, True)
    tile_step(0, 0, buf_a)
    finish_tile(0, 1 if n_q_tiles > 1 else None)

    def q_tile(qi, _):
        start_tile()

        def two_tiles(i, _):
            tile_step(qi, 2 * i, buf_a, buf_b)
            tile_step(qi, 2 * i + 1, buf_b, buf_a)
            return 0

        n_pairs = (qi - 1) // 2
        lax.fori_loop(0, n_pairs, two_tiles, 0)
        j = 2 * n_pairs

        @pl.when(qi % 2 == 1)
        def _():
            tile_step(qi, j, buf_a, buf_b, next_masked=True)
            tile_step(qi, j + 1, buf_b)

        @pl.when(qi % 2 == 0)
        def _():
            tile_step(qi, j, buf_a, buf_b)
            tile_step(qi, j + 1, buf_b, buf_a, next_masked=True)
            tile_step(qi, j + 2, buf_a)

        @pl.when(qi < n_q_tiles - 1)
        def _():
            finish_tile(qi, qi + 1)

        @pl.when(qi == n_q_tiles - 1)
        def _():
            finish_tile(qi)

        return 0

    lax.fori_loop(1, n_q_tiles, q_tile, 0)


def _causal_attention(q_t, k_aug, v_t, batch, seq, granule):
    tile = ATTN_TILE
    k3 = k_aug.reshape(batch, seq, WIDE)
    out = pl.pallas_call(
        functools.partial(_attn_kernel, granule_shift=granule.bit_length() - 1),
        grid=(batch,),
        in_specs=[pl.BlockSpec((1, WIDE, seq), lambda b: (b, 0, 0)),
                  pl.BlockSpec((1, seq, WIDE), lambda b: (b, 0, 0)),
                  pl.BlockSpec((1, BRANCH_WIDTH, seq), lambda b: (b, 0, 0))],
        out_specs=pl.BlockSpec((1, seq, BRANCH_WIDTH), lambda b: (b, 0, 0)),
        out_shape=jax.ShapeDtypeStruct((batch, seq, BRANCH_WIDTH), BF16),
        scratch_shapes=[pltpu.VMEM((N_GROUPS, tile, tile), F32),
                        pltpu.VMEM((N_GROUPS, tile, tile), F32),
                        pltpu.VMEM((N_GROUPS, 1, tile), F32),
                        pltpu.VMEM((N_GROUPS, 1, tile), F32),
                        pltpu.VMEM((N_GROUPS, 1, tile), F32),
                        pltpu.VMEM((N_GROUPS, HALF + SUM_ROWS, tile), F32)],
        compiler_params=_params(1),
        name="causal_attention_g%d" % granule,
    )(q_t, k3, v_t)
    return out.reshape(batch * seq, BRANCH_WIDTH)


FF_CHUNK = 1024
ROW_PARTS = 2


def _merge_ffn_kernel(oa_ref, ob_ref, oc_ref, gates_ref, h_ref, wbr_ref, wout_ref, g1_ref, b1_ref,
                      w1_ref, w2_ref, g2_ref, b2_ref, o_ref, *, alpha):
    n_parts = ROW_PARTS
    part = o_ref.shape[0] // n_parts
    rows = [slice(p * part, (p + 1) * part) for p in range(n_parts)]

    def merge(r):
        merged = None
        for n, br_ref in enumerate((oa_ref, ob_ref, oc_ref)):
            gate = gates_ref[r, n * D_MODEL:(n + 1) * D_MODEL].astype(F32)
            term = gate * _dot(br_ref[r, :], wbr_ref[n])
            merged = term if merged is None else merged + term
        y = _dot(merged.astype(BF16), wout_ref[...])
        return _layer_norm(alpha * h_ref[r, :] + y, g1_ref[...], b1_ref[...])

    def mlp(h1):
        xb = h1.astype(BF16)
        acc = alpha * h1
        for c in range(0, D_FF, FF_CHUNK):
            u = jnp.maximum(_dot(xb, w1_ref[:, c:c + FF_CHUNK]), 0.0)
            acc = acc + _dot((u * u).astype(BF16), w2_ref[c:c + FF_CHUNK, :])
        return _layer_norm(acc, g2_ref[...], b2_ref[...])

    h1 = [merge(r) for r in rows]
    for r, h1_part in zip(rows, h1):
        o_ref[r, :] = mlp(h1_part)


def _merge_ffn(oa, ob, oc, gates, h, w_br, w_out, g1, b1, w1, w2, g2, b2, alpha):
    t = h.shape[0]
    br = _row_spec(ROW_TILE, BRANCH_WIDTH)
    vec = _full_spec((1, D_MODEL))
    return pl.pallas_call(
        functools.partial(_merge_ffn_kernel, alpha=alpha),
        grid=(t // ROW_TILE,),
        in_specs=[br, br, br, _row_spec(ROW_TILE, GATE_W), _row_spec(ROW_TILE, D_MODEL),
                  _full_spec((N_BRANCHES, BRANCH_WIDTH, D_MODEL)), _full_spec((D_MODEL, D_MODEL)),
                  vec, vec, _full_spec((D_MODEL, D_FF)), _full_spec((D_FF, D_MODEL)), vec, vec],
        out_specs=_row_spec(ROW_TILE, D_MODEL),
        out_shape=jax.ShapeDtypeStruct((t, D_MODEL), F32),
        compiler_params=_params(1),
        name="merge_ffn",
    )(oa, ob, oc, gates, h, w_br, w_out, g1.reshape(1, -1), b1.reshape(1, -1), w1, w2,
      g2.reshape(1, -1), b2.reshape(1, -1))


def _layer_weights(w_in, b_forget, w_uq, w_ukv):
    depth = w_in.shape[0]
    fox_w = 3 * BRANCH_WIDTH
    tail = LANES - FORGET_LANE - N_SPLIT * FOX_HEADS
    f0 = fox_w
    f1 = f0 + FOX_HEADS
    r0 = f1 + MLA_Q_RANK + MLA_KV_RANK
    r1 = r0 + MLA_ROPE_DIM
    f_log, k_rope = w_in[..., f0:f1], w_in[..., r0:r1]
    w_main = (w_in[..., :f0].astype(BF16), w_in[..., f1:r0].astype(BF16), w_in[..., r1:].astype(BF16))
    w_small = jnp.pad(jnp.concatenate([k_rope] + [f_log] * N_SPLIT, axis=-1).astype(BF16),
                      ((0, 0), (0, 0), (MLA_NOPE_DIM, tail)))
    b3 = jnp.pad(jnp.concatenate([b_forget] * N_SPLIT, axis=-1),
                 ((0, 0), (FORGET_LANE, tail))).reshape(depth, 1, LANES)

    dq = MLA_NOPE_DIM + MLA_ROPE_DIM
    wq = w_uq.astype(BF16).reshape(depth, MLA_Q_RANK, MLA_HEADS, dq)
    wq = jnp.pad(wq, ((0, 0), (0, 0), (0, 0), (0, LANES - dq))).reshape(depth, MLA_Q_RANK, -1)
    wkv = w_ukv.astype(BF16).reshape(depth, MLA_KV_RANK, MLA_HEADS, MLA_NOPE_DIM + MLA_V_DIM)
    wk = jnp.pad(wkv[..., :MLA_NOPE_DIM], ((0, 0), (0, 0), (0, 0), (0, LANES - MLA_NOPE_DIM)))
    wk = wk.reshape(depth, MLA_KV_RANK, -1)
    wv_t = jnp.swapaxes(wkv[..., MLA_NOPE_DIM:].reshape(depth, MLA_KV_RANK, -1), 1, 2)
    return w_main, w_small, b3, wq, wk, wv_t


def kernel(x, mem, positions, ln_in_g, ln_in_b, w_in, b_forget, w_uq, g_cq, w_ukv, g_ckv,
           w_mem_kv, w_br, w_out, ln1_g, ln1_b, w_ff1, w_ff2, ln2_g, ln2_b):
    batch, seq, _ = x.shape
    depth = w_in.shape[0]
    assert seq % ATTN_TILE == 0 and seq % ROW_TILE == 0
    alpha = float((2 * depth) ** 0.25)

    w_main, w_small, b3, wq, wk, wv_t = _layer_weights(w_in, b_forget, w_uq, w_ukv)
    w_br_b = w_br.astype(BF16)
    w_out_b = w_out.astype(BF16)
    w_ff1_b = w_ff1.astype(BF16)
    w_ff2_b = w_ff2.astype(BF16)

    rope_tabs = _rope_tables(positions)
    mk, mv = _mem_kv(mem, w_mem_kv)
    h = x.reshape(batch * seq, D_MODEL)
    for l in range(depth):
        outs = _mixer_inputs(h, (ln_in_g, ln_in_b) if l == 0 else None,
                             tuple(w[l] for w in w_main), w_small[l], b3[l],
                             rope_tabs, g_cq[l], g_ckv[l], wq[l], wk[l], wv_t[l], mk, mv, l,
                             batch, seq)
        gates, o_c, fq_t, fk_aug, fv_t, mq_t, mk_aug, mv_t = outs[:8]
        if l == 0:
            h = outs[8]
        o_a = _causal_attention(fq_t, fk_aug, fv_t, batch, seq, 1)
        o_b = _causal_attention(mq_t, mk_aug, mv_t, batch, seq, CHUNK)
        h = _merge_ffn(o_a, o_b, o_c, gates, h, w_br_b[l], w_out_b[l], ln1_g[l], ln1_b[l],
                       w_ff1_b[l], w_ff2_b[l], ln2_g[l], ln2_b[l], alpha)
    return h.reshape(batch, seq, D_MODEL)
```

```python
import functools
import math

import numpy as np
import jax
import jax.numpy as jnp
from jax import lax
from jax.experimental import pallas as pl
from jax.experimental.pallas import tpu as pltpu

D_MODEL = 1024
CHUNK = 64
FOX_HEADS = 8
FOX_HEAD_DIM = 64
MLA_HEADS = 8
MLA_NOPE_DIM = 64
MLA_ROPE_DIM = 32
MLA_V_DIM = 64
MLA_Q_RANK = 384
MLA_KV_RANK = 256
ROPE_BASE = 10000.0
MEM_HEADS = 4
MEM_HEAD_DIM = 128
N_BRANCHES = 3
BRANCH_WIDTH = 512
D_FF = 4 * D_MODEL
LN_EPS = 1e-5
RMS_EPS = 1e-6
NEG_INF = -1e30
LOG2E = math.log2(math.e)

LANES = 128
HALF = 64
N_GROUPS = 8
WIDE = N_GROUPS * LANES
MXU_WIDTH = 256
VMEM_LIMIT = 56 * 1024 * 1024

ROW_TILE = 512
ATTN_TILE = 256
SCORE_LEAD = 2
SUM_ROWS = 16

F32 = jnp.float32
BF16 = jnp.bfloat16


def _params(n_axes):
    return pltpu.CompilerParams(
        dimension_semantics=("arbitrary",) * n_axes, vmem_limit_bytes=VMEM_LIMIT)


def _dot(a, b):
    return jnp.dot(a, b, preferred_element_type=F32)


def _dot_nt(a, b):
    return lax.dot_general(a, b, (((1,), (1,)), ((), ())), preferred_element_type=F32)


def _layer_norm(z, g, b):
    mu = jnp.mean(z, axis=-1, keepdims=True)
    d = z - mu
    var = jnp.mean(d * d, axis=-1, keepdims=True)
    return d * lax.rsqrt(var + LN_EPS) * g + b


def _rms_norm(z, g):
    return z * lax.rsqrt(jnp.mean(z * z, axis=-1, keepdims=True) + RMS_EPS) * g


def _row_spec(tile, width, col_block=0):
    return pl.BlockSpec((tile, width), lambda i: (i, col_block))


def _full_spec(shape):
    return pl.BlockSpec(shape, lambda *_: (0,) * len(shape), pipeline_mode=pl.Buffered(1))


ROPE_HALF = MLA_ROPE_DIM // 2


def _rope_table_kernel(pos_ref, invf_ref, c_ref, slo_ref, shi_ref):
    ang = pos_ref[...].astype(F32) * invf_ref[...]
    lane = lax.broadcasted_iota(jnp.int32, ang.shape, 1)
    lo = (lane >= MLA_NOPE_DIM) & (lane < MLA_NOPE_DIM + ROPE_HALF)
    hi = (lane >= MLA_NOPE_DIM + ROPE_HALF) & (lane < MLA_NOPE_DIM + MLA_ROPE_DIM)
    sin = jnp.sin(ang)
    c_ref[...] = jnp.where(lane < MLA_NOPE_DIM, 1.0, jnp.where(lo | hi, jnp.cos(ang), 0.0))
    slo_ref[...] = jnp.where(lo, -sin, 0.0)
    shi_ref[...] = jnp.where(hi, sin, 0.0)


def _rotary(x, ct, s_lo, s_hi):
    width = x.shape[1]
    return (x * ct + pltpu.roll(x, width - ROPE_HALF, 1) * s_lo
            + pltpu.roll(x, ROPE_HALF, 1) * s_hi)


def _rope_tables(positions):
    t = positions.size
    inv_freq = ROPE_BASE ** (-jnp.arange(0, MLA_ROPE_DIM, 2, dtype=F32) / MLA_ROPE_DIM)
    invf = jnp.zeros((1, LANES), F32).at[0, MLA_NOPE_DIM:MLA_NOPE_DIM + MLA_ROPE_DIM].set(
        jnp.concatenate([inv_freq, inv_freq]))
    return pl.pallas_call(
        _rope_table_kernel,
        grid=(t // ROW_TILE,),
        in_specs=[_row_spec(ROW_TILE, 1), _full_spec((1, LANES))],
        out_specs=[_row_spec(ROW_TILE, LANES)] * 3,
        out_shape=[jax.ShapeDtypeStruct((t, LANES), F32)] * 3,
        compiler_params=_params(1),
        name="rope_tables",
    )(positions.reshape(t, 1), invf)


MEM_BATCH_GROUP = 4


def _mem_kv_kernel(mem_ref, w_ref, k_ref, v_ref):
    group, n_mem, _ = mem_ref.shape
    rows = mem_ref[...].reshape(group * n_mem, D_MODEL).astype(BF16)
    kv = _dot(rows, w_ref[0].astype(BF16))
    k = (kv[:, :BRANCH_WIDTH] * (MEM_HEAD_DIM ** -0.5 * LOG2E)).astype(BF16)
    k_ref[0] = k.reshape(group, n_mem, BRANCH_WIDTH)
    v_ref[0] = kv[:, BRANCH_WIDTH:].astype(BF16).reshape(group, n_mem, BRANCH_WIDTH)


def _mem_kv(mem, w_mem_kv):
    depth = w_mem_kv.shape[0]
    batch, n_mem, _ = mem.shape
    group = math.gcd(batch, MEM_BATCH_GROUP)
    return pl.pallas_call(
        _mem_kv_kernel,
        grid=(batch // group, depth),
        in_specs=[pl.BlockSpec((group, n_mem, D_MODEL), lambda b, l: (b, 0, 0)),
                  pl.BlockSpec((1, D_MODEL, 2 * BRANCH_WIDTH), lambda b, l: (l, 0, 0))],
        out_specs=[pl.BlockSpec((1, group, n_mem, BRANCH_WIDTH), lambda b, l: (l, b, 0, 0))] * 2,
        out_shape=[jax.ShapeDtypeStruct((depth, batch, n_mem, BRANCH_WIDTH), BF16)] * 2,
        compiler_params=_params(2),
        name="mem_kv",
    )(mem, w_mem_kv)


GATE_W = N_BRANCHES * D_MODEL
COL_FOX_Q = 0
COL_FOX_K = COL_FOX_Q + BRANCH_WIDTH
COL_FOX_V = COL_FOX_K + BRANCH_WIDTH
COL_CQ = COL_FOX_V + BRANCH_WIDTH
COL_CKV = COL_CQ + MLA_Q_RANK
COL_QMEM = COL_CKV + MLA_KV_RANK
COL_GATE = COL_QMEM + BRANCH_WIDTH
MAIN_W = COL_GATE + GATE_W
GATE_CHUNK = 256
N_SPLIT = 3
FORGET_LANE = MLA_NOPE_DIM + MLA_ROPE_DIM


def _split3(x):
    hi = x.astype(BF16)
    r = x - hi.astype(F32)
    mid = r.astype(BF16)
    lo = (r - mid.astype(F32)).astype(BF16)
    return hi, mid, lo


def _mixer_inputs_kernel(*refs, entry_norm, tiles_per_seq):
    if entry_norm:
        x_ref, lng_ref, lnb_ref, *refs = refs
    else:
        x_ref, *refs = refs
    (wfox_ref, wmla_ref, wtail_ref, ws_ref, bf_ref, pq_ref, pk_ref, oq_ref, ok_ref,
     ct_ref, slo_ref, shi_ref, gq_ref, gkv_ref, wq_ref, wk_ref, wvt_ref, mk_ref, mv_ref,
     gates_ref, oc_ref, fqt_ref, fka_ref, fvt_ref, mqt_ref, mka_ref, mvt_ref, *rest) = refs
    carry_ref = rest[-1]
    tm = x_ref.shape[0]

    @pl.when(pl.program_id(0) % tiles_per_seq == 0)
    def _():
        carry_ref[...] = jnp.zeros_like(carry_ref)

    h = x_ref[...]
    if entry_norm:
        h = _layer_norm(h, lng_ref[...], lnb_ref[...])
        rest[0][...] = h
    xb = h.astype(BF16)

    def mm(c0, n):
        for w_ref, first in ((wtail_ref, COL_QMEM), (wmla_ref, COL_CQ), (wfox_ref, COL_FOX_Q)):
            if c0 >= first:
                return _dot(xb, w_ref[:, c0 - first:c0 - first + n])

    gate_cols = iter(range(0, GATE_W, GATE_CHUNK))

    def emit_gates(n_chunks=2):
        for _ in range(n_chunks):
            c = next(gate_cols, None)
            if c is not None:
                g = mm(COL_GATE + c, GATE_CHUNK)
                gates_ref[:, c:c + GATE_CHUNK] = (0.5 * jnp.tanh(0.5 * g) + 0.5).astype(BF16)

    small = _dot(xb, ws_ref[...])
    xq = _rms_norm(mm(COL_CQ, MLA_Q_RANK), gq_ref[...]).astype(BF16)
    xkv = _rms_norm(mm(COL_CKV, MLA_KV_RANK), gkv_ref[...]).astype(BF16)

    x = small + bf_ref[...]
    logf = jnp.minimum(x, 0.0) - jnp.log1p(jnp.exp(-jnp.abs(x)))
    row = lax.broadcasted_iota(jnp.int32, (tm, tm), 0)
    col = lax.broadcasted_iota(jnp.int32, (tm, tm), 1)
    tri = jnp.where(col <= row, 1.0, 0.0).astype(BF16)
    hi, mid, lo = _split3(logf)
    cum = _dot(tri, hi) + _dot(tri, mid) + _dot(tri, lo) + carry_ref[...]
    carry_ref[...] = cum[tm - 1:tm, :]
    emit_gates()

    lane = lax.broadcasted_iota(jnp.int32, (tm, LANES), 1)
    span = lane - FORGET_LANE
    chi, cmid, clo = _split3(cum * LOG2E)
    pieces = jnp.where((span >= 0) & (span < FOX_HEADS), chi,
                       jnp.where((span >= FOX_HEADS) & (span < 2 * FOX_HEADS), cmid,
                                 jnp.where((span >= 2 * FOX_HEADS) & (span < 3 * FOX_HEADS), clo,
                                           jnp.zeros_like(chi))))
    left = lane < HALF
    heads_per_dot = MXU_WIDTH // HALF
    for h0 in range(0, FOX_HEADS, heads_per_dot):
        c0 = h0 * HALF
        qp2 = mm(COL_FOX_Q + c0, MXU_WIDTH) * (FOX_HEAD_DIM ** -0.5 * LOG2E)
        kp2 = mm(COL_FOX_K + c0, MXU_WIDTH)
        grp4 = slice(h0 * LANES, (h0 + heads_per_dot) * LANES)
        q_spare = _dot(pieces, pq_ref[:, grp4]) + oq_ref[:, grp4]
        k_spare = _dot(pieces, pk_ref[:, grp4]) + ok_ref[:, grp4]
        for i in range(heads_per_dot):
            head = h0 + i
            pair = slice((i // 2) * LANES, (i // 2 + 1) * LANES)
            loc = slice(i * LANES, (i + 1) * LANES)
            grp = slice(head * LANES, (head + 1) * LANES)
            own = left if head % 2 == 0 else jnp.logical_not(left)
            fqt_ref[0, grp, :] = jnp.where(own, qp2[:, pair], q_spare[:, loc]).T.astype(BF16)
            fka_ref[:, grp] = jnp.where(own, kp2[:, pair], k_spare[:, loc]).astype(BF16)
        emit_gates()
    fvt_ref[0] = mm(COL_FOX_V, BRANCH_WIDTH).T.astype(BF16)

    qm = mm(COL_QMEM, BRANCH_WIDTH).astype(BF16)
    for hm in range(MEM_HEADS):
        grp = slice(hm * MEM_HEAD_DIM, (hm + 1) * MEM_HEAD_DIM)
        s = _dot_nt(qm[:, grp], mk_ref[0, 0, :, grp])
        e = jnp.exp2(s - jnp.max(s, axis=1, keepdims=True))
        o = _dot(e.astype(BF16), mv_ref[0, 0, :, grp])
        oc_ref[:, grp] = (o * (1.0 / jnp.sum(e, axis=1, keepdims=True))).astype(BF16)
    emit_gates()

    ct, s_lo, s_hi = ct_ref[...], slo_ref[...], shi_ref[...]
    groups_per_dot = MXU_WIDTH // LANES
    ct2, s_lo2, s_hi2 = (jnp.concatenate([tab] * groups_per_dot, axis=1) for tab in (ct, s_lo, s_hi))
    scale = (MLA_NOPE_DIM + MLA_ROPE_DIM) ** -0.5 * LOG2E
    kpe = _rotary(small, ct, s_lo, s_hi)
    kpe2 = jnp.concatenate([kpe] * groups_per_dot, axis=1)
    for n, c in enumerate(range(0, WIDE, MXU_WIDTH)):
        q = _rotary(_dot(xq, wq_ref[:, c:c + MXU_WIDTH]), ct2, s_lo2, s_hi2) * scale
        mqt_ref[0, c:c + MXU_WIDTH, :] = q.T.astype(BF16)
        mka_ref[:, c:c + MXU_WIDTH] = (_dot(xkv, wk_ref[:, c:c + MXU_WIDTH]) + kpe2).astype(BF16)
        if n == 1:
            emit_gates(3)
    mvt_ref[0] = _dot_nt(wvt_ref[...], xkv).astype(BF16)
    emit_gates(GATE_W // GATE_CHUNK)


def _fox_placement():
    pq = np.zeros((LANES, WIDE), np.float32)
    pk = np.zeros((LANES, WIDE), np.float32)
    oq = np.zeros((1, WIDE), np.float32)
    ok = np.zeros((1, WIDE), np.float32)
    for h in range(FOX_HEADS):
        base = h * LANES + (HALF if h % 2 == 0 else 0)
        for j in range(N_SPLIT):
            src = FORGET_LANE + j * FOX_HEADS + h
            pq[src, base + j] = 1.0
            ok[0, base + j] = 1.0
            oq[0, base + N_SPLIT + j] = 1.0
            pk[src, base + N_SPLIT + j] = -1.0
    return (jnp.asarray(pq, BF16), jnp.asarray(pk, BF16), jnp.asarray(oq), jnp.asarray(ok))


def _mixer_inputs(h, ln_in, w_main, w_small, b_forget3, rope_tabs, g_cq, g_ckv, wq, wk, wvt,
                  mk, mv, layer, batch, seq):
    t = h.shape[0]
    nt = seq // ROW_TILE
    n_mem = mk.shape[2]
    entry_norm = ln_in is not None
    pq, pk, oq, ok = _fox_placement()
    mem_spec = pl.BlockSpec((1, 1, n_mem, BRANCH_WIDTH), lambda i: (layer, i // nt, 0, 0))

    def transposed(rows):
        return (pl.BlockSpec((1, rows, ROW_TILE), lambda i: (i // nt, 0, i % nt)),
                jax.ShapeDtypeStruct((batch, rows, seq), BF16))

    def row_major(width):
        return _row_spec(ROW_TILE, width), jax.ShapeDtypeStruct((t, width), BF16)

    operands = [h]
    in_specs = [_row_spec(ROW_TILE, D_MODEL)]
    if entry_norm:
        operands += [ln_in[0].reshape(1, -1), ln_in[1].reshape(1, -1)]
        in_specs += [_full_spec((1, D_MODEL))] * 2
    operands += [*w_main, w_small, b_forget3, pq, pk, oq, ok, *rope_tabs,
                 g_cq.reshape(1, -1), g_ckv.reshape(1, -1), wq, wk, wvt, mk, mv]
    in_specs += [_full_spec((D_MODEL, COL_CQ - COL_FOX_Q)), _full_spec((D_MODEL, COL_QMEM - COL_CQ)),
                 _full_spec((D_MODEL, MAIN_W - COL_QMEM)), _full_spec((D_MODEL, LANES)),
                 _full_spec((1, LANES)),
                 _full_spec((LANES, WIDE)), _full_spec((LANES, WIDE)), _full_spec((1, WIDE)),
                 _full_spec((1, WIDE))] + [_row_spec(ROW_TILE, LANES)] * 3 + [
                 _full_spec((1, MLA_Q_RANK)), _full_spec((1, MLA_KV_RANK)),
                 _full_spec((MLA_Q_RANK, WIDE)), _full_spec((MLA_KV_RANK, WIDE)),
                 _full_spec((BRANCH_WIDTH, MLA_KV_RANK)), mem_spec, mem_spec]

    outs = [row_major(GATE_W), row_major(BRANCH_WIDTH),
            transposed(WIDE), row_major(WIDE), transposed(BRANCH_WIDTH),
            transposed(WIDE), row_major(WIDE), transposed(BRANCH_WIDTH)]
    out_specs = [spec for spec, _ in outs]
    out_shape = [shape for _, shape in outs]
    if entry_norm:
        out_specs.append(_row_spec(ROW_TILE, D_MODEL))
        out_shape.append(jax.ShapeDtypeStruct((t, D_MODEL), F32))
    return pl.pallas_call(
        functools.partial(_mixer_inputs_kernel, entry_norm=entry_norm, tiles_per_seq=nt),
        grid=(t // ROW_TILE,),
        in_specs=in_specs,
        out_specs=out_specs,
        out_shape=out_shape,
        scratch_shapes=[pltpu.VMEM((1, LANES), F32)],
        compiler_params=_params(1),
        name="mixer_inputs",
    )(*operands)


def _attn_kernel(qt_ref, k_ref, vt_ref, o_ref, sa_ref, sb_ref, ta_ref, tb_ref, m_ref, acc_ref,
                 *, granule_shift):
    tq = tk = sa_ref.shape[1]
    n_q_tiles = qt_ref.shape[2] // tq
    heads = range(N_GROUPS)
    buf_a = (sa_ref, ta_ref)
    buf_b = (sb_ref, tb_ref)
    ones_rows = jnp.ones((SUM_ROWS, tk), BF16)

    def start_of(tile_index, size):
        start = tile_index * size
        return start if isinstance(start, int) else pl.multiple_of(start, size)

    def scores(qi, j, buf, h, masked):
        q0 = start_of(qi, tq)
        ks = start_of(j, tk)
        grp = slice(h * LANES, (h + 1) * LANES)
        s = _dot(k_ref[0, pl.ds(ks, tk), grp], qt_ref[0, grp, pl.ds(q0, tq)])
        if masked:
            key = lax.broadcasted_iota(jnp.int32, (tk, tq), 0)
            qry = lax.broadcasted_iota(jnp.int32, (tk, tq), 1)
            s = jnp.where((key >> granule_shift) <= (qry >> granule_shift), s, NEG_INF)
        buf[0][h] = s
        buf[1][h] = jnp.max(s, axis=0, keepdims=True)

    def update(j, buf, h):
        ks = start_of(j, tk)
        m = m_ref[h]
        m_new = jnp.maximum(m, buf[1][h])
        alpha = jnp.exp2(m - m_new)
        e = jnp.exp2(buf[0][h] - m_new)
        m_ref[h] = m_new
        v_t = jnp.concatenate([vt_ref[0, h * HALF:(h + 1) * HALF, pl.ds(ks, tk)], ones_rows],
                              axis=0)
        acc_ref[h] = alpha * acc_ref[h] + _dot(v_t, e.astype(BF16))

    def tile_step(qi, j, cur, nxt=None, next_masked=False):
        if nxt is not None:
            for h in range(SCORE_LEAD):
                scores(qi, j + 1, nxt, h, next_masked)
        for h in heads:
            update(j, cur, h)
            if nxt is not None and h + SCORE_LEAD < N_GROUPS:
                scores(qi, j + 1, nxt, h + SCORE_LEAD, next_masked)

    def start_tile():
        m_ref[...] = jnp.full(m_ref.shape, NEG_INF, F32)
        acc_ref[...] = jnp.zeros(acc_ref.shape, F32)

    def finish_tile(qi, next_qi=None):
        if next_qi is not None:
            for h in heads:
                scores(next_qi, 0, buf_a, h, False)
        q0 = start_of(qi, tq)
        for pair in range(N_GROUPS // 2):
            halves = []
            for h in (2 * pair, 2 * pair + 1):
                halves.append(acc_ref[h, :HALF, :] * (1.0 / acc_ref[h, HALF:HALF + 1, :]))
            o_t = jnp.concatenate(halves, axis=0)
            o_ref[0, pl.ds(q0, tq), pair * LANES:(pair + 1) * LANES] = o_t.T.astype(BF16)

    start_tile()
    for h in heads:
        scores(0, 0, buf_a, h, True)
    tile_step(0, 0, buf_a)
    finish_tile(0, 1 if n_q_tiles > 1 else None)

    def q_tile(qi, _):
        start_tile()

        def two_tiles(i, _):
            tile_step(qi, 2 * i, buf_a, buf_b)
            tile_step(qi, 2 * i + 1, buf_b, buf_a)
            return 0

        n_pairs = (qi - 1) // 2
        lax.fori_loop(0, n_pairs, two_tiles, 0)
        j = 2 * n_pairs

        @pl.when(qi % 2 == 1)
        def _():
            tile_step(qi, j, buf_a, buf_b, next_masked=True)
            tile_step(qi, j + 1, buf_b)

        @pl.when(qi % 2 == 0)
        def _():
            tile_step(qi, j, buf_a, buf_b)
            tile_step(qi, j + 1, buf_b, buf_a, next_masked=True)
            tile_step(qi, j + 2, buf_a)

        @pl.when(qi < n_q_tiles - 1)
        def _():
            finish_tile(qi, qi + 1)

        @pl.when(qi == n_q_tiles - 1)
        def _():
            finish_tile(qi)

        return 0

    lax.fori_loop(1, n_q_tiles, q_tile, 0)


def _causal_attention(q_t, k_aug, v_t, batch, seq, granule):
    tile = ATTN_TILE
    k3 = k_aug.reshape(batch, seq, WIDE)
    out = pl.pallas_call(
        functools.partial(_attn_kernel, granule_shift=granule.bit_length() - 1),
        grid=(batch,),
        in_specs=[pl.BlockSpec((1, WIDE, seq), lambda b: (b, 0, 0)),
                  pl.BlockSpec((1, seq, WIDE), lambda b: (b, 0, 0)),
                  pl.BlockSpec((1, BRANCH_WIDTH, seq), lambda b: (b, 0, 0))],
        out_specs=pl.BlockSpec((1, seq, BRANCH_WIDTH), lambda b: (b, 0, 0)),
        out_shape=jax.ShapeDtypeStruct((batch, seq, BRANCH_WIDTH), BF16),
        scratch_shapes=[pltpu.VMEM((N_GROUPS, tile, tile), F32),
                        pltpu.VMEM((N_GROUPS, tile, tile), F32),
                        pltpu.VMEM((N_GROUPS, 1, tile), F32),
                        pltpu.VMEM((N_GROUPS, 1, tile), F32),
                        pltpu.VMEM((N_GROUPS, 1, tile), F32),
                        pltpu.VMEM((N_GROUPS, HALF + SUM_ROWS, tile), F32)],
        compiler_params=_params(1),
        name="causal_attention_g%d" % granule,
    )(q_t, k3, v_t)
    return out.reshape(batch * seq, BRANCH_WIDTH)


FF_CHUNK = 1024
ROW_PARTS = 2


def _merge_ffn_kernel(oa_ref, ob_ref, oc_ref, gates_ref, h_ref, wbr_ref, wout_ref, g1_ref, b1_ref,
                      w1_ref, w2_ref, g2_ref, b2_ref, o_ref, *, alpha):
    n_parts = ROW_PARTS
    part = o_ref.shape[0] // n_parts
    rows = [slice(p * part, (p + 1) * part) for p in range(n_parts)]

    def merge(r):
        merged = None
        for n, br_ref in enumerate((oa_ref, ob_ref, oc_ref)):
            gate = gates_ref[r, n * D_MODEL:(n + 1) * D_MODEL].astype(F32)
            term = gate * _dot(br_ref[r, :], wbr_ref[n])
            merged = term if merged is None else merged + term
        y = _dot(merged.astype(BF16), wout_ref[...])
        return _layer_norm(alpha * h_ref[r, :] + y, g1_ref[...], b1_ref[...])

    def mlp(h1):
        xb = h1.astype(BF16)
        acc = alpha * h1
        for c in range(0, D_FF, FF_CHUNK):
            u = jnp.maximum(_dot(xb, w1_ref[:, c:c + FF_CHUNK]), 0.0)
            acc = acc + _dot((u * u).astype(BF16), w2_ref[c:c + FF_CHUNK, :])
        return _layer_norm(acc, g2_ref[...], b2_ref[...])

    h1 = [merge(r) for r in rows]
    for r, h1_part in zip(rows, h1):
        o_ref[r, :] = mlp(h1_part)


def _merge_ffn(oa, ob, oc, gates, h, w_br, w_out, g1, b1, w1, w2, g2, b2, alpha):
    t = h.shape[0]
    br = _row_spec(ROW_TILE, BRANCH_WIDTH)
    vec = _full_spec((1, D_MODEL))
    return pl.pallas_call(
        functools.partial(_merge_ffn_kernel, alpha=alpha),
        grid=(t // ROW_TILE,),
        in_specs=[br, br, br, _row_spec(ROW_TILE, GATE_W), _row_spec(ROW_TILE, D_MODEL),
                  _full_spec((N_BRANCHES, BRANCH_WIDTH, D_MODEL)), _full_spec((D_MODEL, D_MODEL)),
                  vec, vec, _full_spec((D_MODEL, D_FF)), _full_spec((D_FF, D_MODEL)), vec, vec],
        out_specs=_row_spec(ROW_TILE, D_MODEL),
        out_shape=jax.ShapeDtypeStruct((t, D_MODEL), F32),
        compiler_params=_params(1),
        name="merge_ffn",
    )(oa, ob, oc, gates, h, w_br, w_out, g1.reshape(1, -1), b1.reshape(1, -1), w1, w2,
      g2.reshape(1, -1), b2.reshape(1, -1))


RELAYOUT_ROWS = 256
IN_FORGET = 3 * BRANCH_WIDTH
IN_CQ = IN_FORGET + FOX_HEADS
IN_ROPE = IN_CQ + MLA_Q_RANK + MLA_KV_RANK
IN_QMEM = IN_ROPE + MLA_ROPE_DIM


def _split_cast_kernel(w_ref, fox_ref, mla_ref, tail_ref):
    fox_ref[0] = w_ref[0, :, :IN_FORGET].astype(BF16)
    mla_ref[0] = w_ref[0, :, IN_CQ:IN_ROPE].astype(BF16)
    tail_ref[0] = w_ref[0, :, IN_QMEM:].astype(BF16)


def _split_cast(w_in):
    depth, d_model, d_in = w_in.shape
    widths = (IN_FORGET, IN_ROPE - IN_CQ, d_in - IN_QMEM)

    def spec(width):
        return pl.BlockSpec((1, RELAYOUT_ROWS, width), lambda l, r: (l, r, 0))

    return pl.pallas_call(
        _split_cast_kernel,
        grid=(depth, d_model // RELAYOUT_ROWS),
        in_specs=[spec(d_in)],
        out_specs=[spec(w) for w in widths],
        out_shape=[jax.ShapeDtypeStruct((depth, d_model, w), BF16) for w in widths],
        compiler_params=_params(2),
        name="split_cast",
    )(w_in)


def _layer_weights(w_in, b_forget, w_uq, w_ukv):
    depth = w_in.shape[0]
    tail = LANES - FORGET_LANE - N_SPLIT * FOX_HEADS
    f_log, k_rope = w_in[..., IN_FORGET:IN_CQ], w_in[..., IN_ROPE:IN_QMEM]
    w_main = _split_cast(w_in)
    w_small = jnp.pad(jnp.concatenate([k_rope] + [f_log] * N_SPLIT, axis=-1).astype(BF16),
                      ((0, 0), (0, 0), (MLA_NOPE_DIM, tail)))
    b3 = jnp.pad(jnp.concatenate([b_forget] * N_SPLIT, axis=-1),
                 ((0, 0), (FORGET_LANE, tail))).reshape(depth, 1, LANES)

    dq = MLA_NOPE_DIM + MLA_ROPE_DIM
    wq = w_uq.astype(BF16).reshape(depth, MLA_Q_RANK, MLA_HEADS, dq)
    wq = jnp.pad(wq, ((0, 0), (0, 0), (0, 0), (0, LANES - dq))).reshape(depth, MLA_Q_RANK, -1)
    wkv = w_ukv.astype(BF16).reshape(depth, MLA_KV_RANK, MLA_HEADS, MLA_NOPE_DIM + MLA_V_DIM)
    wk = jnp.pad(wkv[..., :MLA_NOPE_DIM], ((0, 0), (0, 0), (0, 0), (0, LANES - MLA_NOPE_DIM)))
    wk = wk.reshape(depth, MLA_KV_RANK, -1)
    wv_t = jnp.swapaxes(wkv[..., MLA_NOPE_DIM:].reshape(depth, MLA_KV_RANK, -1), 1, 2)
    return w_main, w_small, b3, wq, wk, wv_t


def kernel(x, mem, positions, ln_in_g, ln_in_b, w_in, b_forget, w_uq, g_cq, w_ukv, g_ckv,
           w_mem_kv, w_br, w_out, ln1_g, ln1_b, w_ff1, w_ff2, ln2_g, ln2_b):
    batch, seq, _ = x.shape
    depth = w_in.shape[0]
    assert seq % ATTN_TILE == 0 and seq % ROW_TILE == 0
    alpha = float((2 * depth) ** 0.25)

    w_main, w_small, b3, wq, wk, wv_t = _layer_weights(w_in, b_forget, w_uq, w_ukv)
    w_br_b = w_br.astype(BF16)
    w_out_b = w_out.astype(BF16)
    w_ff1_b = w_ff1.astype(BF16)
    w_ff2_b = w_ff2.astype(BF16)

    rope_tabs = _rope_tables(positions)
    mk, mv = _mem_kv(mem, w_mem_kv)
    h = x.reshape(batch * seq, D_MODEL)
    for l in range(depth):
        outs = _mixer_inputs(h, (ln_in_g, ln_in_b) if l == 0 else None,
                             tuple(w[l] for w in w_main), w_small[l], b3[l],
                             rope_tabs, g_cq[l], g_ckv[l], wq[l], wk[l], wv_t[l], mk, mv, l,
                             batch, seq)
        gates, o_c, fq_t, fk_aug, fv_t, mq_t, mk_aug, mv_t = outs[:8]
        if l == 0:
            h = outs[8]
        o_a = _causal_attention(fq_t, fk_aug, fv_t, batch, seq, 1)
        o_b = _causal_attention(mq_t, mk_aug, mv_t, batch, seq, CHUNK)
        h = _merge_ffn(o_a, o_b, o_c, gates, h, w_br_b[l], w_out_b[l], ln1_g[l], ln1_b[l],
                       w_ff1_b[l], w_ff2_b[l], ln2_g[l], ln2_b[l], alpha)
    return h.reshape(batch, seq, D_MODEL)
```

```python
import functools
import math

import numpy as np
import jax
import jax.numpy as jnp
from jax import lax
from jax.experimental import pallas as pl
from jax.experimental.pallas import tpu as pltpu

D_MODEL = 1024
CHUNK = 64
FOX_HEADS = 8
FOX_HEAD_DIM = 64
MLA_HEADS = 8
MLA_NOPE_DIM = 64
MLA_ROPE_DIM = 32
MLA_V_DIM = 64
MLA_Q_RANK = 384
MLA_KV_RANK = 256
ROPE_BASE = 10000.0
MEM_HEADS = 4
MEM_HEAD_DIM = 128
N_BRANCHES = 3
BRANCH_WIDTH = 512
D_FF = 4 * D_MODEL
LN_EPS = 1e-5
RMS_EPS = 1e-6
NEG_INF = -1e30
LOG2E = math.log2(math.e)

LANES = 128
HALF = 64
N_GROUPS = 8
WIDE = N_GROUPS * LANES
MXU_WIDTH = 256
VMEM_LIMIT = 56 * 1024 * 1024

ROW_TILE = 512
ATTN_TILE = 256
SCORE_LEAD = 3
SUM_ROWS = 16

F32 = jnp.float32
BF16 = jnp.bfloat16


def _params(n_axes):
    return pltpu.CompilerParams(
        dimension_semantics=("arbitrary",) * n_axes, vmem_limit_bytes=VMEM_LIMIT)


def _dot(a, b):
    return jnp.dot(a, b, preferred_element_type=F32)


def _dot_nt(a, b):
    return lax.dot_general(a, b, (((1,), (1,)), ((), ())), preferred_element_type=F32)


def _layer_norm(z, g, b):
    mu = jnp.mean(z, axis=-1, keepdims=True)
    d = z - mu
    var = jnp.mean(d * d, axis=-1, keepdims=True)
    return d * lax.rsqrt(var + LN_EPS) * g + b


def _rms_norm(z, g):
    return z * lax.rsqrt(jnp.mean(z * z, axis=-1, keepdims=True) + RMS_EPS) * g


def _row_spec(tile, width, col_block=0):
    return pl.BlockSpec((tile, width), lambda i: (i, col_block))


def _full_spec(shape):
    return pl.BlockSpec(shape, lambda *_: (0,) * len(shape), pipeline_mode=pl.Buffered(1))


ROPE_HALF = MLA_ROPE_DIM // 2


def _rope_table_kernel(pos_ref, invf_ref, c_ref, slo_ref, shi_ref):
    ang = pos_ref[...].astype(F32) * invf_ref[...]
    lane = lax.broadcasted_iota(jnp.int32, ang.shape, 1)
    lo = (lane >= MLA_NOPE_DIM) & (lane < MLA_NOPE_DIM + ROPE_HALF)
    hi = (lane >= MLA_NOPE_DIM + ROPE_HALF) & (lane < MLA_NOPE_DIM + MLA_ROPE_DIM)
    sin = jnp.sin(ang)
    c_ref[...] = jnp.where(lane < MLA_NOPE_DIM, 1.0, jnp.where(lo | hi, jnp.cos(ang), 0.0))
    slo_ref[...] = jnp.where(lo, -sin, 0.0)
    shi_ref[...] = jnp.where(hi, sin, 0.0)


def _rotary(x, ct, s_lo, s_hi):
    width = x.shape[1]
    return (x * ct + pltpu.roll(x, width - ROPE_HALF, 1) * s_lo
            + pltpu.roll(x, ROPE_HALF, 1) * s_hi)


def _rope_tables(positions):
    t = positions.size
    inv_freq = ROPE_BASE ** (-jnp.arange(0, MLA_ROPE_DIM, 2, dtype=F32) / MLA_ROPE_DIM)
    invf = jnp.zeros((1, LANES), F32).at[0, MLA_NOPE_DIM:MLA_NOPE_DIM + MLA_ROPE_DIM].set(
        jnp.concatenate([inv_freq, inv_freq]))
    return pl.pallas_call(
        _rope_table_kernel,
        grid=(t // ROW_TILE,),
        in_specs=[_row_spec(ROW_TILE, 1), _full_spec((1, LANES))],
        out_specs=[_row_spec(ROW_TILE, LANES)] * 3,
        out_shape=[jax.ShapeDtypeStruct((t, LANES), F32)] * 3,
        compiler_params=_params(1),
        name="rope_tables",
    )(positions.reshape(t, 1), invf)


MEM_BATCH_GROUP = 4


def _mem_kv_kernel(mem_ref, w_ref, k_ref, v_ref):
    group, n_mem, _ = mem_ref.shape
    rows = mem_ref[...].reshape(group * n_mem, D_MODEL).astype(BF16)
    kv = _dot(rows, w_ref[0].astype(BF16))
    k = (kv[:, :BRANCH_WIDTH] * (MEM_HEAD_DIM ** -0.5 * LOG2E)).astype(BF16)
    k_ref[0] = k.reshape(group, n_mem, BRANCH_WIDTH)
    v_ref[0] = kv[:, BRANCH_WIDTH:].astype(BF16).reshape(group, n_mem, BRANCH_WIDTH)


def _mem_kv(mem, w_mem_kv):
    depth = w_mem_kv.shape[0]
    batch, n_mem, _ = mem.shape
    group = math.gcd(batch, MEM_BATCH_GROUP)
    return pl.pallas_call(
        _mem_kv_kernel,
        grid=(batch // group, depth),
        in_specs=[pl.BlockSpec((group, n_mem, D_MODEL), lambda b, l: (b, 0, 0)),
                  pl.BlockSpec((1, D_MODEL, 2 * BRANCH_WIDTH), lambda b, l: (l, 0, 0))],
        out_specs=[pl.BlockSpec((1, group, n_mem, BRANCH_WIDTH), lambda b, l: (l, b, 0, 0))] * 2,
        out_shape=[jax.ShapeDtypeStruct((depth, batch, n_mem, BRANCH_WIDTH), BF16)] * 2,
        compiler_params=_params(2),
        name="mem_kv",
    )(mem, w_mem_kv)


GATE_W = N_BRANCHES * D_MODEL
COL_FOX_Q = 0
COL_FOX_K = COL_FOX_Q + BRANCH_WIDTH
COL_FOX_V = COL_FOX_K + BRANCH_WIDTH
COL_CQ = COL_FOX_V + BRANCH_WIDTH
COL_CKV = COL_CQ + MLA_Q_RANK
COL_QMEM = COL_CKV + MLA_KV_RANK
COL_GATE = COL_QMEM + BRANCH_WIDTH
MAIN_W = COL_GATE + GATE_W
GATE_CHUNK = 256
N_SPLIT = 3
FORGET_LANE = MLA_NOPE_DIM + MLA_ROPE_DIM


def _split3(x):
    hi = x.astype(BF16)
    r = x - hi.astype(F32)
    mid = r.astype(BF16)
    lo = (r - mid.astype(F32)).astype(BF16)
    return hi, mid, lo


def _mixer_inputs_kernel(*refs, entry_norm, tiles_per_seq):
    if entry_norm:
        x_ref, lng_ref, lnb_ref, *refs = refs
    else:
        x_ref, *refs = refs
    (wfox_ref, wmla_ref, wtail_ref, ws_ref, bf_ref, pq_ref, pk_ref, oq_ref, ok_ref,
     ct_ref, slo_ref, shi_ref, gq_ref, gkv_ref, wq_ref, wk_ref, wvt_ref, mk_ref, mv_ref,
     gates_ref, oc_ref, fqt_ref, fka_ref, fvt_ref, mqt_ref, mka_ref, mvt_ref, *rest) = refs
    carry_ref = rest[-1]
    tm = x_ref.shape[0]

    @pl.when(pl.program_id(0) % tiles_per_seq == 0)
    def _():
        carry_ref[...] = jnp.zeros_like(carry_ref)

    h = x_ref[...]
    if entry_norm:
        h = _layer_norm(h, lng_ref[...], lnb_ref[...])
        rest[0][...] = h
    xb = h.astype(BF16)

    def mm(c0, n):
        for w_ref, first in ((wtail_ref, COL_QMEM), (wmla_ref, COL_CQ), (wfox_ref, COL_FOX_Q)):
            if c0 >= first:
                return _dot(xb, w_ref[:, c0 - first:c0 - first + n])

    gate_cols = iter(range(0, GATE_W, GATE_CHUNK))

    def emit_gates(n_chunks=2):
        for _ in range(n_chunks):
            c = next(gate_cols, None)
            if c is not None:
                g = mm(COL_GATE + c, GATE_CHUNK)
                gates_ref[:, c:c + GATE_CHUNK] = (0.5 * jnp.tanh(0.5 * g) + 0.5).astype(BF16)

    small = _dot(xb, ws_ref[...])
    xq = _rms_norm(mm(COL_CQ, MLA_Q_RANK), gq_ref[...]).astype(BF16)
    xkv = _rms_norm(mm(COL_CKV, MLA_KV_RANK), gkv_ref[...]).astype(BF16)

    x = small + bf_ref[...]
    logf = jnp.minimum(x, 0.0) - jnp.log1p(jnp.exp(-jnp.abs(x)))
    row = lax.broadcasted_iota(jnp.int32, (tm, tm), 0)
    col = lax.broadcasted_iota(jnp.int32, (tm, tm), 1)
    tri = jnp.where(col <= row, 1.0, 0.0).astype(BF16)
    hi, mid, lo = _split3(logf)
    cum = _dot(tri, hi) + _dot(tri, mid) + _dot(tri, lo) + carry_ref[...]
    carry_ref[...] = cum[tm - 1:tm, :]
    emit_gates()

    lane = lax.broadcasted_iota(jnp.int32, (tm, LANES), 1)
    span = lane - FORGET_LANE
    chi, cmid, clo = _split3(cum * LOG2E)
    pieces = jnp.where((span >= 0) & (span < FOX_HEADS), chi,
                       jnp.where((span >= FOX_HEADS) & (span < 2 * FOX_HEADS), cmid,
                                 jnp.where((span >= 2 * FOX_HEADS) & (span < 3 * FOX_HEADS), clo,
                                           jnp.zeros_like(chi))))
    left = lane < HALF
    heads_per_dot = MXU_WIDTH // HALF
    for h0 in range(0, FOX_HEADS, heads_per_dot):
        c0 = h0 * HALF
        qp2 = mm(COL_FOX_Q + c0, MXU_WIDTH) * (FOX_HEAD_DIM ** -0.5 * LOG2E)
        kp2 = mm(COL_FOX_K + c0, MXU_WIDTH)
        grp4 = slice(h0 * LANES, (h0 + heads_per_dot) * LANES)
        q_spare = _dot(pieces, pq_ref[:, grp4]) + oq_ref[:, grp4]
        k_spare = _dot(pieces, pk_ref[:, grp4]) + ok_ref[:, grp4]
        for i in range(heads_per_dot):
            head = h0 + i
            pair = slice((i // 2) * LANES, (i // 2 + 1) * LANES)
            loc = slice(i * LANES, (i + 1) * LANES)
            grp = slice(head * LANES, (head + 1) * LANES)
            own = left if head % 2 == 0 else jnp.logical_not(left)
            fqt_ref[0, grp, :] = jnp.where(own, qp2[:, pair], q_spare[:, loc]).T.astype(BF16)
            fka_ref[:, grp] = jnp.where(own, kp2[:, pair], k_spare[:, loc]).astype(BF16)
        emit_gates()
    fvt_ref[0] = mm(COL_FOX_V, BRANCH_WIDTH).T.astype(BF16)

    qm = mm(COL_QMEM, BRANCH_WIDTH).astype(BF16)
    for hm in range(MEM_HEADS):
        grp = slice(hm * MEM_HEAD_DIM, (hm + 1) * MEM_HEAD_DIM)
        s = _dot_nt(qm[:, grp], mk_ref[0, 0, :, grp])
        e = jnp.exp2(s - jnp.max(s, axis=1, keepdims=True))
        o = _dot(e.astype(BF16), mv_ref[0, 0, :, grp])
        oc_ref[:, grp] = (o * (1.0 / jnp.sum(e, axis=1, keepdims=True))).astype(BF16)
    emit_gates()

    ct, s_lo, s_hi = ct_ref[...], slo_ref[...], shi_ref[...]
    groups_per_dot = MXU_WIDTH // LANES
    ct2, s_lo2, s_hi2 = (jnp.concatenate([tab] * groups_per_dot, axis=1) for tab in (ct, s_lo, s_hi))
    scale = (MLA_NOPE_DIM + MLA_ROPE_DIM) ** -0.5 * LOG2E
    kpe = _rotary(small, ct, s_lo, s_hi)
    kpe2 = jnp.concatenate([kpe] * groups_per_dot, axis=1)
    for n, c in enumerate(range(0, WIDE, MXU_WIDTH)):
        q = _rotary(_dot(xq, wq_ref[:, c:c + MXU_WIDTH]), ct2, s_lo2, s_hi2) * scale
        mqt_ref[0, c:c + MXU_WIDTH, :] = q.T.astype(BF16)
        mka_ref[:, c:c + MXU_WIDTH] = (_dot(xkv, wk_ref[:, c:c + MXU_WIDTH]) + kpe2).astype(BF16)
        if n == 1:
            emit_gates(3)
    mvt_ref[0] = _dot_nt(wvt_ref[...], xkv).astype(BF16)
    emit_gates(GATE_W // GATE_CHUNK)


def _fox_placement():
    pq = np.zeros((LANES, WIDE), np.float32)
    pk = np.zeros((LANES, WIDE), np.float32)
    oq = np.zeros((1, WIDE), np.float32)
    ok = np.zeros((1, WIDE), np.float32)
    for h in range(FOX_HEADS):
        base = h * LANES + (HALF if h % 2 == 0 else 0)
        for j in range(N_SPLIT):
            src = FORGET_LANE + j * FOX_HEADS + h
            pq[src, base + j] = 1.0
            ok[0, base + j] = 1.0
            oq[0, base + N_SPLIT + j] = 1.0
            pk[src, base + N_SPLIT + j] = -1.0
    return (jnp.asarray(pq, BF16), jnp.asarray(pk, BF16), jnp.asarray(oq), jnp.asarray(ok))


def _mixer_inputs(h, ln_in, w_main, w_small, b_forget3, rope_tabs, g_cq, g_ckv, wq, wk, wvt,
                  mk, mv, layer, batch, seq):
    t = h.shape[0]
    nt = seq // ROW_TILE
    n_mem = mk.shape[2]
    entry_norm = ln_in is not None
    pq, pk, oq, ok = _fox_placement()
    mem_spec = pl.BlockSpec((1, 1, n_mem, BRANCH_WIDTH), lambda i: (layer, i // nt, 0, 0))

    def transposed(rows):
        return (pl.BlockSpec((1, rows, ROW_TILE), lambda i: (i // nt, 0, i % nt)),
                jax.ShapeDtypeStruct((batch, rows, seq), BF16))

    def row_major(width):
        return _row_spec(ROW_TILE, width), jax.ShapeDtypeStruct((t, width), BF16)

    operands = [h]
    in_specs = [_row_spec(ROW_TILE, D_MODEL)]
    if entry_norm:
        operands += [ln_in[0].reshape(1, -1), ln_in[1].reshape(1, -1)]
        in_specs += [_full_spec((1, D_MODEL))] * 2
    operands += [*w_main, w_small, b_forget3, pq, pk, oq, ok, *rope_tabs,
                 g_cq.reshape(1, -1), g_ckv.reshape(1, -1), wq, wk, wvt, mk, mv]
    in_specs += [_full_spec((D_MODEL, COL_CQ - COL_FOX_Q)), _full_spec((D_MODEL, COL_QMEM - COL_CQ)),
                 _full_spec((D_MODEL, MAIN_W - COL_QMEM)), _full_spec((D_MODEL, LANES)),
                 _full_spec((1, LANES)),
                 _full_spec((LANES, WIDE)), _full_spec((LANES, WIDE)), _full_spec((1, WIDE)),
                 _full_spec((1, WIDE))] + [_row_spec(ROW_TILE, LANES)] * 3 + [
                 _full_spec((1, MLA_Q_RANK)), _full_spec((1, MLA_KV_RANK)),
                 _full_spec((MLA_Q_RANK, WIDE)), _full_spec((MLA_KV_RANK, WIDE)),
                 _full_spec((BRANCH_WIDTH, MLA_KV_RANK)), mem_spec, mem_spec]

    outs = [row_major(GATE_W), row_major(BRANCH_WIDTH),
            transposed(WIDE), row_major(WIDE), transposed(BRANCH_WIDTH),
            transposed(WIDE), row_major(WIDE), transposed(BRANCH_WIDTH)]
    out_specs = [spec for spec, _ in outs]
    out_shape = [shape for _, shape in outs]
    if entry_norm:
        out_specs.append(_row_spec(ROW_TILE, D_MODEL))
        out_shape.append(jax.ShapeDtypeStruct((t, D_MODEL), F32))
    return pl.pallas_call(
        functools.partial(_mixer_inputs_kernel, entry_norm=entry_norm, tiles_per_seq=nt),
        grid=(t // ROW_TILE,),
        in_specs=in_specs,
        out_specs=out_specs,
        out_shape=out_shape,
        scratch_shapes=[pltpu.VMEM((1, LANES), F32)],
        compiler_params=_params(1),
        name="mixer_inputs",
    )(*operands)


def _attn_kernel(qt_ref, k_ref, vt_ref, o_ref, sa_ref, sb_ref, ta_ref, tb_ref, m_ref, acc_ref,
                 *, granule_shift):
    tq = tk = sa_ref.shape[1]
    n_q_tiles = qt_ref.shape[2] // tq
    heads = range(N_GROUPS)
    buf_a = (sa_ref, ta_ref)
    buf_b = (sb_ref, tb_ref)
    ones_rows = jnp.ones((SUM_ROWS, tk), BF16)

    def start_of(tile_index, size):
        start = tile_index * size
        return start if isinstance(start, int) else pl.multiple_of(start, size)

    def scores(qi, j, buf, h, masked):
        q0 = start_of(qi, tq)
        ks = start_of(j, tk)
        grp = slice(h * LANES, (h + 1) * LANES)
        s = _dot(k_ref[0, pl.ds(ks, tk), grp], qt_ref[0, grp, pl.ds(q0, tq)])
        if masked:
            key = lax.broadcasted_iota(jnp.int32, (tk, tq), 0)
            qry = lax.broadcasted_iota(jnp.int32, (tk, tq), 1)
            s = jnp.where((key >> granule_shift) <= (qry >> granule_shift), s, NEG_INF)
        buf[0][h] = s
        buf[1][h] = jnp.max(s, axis=0, keepdims=True)

    def update(j, buf, h):
        ks = start_of(j, tk)
        m = m_ref[h]
        m_new = jnp.maximum(m, buf[1][h])
        alpha = jnp.exp2(m - m_new)
        e = jnp.exp2(buf[0][h] - m_new)
        m_ref[h] = m_new
        v_t = jnp.concatenate([vt_ref[0, h * HALF:(h + 1) * HALF, pl.ds(ks, tk)], ones_rows],
                              axis=0)
        acc_ref[h] = alpha * acc_ref[h] + _dot(v_t, e.astype(BF16))

    def tile_step(qi, j, cur, nxt=None, next_masked=False):
        if nxt is not None:
            for h in range(SCORE_LEAD):
                scores(qi, j + 1, nxt, h, next_masked)
        for h in heads:
            update(j, cur, h)
            if nxt is not None and h + SCORE_LEAD < N_GROUPS:
                scores(qi, j + 1, nxt, h + SCORE_LEAD, next_masked)

    def start_tile():
        m_ref[...] = jnp.full(m_ref.shape, NEG_INF, F32)
        acc_ref[...] = jnp.zeros(acc_ref.shape, F32)

    def finish_tile(qi, next_qi=None):
        if next_qi is not None:
            for h in heads:
                scores(next_qi, 0, buf_a, h, False)
        q0 = start_of(qi, tq)
        for pair in range(N_GROUPS // 2):
            halves = []
            for h in (2 * pair, 2 * pair + 1):
                halves.append(acc_ref[h, :HALF, :] * (1.0 / acc_ref[h, HALF:HALF + 1, :]))
            o_t = jnp.concatenate(halves, axis=0)
            o_ref[0, pl.ds(q0, tq), pair * LANES:(pair + 1) * LANES] = o_t.T.astype(BF16)

    start_tile()
    for h in heads:
        scores(0, 0, buf_a, h, True)
    tile_step(0, 0, buf_a)
    finish_tile(0, 1 if n_q_tiles > 1 else None)

    def q_tile(qi, _):
        start_tile()

        def two_tiles(i, _):
            tile_step(qi, 2 * i, buf_a, buf_b)
            tile_step(qi, 2 * i + 1, buf_b, buf_a)
            return 0

        n_pairs = (qi - 1) // 2
        lax.fori_loop(0, n_pairs, two_tiles, 0)
        j = 2 * n_pairs

        @pl.when(qi % 2 == 1)
        def _():
            tile_step(qi, j, buf_a, buf_b, next_masked=True)
            tile_step(qi, j + 1, buf_b)

        @pl.when(qi % 2 == 0)
        def _():
            tile_step(qi, j, buf_a, buf_b)
            tile_step(qi, j + 1, buf_b, buf_a, next_masked=True)
            tile_step(qi, j + 2, buf_a)

        @pl.when(qi < n_q_tiles - 1)
        def _():
            finish_tile(qi, qi + 1)

        @pl.when(qi == n_q_tiles - 1)
        def _():
            finish_tile(qi)

        return 0

    lax.fori_loop(1, n_q_tiles, q_tile, 0)


def _causal_attention(q_t, k_aug, v_t, batch, seq, granule):
    tile = ATTN_TILE
    k3 = k_aug.reshape(batch, seq, WIDE)
    out = pl.pallas_call(
        functools.partial(_attn_kernel, granule_shift=granule.bit_length() - 1),
        grid=(batch,),
        in_specs=[pl.BlockSpec((1, WIDE, seq), lambda b: (b, 0, 0)),
                  pl.BlockSpec((1, seq, WIDE), lambda b: (b, 0, 0)),
                  pl.BlockSpec((1, BRANCH_WIDTH, seq), lambda b: (b, 0, 0))],
        out_specs=pl.BlockSpec((1, seq, BRANCH_WIDTH), lambda b: (b, 0, 0)),
        out_shape=jax.ShapeDtypeStruct((batch, seq, BRANCH_WIDTH), BF16),
        scratch_shapes=[pltpu.VMEM((N_GROUPS, tile, tile), F32),
                        pltpu.VMEM((N_GROUPS, tile, tile), F32),
                        pltpu.VMEM((N_GROUPS, 1, tile), F32),
                        pltpu.VMEM((N_GROUPS, 1, tile), F32),
                        pltpu.VMEM((N_GROUPS, 1, tile), F32),
                        pltpu.VMEM((N_GROUPS, HALF + SUM_ROWS, tile), F32)],
        compiler_params=_params(1),
        name="causal_attention_g%d" % granule,
    )(q_t, k3, v_t)
    return out.reshape(batch * seq, BRANCH_WIDTH)


FF_CHUNK = 1024
ROW_PARTS = 2


def _merge_ffn_kernel(oa_ref, ob_ref, oc_ref, gates_ref, h_ref, wbr_ref, wout_ref, g1_ref, b1_ref,
                      w1_ref, w2_ref, g2_ref, b2_ref, o_ref, *, alpha):
    n_parts = ROW_PARTS
    part = o_ref.shape[0] // n_parts
    rows = [slice(p * part, (p + 1) * part) for p in range(n_parts)]

    def merge(r):
        merged = None
        for n, br_ref in enumerate((oa_ref, ob_ref, oc_ref)):
            gate = gates_ref[r, n * D_MODEL:(n + 1) * D_MODEL].astype(F32)
            term = gate * _dot(br_ref[r, :], wbr_ref[n])
            merged = term if merged is None else merged + term
        y = _dot(merged.astype(BF16), wout_ref[...])
        return _layer_norm(alpha * h_ref[r, :] + y, g1_ref[...], b1_ref[...])

    def mlp(h1):
        xb = h1.astype(BF16)
        acc = alpha * h1
        for c in range(0, D_FF, FF_CHUNK):
            u = jnp.maximum(_dot(xb, w1_ref[:, c:c + FF_CHUNK]), 0.0)
            acc = acc + _dot((u * u).astype(BF16), w2_ref[c:c + FF_CHUNK, :])
        return _layer_norm(acc, g2_ref[...], b2_ref[...])

    h1 = [merge(r) for r in rows]
    for r, h1_part in zip(rows, h1):
        o_ref[r, :] = mlp(h1_part)


def _merge_ffn(oa, ob, oc, gates, h, w_br, w_out, g1, b1, w1, w2, g2, b2, alpha):
    t = h.shape[0]
    br = _row_spec(ROW_TILE, BRANCH_WIDTH)
    vec = _full_spec((1, D_MODEL))
    return pl.pallas_call(
        functools.partial(_merge_ffn_kernel, alpha=alpha),
        grid=(t // ROW_TILE,),
        in_specs=[br, br, br, _row_spec(ROW_TILE, GATE_W), _row_spec(ROW_TILE, D_MODEL),
                  _full_spec((N_BRANCHES, BRANCH_WIDTH, D_MODEL)), _full_spec((D_MODEL, D_MODEL)),
                  vec, vec, _full_spec((D_MODEL, D_FF)), _full_spec((D_FF, D_MODEL)), vec, vec],
        out_specs=_row_spec(ROW_TILE, D_MODEL),
        out_shape=jax.ShapeDtypeStruct((t, D_MODEL), F32),
        compiler_params=_params(1),
        name="merge_ffn",
    )(oa, ob, oc, gates, h, w_br, w_out, g1.reshape(1, -1), b1.reshape(1, -1), w1, w2,
      g2.reshape(1, -1), b2.reshape(1, -1))


def _layer_weights(w_in, b_forget, w_uq, w_ukv):
    depth = w_in.shape[0]
    fox_w = 3 * BRANCH_WIDTH
    tail = LANES - FORGET_LANE - N_SPLIT * FOX_HEADS
    f0 = fox_w
    f1 = f0 + FOX_HEADS
    r0 = f1 + MLA_Q_RANK + MLA_KV_RANK
    r1 = r0 + MLA_ROPE_DIM
    f_log, k_rope = w_in[..., f0:f1], w_in[..., r0:r1]
    w_main = (w_in[..., :f0].astype(BF16), w_in[..., f1:r0].astype(BF16), w_in[..., r1:].astype(BF16))
    w_small = jnp.pad(jnp.concatenate([k_rope] + [f_log] * N_SPLIT, axis=-1).astype(BF16),
                      ((0, 0), (0, 0), (MLA_NOPE_DIM, tail)))
    b3 = jnp.pad(jnp.concatenate([b_forget] * N_SPLIT, axis=-1),
                 ((0, 0), (FORGET_LANE, tail))).reshape(depth, 1, LANES)

    dq = MLA_NOPE_DIM + MLA_ROPE_DIM
    wq = w_uq.astype(BF16).reshape(depth, MLA_Q_RANK, MLA_HEADS, dq)
    wq = jnp.pad(wq, ((0, 0), (0, 0), (0, 0), (0, LANES - dq))).reshape(depth, MLA_Q_RANK, -1)
    wkv = w_ukv.astype(BF16).reshape(depth, MLA_KV_RANK, MLA_HEADS, MLA_NOPE_DIM + MLA_V_DIM)
    wk = jnp.pad(wkv[..., :MLA_NOPE_DIM], ((0, 0), (0, 0), (0, 0), (0, LANES - MLA_NOPE_DIM)))
    wk = wk.reshape(depth, MLA_KV_RANK, -1)
    wv_t = jnp.swapaxes(wkv[..., MLA_NOPE_DIM:].reshape(depth, MLA_KV_RANK, -1), 1, 2)
    return w_main, w_small, b3, wq, wk, wv_t


def kernel(x, mem, positions, ln_in_g, ln_in_b, w_in, b_forget, w_uq, g_cq, w_ukv, g_ckv,
           w_mem_kv, w_br, w_out, ln1_g, ln1_b, w_ff1, w_ff2, ln2_g, ln2_b):
    batch, seq, _ = x.shape
    depth = w_in.shape[0]
    assert seq % ATTN_TILE == 0 and seq % ROW_TILE == 0
    alpha = float((2 * depth) ** 0.25)

    w_main, w_small, b3, wq, wk, wv_t = _layer_weights(w_in, b_forget, w_uq, w_ukv)
    w_br_b = w_br.astype(BF16)
    w_out_b = w_out.astype(BF16)
    w_ff1_b = w_ff1.astype(BF16)
    w_ff2_b = w_ff2.astype(BF16)

    rope_tabs = _rope_tables(positions)
    mk, mv = _mem_kv(mem, w_mem_kv)
    h = x.reshape(batch * seq, D_MODEL)
    for l in range(depth):
        outs = _mixer_inputs(h, (ln_in_g, ln_in_b) if l == 0 else None,
                             tuple(w[l] for w in w_main), w_small[l], b3[l],
                             rope_tabs, g_cq[l], g_ckv[l], wq[l], wk[l], wv_t[l], mk, mv, l,
                             batch, seq)
        gates, o_c, fq_t, fk_aug, fv_t, mq_t, mk_aug, mv_t = outs[:8]
        if l == 0:
            h = outs[8]
        o_a = _causal_attention(fq_t, fk_aug, fv_t, batch, seq, 1)
        o_b = _causal_attention(mq_t, mk_aug, mv_t, batch, seq, CHUNK)
        h = _merge_ffn(o_a, o_b, o_c, gates, h, w_br_b[l], w_out_b[l], ln1_g[l], ln1_b[l],
                       w_ff1_b[l], w_ff2_b[l], ln2_g[l], ln2_b[l], alpha)
    return h.reshape(batch, seq, D_MODEL)
```

```python
import functools
import math

import numpy as np
import jax
import jax.numpy as jnp
from jax import lax
from jax.experimental import pallas as pl
from jax.experimental.pallas import tpu as pltpu

D_MODEL = 1024
CHUNK = 64
FOX_HEADS = 8
FOX_HEAD_DIM = 64
MLA_HEADS = 8
MLA_NOPE_DIM = 64
MLA_ROPE_DIM = 32
MLA_V_DIM = 64
MLA_Q_RANK = 384
MLA_KV_RANK = 256
ROPE_BASE = 10000.0
MEM_HEADS = 4
MEM_HEAD_DIM = 128
N_BRANCHES = 3
BRANCH_WIDTH = 512
D_FF = 4 * D_MODEL
LN_EPS = 1e-5
RMS_EPS = 1e-6
NEG_INF = -1e30
LOG2E = math.log2(math.e)

LANES = 128
HALF = 64
N_GROUPS = 8
WIDE = N_GROUPS * LANES
MXU_WIDTH = 256
VMEM_LIMIT = 56 * 1024 * 1024

ROW_TILE = 512
ATTN_TILE = 512
SCORE_LEAD = 3
SUM_ROWS = 16

F32 = jnp.float32
BF16 = jnp.bfloat16


def _params(n_axes):
    return pltpu.CompilerParams(
        dimension_semantics=("arbitrary",) * n_axes, vmem_limit_bytes=VMEM_LIMIT)


def _dot(a, b):
    return jnp.dot(a, b, preferred_element_type=F32)


def _dot_nt(a, b):
    return lax.dot_general(a, b, (((1,), (1,)), ((), ())), preferred_element_type=F32)


def _layer_norm(z, g, b):
    mu = jnp.mean(z, axis=-1, keepdims=True)
    d = z - mu
    var = jnp.mean(d * d, axis=-1, keepdims=True)
    return d * lax.rsqrt(var + LN_EPS) * g + b


def _rms_norm(z, g):
    return z * lax.rsqrt(jnp.mean(z * z, axis=-1, keepdims=True) + RMS_EPS) * g


def _row_spec(tile, width, col_block=0):
    return pl.BlockSpec((tile, width), lambda i: (i, col_block))


def _full_spec(shape):
    return pl.BlockSpec(shape, lambda *_: (0,) * len(shape), pipeline_mode=pl.Buffered(1))


ROPE_HALF = MLA_ROPE_DIM // 2


def _rope_table_kernel(pos_ref, invf_ref, c_ref, slo_ref, shi_ref):
    ang = pos_ref[...].astype(F32) * invf_ref[...]
    lane = lax.broadcasted_iota(jnp.int32, ang.shape, 1)
    lo = (lane >= MLA_NOPE_DIM) & (lane < MLA_NOPE_DIM + ROPE_HALF)
    hi = (lane >= MLA_NOPE_DIM + ROPE_HALF) & (lane < MLA_NOPE_DIM + MLA_ROPE_DIM)
    sin = jnp.sin(ang)
    c_ref[...] = jnp.where(lane < MLA_NOPE_DIM, 1.0, jnp.where(lo | hi, jnp.cos(ang), 0.0))
    slo_ref[...] = jnp.where(lo, -sin, 0.0)
    shi_ref[...] = jnp.where(hi, sin, 0.0)


def _rotary(x, ct, s_lo, s_hi):
    width = x.shape[1]
    return (x * ct + pltpu.roll(x, width - ROPE_HALF, 1) * s_lo
            + pltpu.roll(x, ROPE_HALF, 1) * s_hi)


def _rope_tables(positions):
    t = positions.size
    inv_freq = ROPE_BASE ** (-jnp.arange(0, MLA_ROPE_DIM, 2, dtype=F32) / MLA_ROPE_DIM)
    invf = jnp.zeros((1, LANES), F32).at[0, MLA_NOPE_DIM:MLA_NOPE_DIM + MLA_ROPE_DIM].set(
        jnp.concatenate([inv_freq, inv_freq]))
    return pl.pallas_call(
        _rope_table_kernel,
        grid=(t // ROW_TILE,),
        in_specs=[_row_spec(ROW_TILE, 1), _full_spec((1, LANES))],
        out_specs=[_row_spec(ROW_TILE, LANES)] * 3,
        out_shape=[jax.ShapeDtypeStruct((t, LANES), F32)] * 3,
        compiler_params=_params(1),
        name="rope_tables",
    )(positions.reshape(t, 1), invf)


MEM_BATCH_GROUP = 4


def _mem_kv_kernel(mem_ref, w_ref, k_ref, v_ref):
    group, n_mem, _ = mem_ref.shape
    rows = mem_ref[...].reshape(group * n_mem, D_MODEL).astype(BF16)
    kv = _dot(rows, w_ref[0].astype(BF16))
    k = (kv[:, :BRANCH_WIDTH] * (MEM_HEAD_DIM ** -0.5 * LOG2E)).astype(BF16)
    k_ref[0] = k.reshape(group, n_mem, BRANCH_WIDTH)
    v_ref[0] = kv[:, BRANCH_WIDTH:].astype(BF16).reshape(group, n_mem, BRANCH_WIDTH)


def _mem_kv(mem, w_mem_kv):
    depth = w_mem_kv.shape[0]
    batch, n_mem, _ = mem.shape
    group = math.gcd(batch, MEM_BATCH_GROUP)
    return pl.pallas_call(
        _mem_kv_kernel,
        grid=(batch // group, depth),
        in_specs=[pl.BlockSpec((group, n_mem, D_MODEL), lambda b, l: (b, 0, 0)),
                  pl.BlockSpec((1, D_MODEL, 2 * BRANCH_WIDTH), lambda b, l: (l, 0, 0))],
        out_specs=[pl.BlockSpec((1, group, n_mem, BRANCH_WIDTH), lambda b, l: (l, b, 0, 0))] * 2,
        out_shape=[jax.ShapeDtypeStruct((depth, batch, n_mem, BRANCH_WIDTH), BF16)] * 2,
        compiler_params=_params(2),
        name="mem_kv",
    )(mem, w_mem_kv)


GATE_W = N_BRANCHES * D_MODEL
COL_FOX_Q = 0
COL_FOX_K = COL_FOX_Q + BRANCH_WIDTH
COL_FOX_V = COL_FOX_K + BRANCH_WIDTH
COL_CQ = COL_FOX_V + BRANCH_WIDTH
COL_CKV = COL_CQ + MLA_Q_RANK
COL_QMEM = COL_CKV + MLA_KV_RANK
COL_GATE = COL_QMEM + BRANCH_WIDTH
MAIN_W = COL_GATE + GATE_W
GATE_CHUNK = 256
N_SPLIT = 3
FORGET_LANE = MLA_NOPE_DIM + MLA_ROPE_DIM


def _split3(x):
    hi = x.astype(BF16)
    r = x - hi.astype(F32)
    mid = r.astype(BF16)
    lo = (r - mid.astype(F32)).astype(BF16)
    return hi, mid, lo


def _mixer_inputs_kernel(*refs, entry_norm, tiles_per_seq):
    if entry_norm:
        x_ref, lng_ref, lnb_ref, *refs = refs
    else:
        x_ref, *refs = refs
    (wfox_ref, wmla_ref, wtail_ref, ws_ref, bf_ref, pq_ref, pk_ref, oq_ref, ok_ref,
     ct_ref, slo_ref, shi_ref, gq_ref, gkv_ref, wq_ref, wk_ref, wvt_ref, mk_ref, mv_ref,
     gates_ref, oc_ref, fqt_ref, fka_ref, fvt_ref, mqt_ref, mka_ref, mvt_ref, *rest) = refs
    carry_ref = rest[-1]
    tm = x_ref.shape[0]

    @pl.when(pl.program_id(0) % tiles_per_seq == 0)
    def _():
        carry_ref[...] = jnp.zeros_like(carry_ref)

    h = x_ref[...]
    if entry_norm:
        h = _layer_norm(h, lng_ref[...], lnb_ref[...])
        rest[0][...] = h
    xb = h.astype(BF16)

    def mm(c0, n):
        for w_ref, first in ((wtail_ref, COL_QMEM), (wmla_ref, COL_CQ), (wfox_ref, COL_FOX_Q)):
            if c0 >= first:
                return _dot(xb, w_ref[:, c0 - first:c0 - first + n])

    gate_cols = iter(range(0, GATE_W, GATE_CHUNK))

    def emit_gates(n_chunks=2):
        for _ in range(n_chunks):
            c = next(gate_cols, None)
            if c is not None:
                g = mm(COL_GATE + c, GATE_CHUNK)
                gates_ref[:, c:c + GATE_CHUNK] = (0.5 * jnp.tanh(0.5 * g) + 0.5).astype(BF16)

    small = _dot(xb, ws_ref[...])
    xq = _rms_norm(mm(COL_CQ, MLA_Q_RANK), gq_ref[...]).astype(BF16)
    xkv = _rms_norm(mm(COL_CKV, MLA_KV_RANK), gkv_ref[...]).astype(BF16)

    x = small + bf_ref[...]
    logf = jnp.minimum(x, 0.0) - jnp.log1p(jnp.exp(-jnp.abs(x)))
    row = lax.broadcasted_iota(jnp.int32, (tm, tm), 0)
    col = lax.broadcasted_iota(jnp.int32, (tm, tm), 1)
    tri = jnp.where(col <= row, 1.0, 0.0).astype(BF16)
    hi, mid, lo = _split3(logf)
    cum = _dot(tri, hi) + _dot(tri, mid) + _dot(tri, lo) + carry_ref[...]
    carry_ref[...] = cum[tm - 1:tm, :]
    emit_gates()

    lane = lax.broadcasted_iota(jnp.int32, (tm, LANES), 1)
    span = lane - FORGET_LANE
    chi, cmid, clo = _split3(cum * LOG2E)
    pieces = jnp.where((span >= 0) & (span < FOX_HEADS), chi,
                       jnp.where((span >= FOX_HEADS) & (span < 2 * FOX_HEADS), cmid,
                                 jnp.where((span >= 2 * FOX_HEADS) & (span < 3 * FOX_HEADS), clo,
                                           jnp.zeros_like(chi))))
    left = lane < HALF
    heads_per_dot = MXU_WIDTH // HALF
    for h0 in range(0, FOX_HEADS, heads_per_dot):
        c0 = h0 * HALF
        qp2 = mm(COL_FOX_Q + c0, MXU_WIDTH) * (FOX_HEAD_DIM ** -0.5 * LOG2E)
        kp2 = mm(COL_FOX_K + c0, MXU_WIDTH)
        grp4 = slice(h0 * LANES, (h0 + heads_per_dot) * LANES)
        q_spare = _dot(pieces, pq_ref[:, grp4]) + oq_ref[:, grp4]
        k_spare = _dot(pieces, pk_ref[:, grp4]) + ok_ref[:, grp4]
        for i in range(heads_per_dot):
            head = h0 + i
            pair = slice((i // 2) * LANES, (i // 2 + 1) * LANES)
            loc = slice(i * LANES, (i + 1) * LANES)
            grp = slice(head * LANES, (head + 1) * LANES)
            own = left if head % 2 == 0 else jnp.logical_not(left)
            fqt_ref[0, grp, :] = jnp.where(own, qp2[:, pair], q_spare[:, loc]).T.astype(BF16)
            fka_ref[:, grp] = jnp.where(own, kp2[:, pair], k_spare[:, loc]).astype(BF16)
        emit_gates()
    fvt_ref[0] = mm(COL_FOX_V, BRANCH_WIDTH).T.astype(BF16)

    qm = mm(COL_QMEM, BRANCH_WIDTH).astype(BF16)
    for hm in range(MEM_HEADS):
        grp = slice(hm * MEM_HEAD_DIM, (hm + 1) * MEM_HEAD_DIM)
        s = _dot_nt(qm[:, grp], mk_ref[0, 0, :, grp])
        e = jnp.exp2(s - jnp.max(s, axis=1, keepdims=True))
        o = _dot(e.astype(BF16), mv_ref[0, 0, :, grp])
        oc_ref[:, grp] = (o * (1.0 / jnp.sum(e, axis=1, keepdims=True))).astype(BF16)
    emit_gates()

    ct, s_lo, s_hi = ct_ref[...], slo_ref[...], shi_ref[...]
    groups_per_dot = MXU_WIDTH // LANES
    ct2, s_lo2, s_hi2 = (jnp.concatenate([tab] * groups_per_dot, axis=1) for tab in (ct, s_lo, s_hi))
    scale = (MLA_NOPE_DIM + MLA_ROPE_DIM) ** -0.5 * LOG2E
    kpe = _rotary(small, ct, s_lo, s_hi)
    kpe2 = jnp.concatenate([kpe] * groups_per_dot, axis=1)
    for n, c in enumerate(range(0, WIDE, MXU_WIDTH)):
        q = _rotary(_dot(xq, wq_ref[:, c:c + MXU_WIDTH]), ct2, s_lo2, s_hi2) * scale
        mqt_ref[0, c:c + MXU_WIDTH, :] = q.T.astype(BF16)
        mka_ref[:, c:c + MXU_WIDTH] = (_dot(xkv, wk_ref[:, c:c + MXU_WIDTH]) + kpe2).astype(BF16)
        if n == 1:
            emit_gates(3)
    mvt_ref[0] = _dot_nt(wvt_ref[...], xkv).astype(BF16)
    emit_gates(GATE_W // GATE_CHUNK)


def _fox_placement():
    pq = np.zeros((LANES, WIDE), np.float32)
    pk = np.zeros((LANES, WIDE), np.float32)
    oq = np.zeros((1, WIDE), np.float32)
    ok = np.zeros((1, WIDE), np.float32)
    for h in range(FOX_HEADS):
        base = h * LANES + (HALF if h % 2 == 0 else 0)
        for j in range(N_SPLIT):
            src = FORGET_LANE + j * FOX_HEADS + h
            pq[src, base + j] = 1.0
            ok[0, base + j] = 1.0
            oq[0, base + N_SPLIT + j] = 1.0
            pk[src, base + N_SPLIT + j] = -1.0
    return (jnp.asarray(pq, BF16), jnp.asarray(pk, BF16), jnp.asarray(oq), jnp.asarray(ok))


def _mixer_inputs(h, ln_in, w_main, w_small, b_forget3, rope_tabs, g_cq, g_ckv, wq, wk, wvt,
                  mk, mv, layer, batch, seq):
    t = h.shape[0]
    nt = seq // ROW_TILE
    n_mem = mk.shape[2]
    entry_norm = ln_in is not None
    pq, pk, oq, ok = _fox_placement()
    mem_spec = pl.BlockSpec((1, 1, n_mem, BRANCH_WIDTH), lambda i: (layer, i // nt, 0, 0))

    def transposed(rows):
        return (pl.BlockSpec((1, rows, ROW_TILE), lambda i: (i // nt, 0, i % nt)),
                jax.ShapeDtypeStruct((batch, rows, seq), BF16))

    def row_major(width):
        return _row_spec(ROW_TILE, width), jax.ShapeDtypeStruct((t, width), BF16)

    operands = [h]
    in_specs = [_row_spec(ROW_TILE, D_MODEL)]
    if entry_norm:
        operands += [ln_in[0].reshape(1, -1), ln_in[1].reshape(1, -1)]
        in_specs += [_full_spec((1, D_MODEL))] * 2
    operands += [*w_main, w_small, b_forget3, pq, pk, oq, ok, *rope_tabs,
                 g_cq.reshape(1, -1), g_ckv.reshape(1, -1), wq, wk, wvt, mk, mv]
    in_specs += [_full_spec((D_MODEL, COL_CQ - COL_FOX_Q)), _full_spec((D_MODEL, COL_QMEM - COL_CQ)),
                 _full_spec((D_MODEL, MAIN_W - COL_QMEM)), _full_spec((D_MODEL, LANES)),
                 _full_spec((1, LANES)),
                 _full_spec((LANES, WIDE)), _full_spec((LANES, WIDE)), _full_spec((1, WIDE)),
                 _full_spec((1, WIDE))] + [_row_spec(ROW_TILE, LANES)] * 3 + [
                 _full_spec((1, MLA_Q_RANK)), _full_spec((1, MLA_KV_RANK)),
                 _full_spec((MLA_Q_RANK, WIDE)), _full_spec((MLA_KV_RANK, WIDE)),
                 _full_spec((BRANCH_WIDTH, MLA_KV_RANK)), mem_spec, mem_spec]

    outs = [row_major(GATE_W), row_major(BRANCH_WIDTH),
            transposed(WIDE), row_major(WIDE), transposed(BRANCH_WIDTH),
            transposed(WIDE), row_major(WIDE), transposed(BRANCH_WIDTH)]
    out_specs = [spec for spec, _ in outs]
    out_shape = [shape for _, shape in outs]
    if entry_norm:
        out_specs.append(_row_spec(ROW_TILE, D_MODEL))
        out_shape.append(jax.ShapeDtypeStruct((t, D_MODEL), F32))
    return pl.pallas_call(
        functools.partial(_mixer_inputs_kernel, entry_norm=entry_norm, tiles_per_seq=nt),
        grid=(t // ROW_TILE,),
        in_specs=in_specs,
        out_specs=out_specs,
        out_shape=out_shape,
        scratch_shapes=[pltpu.VMEM((1, LANES), F32)],
        compiler_params=_params(1),
        name="mixer_inputs",
    )(*operands)


def _attn_kernel(qt_ref, k_ref, vt_ref, o_ref, sa_ref, sb_ref, ta_ref, tb_ref, m_ref, acc_ref,
                 *, granule_shift):
    tq = tk = sa_ref.shape[1]
    n_q_tiles = qt_ref.shape[2] // tq
    heads = range(N_GROUPS)
    buf_a = (sa_ref, ta_ref)
    buf_b = (sb_ref, tb_ref)
    ones_rows = jnp.ones((SUM_ROWS, tk), BF16)

    def start_of(tile_index, size):
        start = tile_index * size
        return start if isinstance(start, int) else pl.multiple_of(start, size)

    def scores(qi, j, buf, h, masked):
        q0 = start_of(qi, tq)
        ks = start_of(j, tk)
        grp = slice(h * LANES, (h + 1) * LANES)
        s = _dot(k_ref[0, pl.ds(ks, tk), grp], qt_ref[0, grp, pl.ds(q0, tq)])
        if masked:
            key = lax.broadcasted_iota(jnp.int32, (tk, tq), 0)
            qry = lax.broadcasted_iota(jnp.int32, (tk, tq), 1)
            s = jnp.where((key >> granule_shift) <= (qry >> granule_shift), s, NEG_INF)
        buf[0][h] = s
        buf[1][h] = jnp.max(s, axis=0, keepdims=True)

    def update(j, buf, h):
        ks = start_of(j, tk)
        m = m_ref[h]
        m_new = jnp.maximum(m, buf[1][h])
        alpha = jnp.exp2(m - m_new)
        e = jnp.exp2(buf[0][h] - m_new)
        m_ref[h] = m_new
        v_t = jnp.concatenate([vt_ref[0, h * HALF:(h + 1) * HALF, pl.ds(ks, tk)], ones_rows],
                              axis=0)
        acc_ref[h] = alpha * acc_ref[h] + _dot(v_t, e.astype(BF16))

    def tile_step(qi, j, cur, nxt=None, next_masked=False):
        if nxt is not None:
            for h in range(SCORE_LEAD):
                scores(qi, j + 1, nxt, h, next_masked)
        for h in heads:
            update(j, cur, h)
            if nxt is not None and h + SCORE_LEAD < N_GROUPS:
                scores(qi, j + 1, nxt, h + SCORE_LEAD, next_masked)

    def start_tile():
        m_ref[...] = jnp.full(m_ref.shape, NEG_INF, F32)
        acc_ref[...] = jnp.zeros(acc_ref.shape, F32)

    def finish_tile(qi, next_qi=None):
        if next_qi is not None:
            for h in heads:
                scores(next_qi, 0, buf_a, h, False)
        q0 = start_of(qi, tq)
        for pair in range(N_GROUPS // 2):
            halves = []
            for h in (2 * pair, 2 * pair + 1):
                halves.append(acc_ref[h, :HALF, :] * (1.0 / acc_ref[h, HALF:HALF + 1, :]))
            o_t = jnp.concatenate(halves, axis=0)
            o_ref[0, pl.ds(q0, tq), pair * LANES:(pair + 1) * LANES] = o_t.T.astype(BF16)

    start_tile()
    for h in heads:
        scores(0, 0, buf_a, h, True)
    tile_step(0, 0, buf_a)
    finish_tile(0, 1 if n_q_tiles > 1 else None)

    def q_tile(qi, _):
        start_tile()

        def two_tiles(i, _):
            tile_step(qi, 2 * i, buf_a, buf_b)
            tile_step(qi, 2 * i + 1, buf_b, buf_a)
            return 0

        n_pairs = (qi - 1) // 2
        lax.fori_loop(0, n_pairs, two_tiles, 0)
        j = 2 * n_pairs

        @pl.when(qi % 2 == 1)
        def _():
            tile_step(qi, j, buf_a, buf_b, next_masked=True)
            tile_step(qi, j + 1, buf_b)

        @pl.when(qi % 2 == 0)
        def _():
            tile_step(qi, j, buf_a, buf_b)
            tile_step(qi, j + 1, buf_b, buf_a, next_masked=True)
            tile_step(qi, j + 2, buf_a)

        @pl.when(qi < n_q_tiles - 1)
        def _():
            finish_tile(qi, qi + 1)

        @pl.when(qi == n_q_tiles - 1)
        def _():
            finish_tile(qi)

        return 0

    lax.fori_loop(1, n_q_tiles, q_tile, 0)


def _causal_attention(q_t, k_aug, v_t, batch, seq, granule):
    tile = ATTN_TILE
    k3 = k_aug.reshape(batch, seq, WIDE)
    out = pl.pallas_call(
        functools.partial(_attn_kernel, granule_shift=granule.bit_length() - 1),
        grid=(batch,),
        in_specs=[pl.BlockSpec((1, WIDE, seq), lambda b: (b, 0, 0)),
                  pl.BlockSpec((1, seq, WIDE), lambda b: (b, 0, 0)),
                  pl.BlockSpec((1, BRANCH_WIDTH, seq), lambda b: (b, 0, 0))],
        out_specs=pl.BlockSpec((1, seq, BRANCH_WIDTH), lambda b: (b, 0, 0)),
        out_shape=jax.ShapeDtypeStruct((batch, seq, BRANCH_WIDTH), BF16),
        scratch_shapes=[pltpu.VMEM((N_GROUPS, tile, tile), F32),
                        pltpu.VMEM((N_GROUPS, tile, tile), F32),
                        pltpu.VMEM((N_GROUPS, 1, tile), F32),
                        pltpu.VMEM((N_GROUPS, 1, tile), F32),
                        pltpu.VMEM((N_GROUPS, 1, tile), F32),
                        pltpu.VMEM((N_GROUPS, HALF + SUM_ROWS, tile), F32)],
        compiler_params=_params(1),
        name="causal_attention_g%d" % granule,
    )(q_t, k3, v_t)
    return out.reshape(batch * seq, BRANCH_WIDTH)


FF_CHUNK = 1024
ROW_PARTS = 2


def _merge_ffn_kernel(oa_ref, ob_ref, oc_ref, gates_ref, h_ref, wbr_ref, wout_ref, g1_ref, b1_ref,
                      w1_ref, w2_ref, g2_ref, b2_ref, o_ref, *, alpha):
    n_parts = ROW_PARTS
    part = o_ref.shape[0] // n_parts
    rows = [slice(p * part, (p + 1) * part) for p in range(n_parts)]

    def merge(r):
        merged = None
        for n, br_ref in enumerate((oa_ref, ob_ref, oc_ref)):
            gate = gates_ref[r, n * D_MODEL:(n + 1) * D_MODEL].astype(F32)
            term = gate * _dot(br_ref[r, :], wbr_ref[n])
            merged = term if merged is None else merged + term
        y = _dot(merged.astype(BF16), wout_ref[...])
        return _layer_norm(alpha * h_ref[r, :] + y, g1_ref[...], b1_ref[...])

    def mlp(h1):
        xb = h1.astype(BF16)
        acc = alpha * h1
        for c in range(0, D_FF, FF_CHUNK):
            u = jnp.maximum(_dot(xb, w1_ref[:, c:c + FF_CHUNK]), 0.0)
            acc = acc + _dot((u * u).astype(BF16), w2_ref[c:c + FF_CHUNK, :])
        return _layer_norm(acc, g2_ref[...], b2_ref[...])

    h1 = [merge(r) for r in rows]
    for r, h1_part in zip(rows, h1):
        o_ref[r, :] = mlp(h1_part)


def _merge_ffn(oa, ob, oc, gates, h, w_br, w_out, g1, b1, w1, w2, g2, b2, alpha):
    t = h.shape[0]
    br = _row_spec(ROW_TILE, BRANCH_WIDTH)
    vec = _full_spec((1, D_MODEL))
    return pl.pallas_call(
        functools.partial(_merge_ffn_kernel, alpha=alpha),
        grid=(t // ROW_TILE,),
        in_specs=[br, br, br, _row_spec(ROW_TILE, GATE_W), _row_spec(ROW_TILE, D_MODEL),
                  _full_spec((N_BRANCHES, BRANCH_WIDTH, D_MODEL)), _full_spec((D_MODEL, D_MODEL)),
                  vec, vec, _full_spec((D_MODEL, D_FF)), _full_spec((D_FF, D_MODEL)), vec, vec],
        out_specs=_row_spec(ROW_TILE, D_MODEL),
        out_shape=jax.ShapeDtypeStruct((t, D_MODEL), F32),
        compiler_params=_params(1),
        name="merge_ffn",
    )(oa, ob, oc, gates, h, w_br, w_out, g1.reshape(1, -1), b1.reshape(1, -1), w1, w2,
      g2.reshape(1, -1), b2.reshape(1, -1))


def _layer_weights(w_in, b_forget, w_uq, w_ukv):
    depth = w_in.shape[0]
    fox_w = 3 * BRANCH_WIDTH
    tail = LANES - FORGET_LANE - N_SPLIT * FOX_HEADS
    f0 = fox_w
    f1 = f0 + FOX_HEADS
    r0 = f1 + MLA_Q_RANK + MLA_KV_RANK
    r1 = r0 + MLA_ROPE_DIM
    f_log, k_rope = w_in[..., f0:f1], w_in[..., r0:r1]
    w_main = (w_in[..., :f0].astype(BF16), w_in[..., f1:r0].astype(BF16), w_in[..., r1:].astype(BF16))
    w_small = jnp.pad(jnp.concatenate([k_rope] + [f_log] * N_SPLIT, axis=-1).astype(BF16),
                      ((0, 0), (0, 0), (MLA_NOPE_DIM, tail)))
    b3 = jnp.pad(jnp.concatenate([b_forget] * N_SPLIT, axis=-1),
                 ((0, 0), (FORGET_LANE, tail))).reshape(depth, 1, LANES)

    dq = MLA_NOPE_DIM + MLA_ROPE_DIM
    wq = w_uq.astype(BF16).reshape(depth, MLA_Q_RANK, MLA_HEADS, dq)
    wq = jnp.pad(wq, ((0, 0), (0, 0), (0, 0), (0, LANES - dq))).reshape(depth, MLA_Q_RANK, -1)
    wkv = w_ukv.astype(BF16).reshape(depth, MLA_KV_RANK, MLA_HEADS, MLA_NOPE_DIM + MLA_V_DIM)
    wk = jnp.pad(wkv[..., :MLA_NOPE_DIM], ((0, 0), (0, 0), (0, 0), (0, LANES - MLA_NOPE_DIM)))
    wk = wk.reshape(depth, MLA_KV_RANK, -1)
    wv_t = jnp.swapaxes(wkv[..., MLA_NOPE_DIM:].reshape(depth, MLA_KV_RANK, -1), 1, 2)
    return w_main, w_small, b3, wq, wk, wv_t


def kernel(x, mem, positions, ln_in_g, ln_in_b, w_in, b_forget, w_uq, g_cq, w_ukv, g_ckv,
           w_mem_kv, w_br, w_out, ln1_g, ln1_b, w_ff1, w_ff2, ln2_g, ln2_b):
    batch, seq, _ = x.shape
    depth = w_in.shape[0]
    assert seq % ATTN_TILE == 0 and seq % ROW_TILE == 0
    alpha = float((2 * depth) ** 0.25)

    w_main, w_small, b3, wq, wk, wv_t = _layer_weights(w_in, b_forget, w_uq, w_ukv)
    w_br_b = w_br.astype(BF16)
    w_out_b = w_out.astype(BF16)
    w_ff1_b = w_ff1.astype(BF16)
    w_ff2_b = w_ff2.astype(BF16)

    rope_tabs = _rope_tables(positions)
    mk, mv = _mem_kv(mem, w_mem_kv)
    h = x.reshape(batch * seq, D_MODEL)
    for l in range(depth):
        outs = _mixer_inputs(h, (ln_in_g, ln_in_b) if l == 0 else None,
                             tuple(w[l] for w in w_main), w_small[l], b3[l],
                             rope_tabs, g_cq[l], g_ckv[l], wq[l], wk[l], wv_t[l], mk, mv, l,
                             batch, seq)
        gates, o_c, fq_t, fk_aug, fv_t, mq_t, mk_aug, mv_t = outs[:8]
        if l == 0:
            h = outs[8]
        o_a = _causal_attention(fq_t, fk_aug, fv_t, batch, seq, 1)
        o_b = _causal_attention(mq_t, mk_aug, mv_t, batch, seq, CHUNK)
        h = _merge_ffn(o_a, o_b, o_c, gates, h, w_br_b[l], w_out_b[l], ln1_g[l], ln1_b[l],
                       w_ff1_b[l], w_ff2_b[l], ln2_g[l], ln2_b[l], alpha)
    return h.reshape(batch, seq, D_MODEL)
```

```python
import functools
import math

import numpy as np
import jax
import jax.numpy as jnp
from jax import lax
from jax.experimental import pallas as pl
from jax.experimental.pallas import tpu as pltpu

D_MODEL = 1024
CHUNK = 64
FOX_HEADS = 8
FOX_HEAD_DIM = 64
MLA_HEADS = 8
MLA_NOPE_DIM = 64
MLA_ROPE_DIM = 32
MLA_V_DIM = 64
MLA_Q_RANK = 384
MLA_KV_RANK = 256
ROPE_BASE = 10000.0
MEM_HEADS = 4
MEM_HEAD_DIM = 128
N_BRANCHES = 3
BRANCH_WIDTH = 512
D_FF = 4 * D_MODEL
LN_EPS = 1e-5
RMS_EPS = 1e-6
NEG_INF = -1e30
LOG2E = math.log2(math.e)

LANES = 128
HALF = 64
N_GROUPS = 8
WIDE = N_GROUPS * LANES
MXU_WIDTH = 256
VMEM_LIMIT = 56 * 1024 * 1024

ROW_TILE = 512
ATTN_TILE = 512
SCORE_LEAD = 3
SUM_ROWS = 16

F32 = jnp.float32
BF16 = jnp.bfloat16


def _params(n_axes):
    return pltpu.CompilerParams(
        dimension_semantics=("arbitrary",) * n_axes, vmem_limit_bytes=VMEM_LIMIT)


def _dot(a, b):
    return jnp.dot(a, b, preferred_element_type=F32)


def _dot_nt(a, b):
    return lax.dot_general(a, b, (((1,), (1,)), ((), ())), preferred_element_type=F32)


def _layer_norm(z, g, b):
    mu = jnp.mean(z, axis=-1, keepdims=True)
    d = z - mu
    var = jnp.mean(d * d, axis=-1, keepdims=True)
    return d * lax.rsqrt(var + LN_EPS) * g + b


def _rms_norm(z, g):
    return z * lax.rsqrt(jnp.mean(z * z, axis=-1, keepdims=True) + RMS_EPS) * g


def _row_spec(tile, width, col_block=0):
    return pl.BlockSpec((tile, width), lambda i: (i, col_block))


def _full_spec(shape):
    return pl.BlockSpec(shape, lambda *_: (0,) * len(shape), pipeline_mode=pl.Buffered(1))


ROPE_HALF = MLA_ROPE_DIM // 2


def _rope_table_kernel(pos_ref, invf_ref, c_ref, slo_ref, shi_ref):
    ang = pos_ref[...].astype(F32) * invf_ref[...]
    lane = lax.broadcasted_iota(jnp.int32, ang.shape, 1)
    lo = (lane >= MLA_NOPE_DIM) & (lane < MLA_NOPE_DIM + ROPE_HALF)
    hi = (lane >= MLA_NOPE_DIM + ROPE_HALF) & (lane < MLA_NOPE_DIM + MLA_ROPE_DIM)
    sin = jnp.sin(ang)
    c_ref[...] = jnp.where(lane < MLA_NOPE_DIM, 1.0, jnp.where(lo | hi, jnp.cos(ang), 0.0))
    slo_ref[...] = jnp.where(lo, -sin, 0.0)
    shi_ref[...] = jnp.where(hi, sin, 0.0)


def _rotary(x, ct, s_lo, s_hi):
    width = x.shape[1]
    return (x * ct + pltpu.roll(x, width - ROPE_HALF, 1) * s_lo
            + pltpu.roll(x, ROPE_HALF, 1) * s_hi)


def _rope_tables(positions):
    t = positions.size
    inv_freq = ROPE_BASE ** (-jnp.arange(0, MLA_ROPE_DIM, 2, dtype=F32) / MLA_ROPE_DIM)
    invf = jnp.zeros((1, LANES), F32).at[0, MLA_NOPE_DIM:MLA_NOPE_DIM + MLA_ROPE_DIM].set(
        jnp.concatenate([inv_freq, inv_freq]))
    return pl.pallas_call(
        _rope_table_kernel,
        grid=(t // ROW_TILE,),
        in_specs=[_row_spec(ROW_TILE, 1), _full_spec((1, LANES))],
        out_specs=[_row_spec(ROW_TILE, LANES)] * 3,
        out_shape=[jax.ShapeDtypeStruct((t, LANES), F32)] * 3,
        compiler_params=_params(1),
        name="rope_tables",
    )(positions.reshape(t, 1), invf)


MEM_BATCH_GROUP = 4


def _mem_kv_kernel(mem_ref, w_ref, k_ref, v_ref):
    group, n_mem, _ = mem_ref.shape
    rows = mem_ref[...].reshape(group * n_mem, D_MODEL).astype(BF16)
    kv = _dot(rows, w_ref[0].astype(BF16))
    k = (kv[:, :BRANCH_WIDTH] * (MEM_HEAD_DIM ** -0.5 * LOG2E)).astype(BF16)
    k_ref[0] = k.reshape(group, n_mem, BRANCH_WIDTH)
    v_ref[0] = kv[:, BRANCH_WIDTH:].astype(BF16).reshape(group, n_mem, BRANCH_WIDTH)


def _mem_kv(mem, w_mem_kv):
    depth = w_mem_kv.shape[0]
    batch, n_mem, _ = mem.shape
    group = math.gcd(batch, MEM_BATCH_GROUP)
    return pl.pallas_call(
        _mem_kv_kernel,
        grid=(batch // group, depth),
        in_specs=[pl.BlockSpec((group, n_mem, D_MODEL), lambda b, l: (b, 0, 0)),
                  pl.BlockSpec((1, D_MODEL, 2 * BRANCH_WIDTH), lambda b, l: (l, 0, 0))],
        out_specs=[pl.BlockSpec((1, group, n_mem, BRANCH_WIDTH), lambda b, l: (l, b, 0, 0))] * 2,
        out_shape=[jax.ShapeDtypeStruct((depth, batch, n_mem, BRANCH_WIDTH), BF16)] * 2,
        compiler_params=_params(2),
        name="mem_kv",
    )(mem, w_mem_kv)


GATE_W = N_BRANCHES * D_MODEL
COL_FOX_Q = 0
COL_FOX_K = COL_FOX_Q + BRANCH_WIDTH
COL_FOX_V = COL_FOX_K + BRANCH_WIDTH
COL_CQ = COL_FOX_V + BRANCH_WIDTH
COL_CKV = COL_CQ + MLA_Q_RANK
COL_QMEM = COL_CKV + MLA_KV_RANK
COL_GATE = COL_QMEM + BRANCH_WIDTH
MAIN_W = COL_GATE + GATE_W
GATE_CHUNK = 256
N_SPLIT = 3
FORGET_LANE = MLA_NOPE_DIM + MLA_ROPE_DIM


def _split3(x):
    hi = x.astype(BF16)
    r = x - hi.astype(F32)
    mid = r.astype(BF16)
    lo = (r - mid.astype(F32)).astype(BF16)
    return hi, mid, lo


def _mixer_inputs_kernel(*refs, entry_norm, tiles_per_seq):
    if entry_norm:
        x_ref, lng_ref, lnb_ref, *refs = refs
    else:
        x_ref, *refs = refs
    (wfox_ref, wmla_ref, wtail_ref, ws_ref, bf_ref, pq_ref, pk_ref, oq_ref, ok_ref,
     ct_ref, slo_ref, shi_ref, gq_ref, gkv_ref, wq_ref, wk_ref, wvt_ref, mk_ref, mv_ref,
     gates_ref, oc_ref, fqt_ref, fka_ref, fvt_ref, mqt_ref, mka_ref, mvt_ref, *rest) = refs
    carry_ref = rest[-1]
    tm = x_ref.shape[0]

    @pl.when(pl.program_id(0) % tiles_per_seq == 0)
    def _():
        carry_ref[...] = jnp.zeros_like(carry_ref)

    h = x_ref[...]
    if entry_norm:
        h = _layer_norm(h, lng_ref[...], lnb_ref[...])
        rest[0][...] = h
    xb = h.astype(BF16)

    def mm(c0, n):
        for w_ref, first in ((wtail_ref, COL_QMEM), (wmla_ref, COL_CQ), (wfox_ref, COL_FOX_Q)):
            if c0 >= first:
                return _dot(xb, w_ref[:, c0 - first:c0 - first + n])

    gate_cols = iter(range(0, GATE_W, GATE_CHUNK))

    def emit_gates(n_chunks=2):
        for _ in range(n_chunks):
            c = next(gate_cols, None)
            if c is not None:
                g = mm(COL_GATE + c, GATE_CHUNK)
                gates_ref[:, c:c + GATE_CHUNK] = (0.5 * jnp.tanh(0.5 * g) + 0.5).astype(BF16)

    small = _dot(xb, ws_ref[...])
    xq = _rms_norm(mm(COL_CQ, MLA_Q_RANK), gq_ref[...]).astype(BF16)
    xkv = _rms_norm(mm(COL_CKV, MLA_KV_RANK), gkv_ref[...]).astype(BF16)

    x = small + bf_ref[...]
    logf = jnp.minimum(x, 0.0) - jnp.log1p(jnp.exp(-jnp.abs(x)))
    row = lax.broadcasted_iota(jnp.int32, (tm, tm), 0)
    col = lax.broadcasted_iota(jnp.int32, (tm, tm), 1)
    tri = jnp.where(col <= row, 1.0, 0.0).astype(BF16)
    hi, mid, lo = _split3(logf)
    cum = _dot(tri, hi) + _dot(tri, mid) + _dot(tri, lo) + carry_ref[...]
    carry_ref[...] = cum[tm - 1:tm, :]
    emit_gates()

    lane = lax.broadcasted_iota(jnp.int32, (tm, LANES), 1)
    span = lane - FORGET_LANE
    chi, cmid, clo = _split3(cum * LOG2E)
    pieces = jnp.where((span >= 0) & (span < FOX_HEADS), chi,
                       jnp.where((span >= FOX_HEADS) & (span < 2 * FOX_HEADS), cmid,
                                 jnp.where((span >= 2 * FOX_HEADS) & (span < 3 * FOX_HEADS), clo,
                                           jnp.zeros_like(chi))))
    left = lane < HALF
    heads_per_dot = MXU_WIDTH // HALF
    for h0 in range(0, FOX_HEADS, heads_per_dot):
        c0 = h0 * HALF
        qp2 = mm(COL_FOX_Q + c0, MXU_WIDTH) * (FOX_HEAD_DIM ** -0.5 * LOG2E)
        kp2 = mm(COL_FOX_K + c0, MXU_WIDTH)
        grp4 = slice(h0 * LANES, (h0 + heads_per_dot) * LANES)
        q_spare = _dot(pieces, pq_ref[:, grp4]) + oq_ref[:, grp4]
        k_spare = _dot(pieces, pk_ref[:, grp4]) + ok_ref[:, grp4]
        for i in range(heads_per_dot):
            head = h0 + i
            pair = slice((i // 2) * LANES, (i // 2 + 1) * LANES)
            loc = slice(i * LANES, (i + 1) * LANES)
            grp = slice(head * LANES, (head + 1) * LANES)
            own = left if head % 2 == 0 else jnp.logical_not(left)
            fqt_ref[0, grp, :] = jnp.where(own, qp2[:, pair], q_spare[:, loc]).T.astype(BF16)
            fka_ref[:, grp] = jnp.where(own, kp2[:, pair], k_spare[:, loc]).astype(BF16)
        emit_gates()
    fvt_ref[0] = mm(COL_FOX_V, BRANCH_WIDTH).T.astype(BF16)

    qm = mm(COL_QMEM, BRANCH_WIDTH).astype(BF16)
    for hm in range(MEM_HEADS):
        grp = slice(hm * MEM_HEAD_DIM, (hm + 1) * MEM_HEAD_DIM)
        s = _dot_nt(qm[:, grp], mk_ref[0, 0, :, grp])
        e = jnp.exp2(s - jnp.max(s, axis=1, keepdims=True))
        o = _dot(e.astype(BF16), mv_ref[0, 0, :, grp])
        oc_ref[:, grp] = (o * (1.0 / jnp.sum(e, axis=1, keepdims=True))).astype(BF16)
    emit_gates()

    ct, s_lo, s_hi = ct_ref[...], slo_ref[...], shi_ref[...]
    groups_per_dot = MXU_WIDTH // LANES
    ct2, s_lo2, s_hi2 = (jnp.concatenate([tab] * groups_per_dot, axis=1) for tab in (ct, s_lo, s_hi))
    scale = (MLA_NOPE_DIM + MLA_ROPE_DIM) ** -0.5 * LOG2E
    kpe = _rotary(small, ct, s_lo, s_hi)
    kpe2 = jnp.concatenate([kpe] * groups_per_dot, axis=1)
    for n, c in enumerate(range(0, WIDE, MXU_WIDTH)):
        q = _rotary(_dot(xq, wq_ref[:, c:c + MXU_WIDTH]), ct2, s_lo2, s_hi2) * scale
        mqt_ref[0, c:c + MXU_WIDTH, :] = q.T.astype(BF16)
        mka_ref[:, c:c + MXU_WIDTH] = (_dot(xkv, wk_ref[:, c:c + MXU_WIDTH]) + kpe2).astype(BF16)
        if n == 1:
            emit_gates(3)
    mvt_ref[0] = _dot_nt(wvt_ref[...], xkv).astype(BF16)
    emit_gates(GATE_W // GATE_CHUNK)


def _fox_placement():
    pq = np.zeros((LANES, WIDE), np.float32)
    pk = np.zeros((LANES, WIDE), np.float32)
    oq = np.zeros((1, WIDE), np.float32)
    ok = np.zeros((1, WIDE), np.float32)
    for h in range(FOX_HEADS):
        base = h * LANES + (HALF if h % 2 == 0 else 0)
        for j in range(N_SPLIT):
            src = FORGET_LANE + j * FOX_HEADS + h
            pq[src, base + j] = 1.0
            ok[0, base + j] = 1.0
            oq[0, base + N_SPLIT + j] = 1.0
            pk[src, base + N_SPLIT + j] = -1.0
    return (jnp.asarray(pq, BF16), jnp.asarray(pk, BF16), jnp.asarray(oq), jnp.asarray(ok))


def _mixer_inputs(h, ln_in, w_main, w_small, b_forget3, rope_tabs, g_cq, g_ckv, wq, wk, wvt,
                  mk, mv, layer, batch, seq):
    t = h.shape[0]
    nt = seq // ROW_TILE
    n_mem = mk.shape[2]
    entry_norm = ln_in is not None
    pq, pk, oq, ok = _fox_placement()
    mem_spec = pl.BlockSpec((1, 1, n_mem, BRANCH_WIDTH), lambda i: (layer, i // nt, 0, 0))

    def transposed(rows):
        return (pl.BlockSpec((1, rows, ROW_TILE), lambda i: (i // nt, 0, i % nt)),
                jax.ShapeDtypeStruct((batch, rows, seq), BF16))

    def row_major(width):
        return _row_spec(ROW_TILE, width), jax.ShapeDtypeStruct((t, width), BF16)

    operands = [h]
    in_specs = [_row_spec(ROW_TILE, D_MODEL)]
    if entry_norm:
        operands += [ln_in[0].reshape(1, -1), ln_in[1].reshape(1, -1)]
        in_specs += [_full_spec((1, D_MODEL))] * 2
    operands += [*w_main, w_small, b_forget3, pq, pk, oq, ok, *rope_tabs,
                 g_cq.reshape(1, -1), g_ckv.reshape(1, -1), wq, wk, wvt, mk, mv]
    in_specs += [_full_spec((D_MODEL, COL_CQ - COL_FOX_Q)), _full_spec((D_MODEL, COL_QMEM - COL_CQ)),
                 _full_spec((D_MODEL, MAIN_W - COL_QMEM)), _full_spec((D_MODEL, LANES)),
                 _full_spec((1, LANES)),
                 _full_spec((LANES, WIDE)), _full_spec((LANES, WIDE)), _full_spec((1, WIDE)),
                 _full_spec((1, WIDE))] + [_row_spec(ROW_TILE, LANES)] * 3 + [
                 _full_spec((1, MLA_Q_RANK)), _full_spec((1, MLA_KV_RANK)),
                 _full_spec((MLA_Q_RANK, WIDE)), _full_spec((MLA_KV_RANK, WIDE)),
                 _full_spec((BRANCH_WIDTH, MLA_KV_RANK)), mem_spec, mem_spec]

    outs = [row_major(GATE_W), row_major(BRANCH_WIDTH),
            transposed(WIDE), row_major(WIDE), transposed(BRANCH_WIDTH),
            transposed(WIDE), row_major(WIDE), transposed(BRANCH_WIDTH)]
    out_specs = [spec for spec, _ in outs]
    out_shape = [shape for _, shape in outs]
    if entry_norm:
        out_specs.append(_row_spec(ROW_TILE, D_MODEL))
        out_shape.append(jax.ShapeDtypeStruct((t, D_MODEL), F32))
    return pl.pallas_call(
        functools.partial(_mixer_inputs_kernel, entry_norm=entry_norm, tiles_per_seq=nt),
        grid=(t // ROW_TILE,),
        in_specs=in_specs,
        out_specs=out_specs,
        out_shape=out_shape,
        scratch_shapes=[pltpu.VMEM((1, LANES), F32)],
        compiler_params=_params(1),
        name="mixer_inputs",
    )(*operands)


def _attn_kernel(qt_ref, k_ref, vt_ref, o_ref, sa_ref, sb_ref, ta_ref, tb_ref, m_ref, acc_ref,
                 *maybe_bias_ref, granule_shift):
    tq = tk = sa_ref.shape[1]
    n_q_tiles = qt_ref.shape[2] // tq
    heads = range(N_GROUPS)
    buf_a = (sa_ref, ta_ref)
    buf_b = (sb_ref, tb_ref)
    key = lax.broadcasted_iota(jnp.int32, (tk, tq), 0)
    qry = lax.broadcasted_iota(jnp.int32, (tk, tq), 1)
    if granule_shift:
        (bias_ref,) = maybe_bias_ref
        bias_ref[...] = jnp.where((key >> granule_shift) <= (qry >> granule_shift), 0.0, NEG_INF)
    ones_rows = jnp.ones((SUM_ROWS, tk), BF16)

    def start_of(tile_index, size):
        start = tile_index * size
        return start if isinstance(start, int) else pl.multiple_of(start, size)

    def scores(qi, j, buf, h, masked):
        q0 = start_of(qi, tq)
        ks = start_of(j, tk)
        grp = slice(h * LANES, (h + 1) * LANES)
        s = _dot(k_ref[0, pl.ds(ks, tk), grp], qt_ref[0, grp, pl.ds(q0, tq)])
        if masked and granule_shift:
            s = s + bias_ref[...]
        elif masked:
            s = jnp.where(key <= qry, s, NEG_INF)
        buf[0][h] = s
        buf[1][h] = jnp.max(s, axis=0, keepdims=True)

    def update(j, buf, h):
        ks = start_of(j, tk)
        m = m_ref[h]
        m_new = jnp.maximum(m, buf[1][h])
        alpha = jnp.exp2(m - m_new)
        e = jnp.exp2(buf[0][h] - m_new)
        m_ref[h] = m_new
        v_t = jnp.concatenate([vt_ref[0, h * HALF:(h + 1) * HALF, pl.ds(ks, tk)], ones_rows],
                              axis=0)
        acc_ref[h] = alpha * acc_ref[h] + _dot(v_t, e.astype(BF16))

    def tile_step(qi, j, cur, nxt=None, next_masked=False):
        if nxt is not None:
            for h in range(SCORE_LEAD):
                scores(qi, j + 1, nxt, h, next_masked)
        for h in heads:
            update(j, cur, h)
            if nxt is not None and h + SCORE_LEAD < N_GROUPS:
                scores(qi, j + 1, nxt, h + SCORE_LEAD, next_masked)

    def start_tile():
        m_ref[...] = jnp.full(m_ref.shape, NEG_INF, F32)
        acc_ref[...] = jnp.zeros(acc_ref.shape, F32)

    def finish_tile(qi, next_qi=None):
        if next_qi is not None:
            for h in heads:
                scores(next_qi, 0, buf_a, h, False)
        q0 = start_of(qi, tq)
        for pair in range(N_GROUPS // 2):
            halves = []
            for h in (2 * pair, 2 * pair + 1):
                halves.append(acc_ref[h, :HALF, :] * (1.0 / acc_ref[h, HALF:HALF + 1, :]))
            o_t = jnp.concatenate(halves, axis=0)
            o_ref[0, pl.ds(q0, tq), pair * LANES:(pair + 1) * LANES] = o_t.T.astype(BF16)

    start_tile()
    for h in heads:
        scores(0, 0, buf_a, h, True)
    tile_step(0, 0, buf_a)
    finish_tile(0, 1 if n_q_tiles > 1 else None)

    def q_tile(qi, _):
        start_tile()

        def two_tiles(i, _):
            tile_step(qi, 2 * i, buf_a, buf_b)
            tile_step(qi, 2 * i + 1, buf_b, buf_a)
            return 0

        n_pairs = (qi - 1) // 2
        lax.fori_loop(0, n_pairs, two_tiles, 0)
        j = 2 * n_pairs

        @pl.when(qi % 2 == 1)
        def _():
            tile_step(qi, j, buf_a, buf_b, next_masked=True)
            tile_step(qi, j + 1, buf_b)

        @pl.when(qi % 2 == 0)
        def _():
            tile_step(qi, j, buf_a, buf_b)
            tile_step(qi, j + 1, buf_b, buf_a, next_masked=True)
            tile_step(qi, j + 2, buf_a)

        @pl.when(qi < n_q_tiles - 1)
        def _():
            finish_tile(qi, qi + 1)

        @pl.when(qi == n_q_tiles - 1)
        def _():
            finish_tile(qi)

        return 0

    lax.fori_loop(1, n_q_tiles, q_tile, 0)


def _causal_attention(q_t, k_aug, v_t, batch, seq, granule):
    tile = ATTN_TILE
    k3 = k_aug.reshape(batch, seq, WIDE)
    out = pl.pallas_call(
        functools.partial(_attn_kernel, granule_shift=granule.bit_length() - 1),
        grid=(batch,),
        in_specs=[pl.BlockSpec((1, WIDE, seq), lambda b: (b, 0, 0)),
                  pl.BlockSpec((1, seq, WIDE), lambda b: (b, 0, 0)),
                  pl.BlockSpec((1, BRANCH_WIDTH, seq), lambda b: (b, 0, 0))],
        out_specs=pl.BlockSpec((1, seq, BRANCH_WIDTH), lambda b: (b, 0, 0)),
        out_shape=jax.ShapeDtypeStruct((batch, seq, BRANCH_WIDTH), BF16),
        scratch_shapes=[pltpu.VMEM((N_GROUPS, tile, tile), F32),
                        pltpu.VMEM((N_GROUPS, tile, tile), F32),
                        pltpu.VMEM((N_GROUPS, 1, tile), F32),
                        pltpu.VMEM((N_GROUPS, 1, tile), F32),
                        pltpu.VMEM((N_GROUPS, 1, tile), F32),
                        pltpu.VMEM((N_GROUPS, HALF + SUM_ROWS, tile), F32)]
        + ([pltpu.VMEM((tile, tile), F32)] if granule > 1 else []),
        compiler_params=_params(1),
        name="causal_attention_g%d" % granule,
    )(q_t, k3, v_t)
    return out.reshape(batch * seq, BRANCH_WIDTH)


FF_CHUNK = 1024
ROW_PARTS = 2


def _merge_ffn_kernel(oa_ref, ob_ref, oc_ref, gates_ref, h_ref, wbr_ref, wout_ref, g1_ref, b1_ref,
                      w1_ref, w2_ref, g2_ref, b2_ref, o_ref, *, alpha):
    n_parts = ROW_PARTS
    part = o_ref.shape[0] // n_parts
    rows = [slice(p * part, (p + 1) * part) for p in range(n_parts)]

    def merge(r):
        merged = None
        for n, br_ref in enumerate((oa_ref, ob_ref, oc_ref)):
            gate = gates_ref[r, n * D_MODEL:(n + 1) * D_MODEL].astype(F32)
            term = gate * _dot(br_ref[r, :], wbr_ref[n])
            merged = term if merged is None else merged + term
        y = _dot(merged.astype(BF16), wout_ref[...])
        return _layer_norm(alpha * h_ref[r, :] + y, g1_ref[...], b1_ref[...])

    def mlp(h1):
        xb = h1.astype(BF16)
        acc = alpha * h1
        for c in range(0, D_FF, FF_CHUNK):
            u = jnp.maximum(_dot(xb, w1_ref[:, c:c + FF_CHUNK]), 0.0)
            acc = acc + _dot((u * u).astype(BF16), w2_ref[c:c + FF_CHUNK, :])
        return _layer_norm(acc, g2_ref[...], b2_ref[...])

    h1 = [merge(r) for r in rows]
    for r, h1_part in zip(rows, h1):
        o_ref[r, :] = mlp(h1_part)


def _merge_ffn(oa, ob, oc, gates, h, w_br, w_out, g1, b1, w1, w2, g2, b2, alpha):
    t = h.shape[0]
    br = _row_spec(ROW_TILE, BRANCH_WIDTH)
    vec = _full_spec((1, D_MODEL))
    return pl.pallas_call(
        functools.partial(_merge_ffn_kernel, alpha=alpha),
        grid=(t // ROW_TILE,),
        in_specs=[br, br, br, _row_spec(ROW_TILE, GATE_W), _row_spec(ROW_TILE, D_MODEL),
                  _full_spec((N_BRANCHES, BRANCH_WIDTH, D_MODEL)), _full_spec((D_MODEL, D_MODEL)),
                  vec, vec, _full_spec((D_MODEL, D_FF)), _full_spec((D_FF, D_MODEL)), vec, vec],
        out_specs=_row_spec(ROW_TILE, D_MODEL),
        out_shape=jax.ShapeDtypeStruct((t, D_MODEL), F32),
        compiler_params=_params(1),
        name="merge_ffn",
    )(oa, ob, oc, gates, h, w_br, w_out, g1.reshape(1, -1), b1.reshape(1, -1), w1, w2,
      g2.reshape(1, -1), b2.reshape(1, -1))


def _layer_weights(w_in, b_forget, w_uq, w_ukv):
    depth = w_in.shape[0]
    fox_w = 3 * BRANCH_WIDTH
    tail = LANES - FORGET_LANE - N_SPLIT * FOX_HEADS
    f0 = fox_w
    f1 = f0 + FOX_HEADS
    r0 = f1 + MLA_Q_RANK + MLA_KV_RANK
    r1 = r0 + MLA_ROPE_DIM
    f_log, k_rope = w_in[..., f0:f1], w_in[..., r0:r1]
    w_main = (w_in[..., :f0].astype(BF16), w_in[..., f1:r0].astype(BF16), w_in[..., r1:].astype(BF16))
    w_small = jnp.pad(jnp.concatenate([k_rope] + [f_log] * N_SPLIT, axis=-1).astype(BF16),
                      ((0, 0), (0, 0), (MLA_NOPE_DIM, tail)))
    b3 = jnp.pad(jnp.concatenate([b_forget] * N_SPLIT, axis=-1),
                 ((0, 0), (FORGET_LANE, tail))).reshape(depth, 1, LANES)

    dq = MLA_NOPE_DIM + MLA_ROPE_DIM
    wq = w_uq.astype(BF16).reshape(depth, MLA_Q_RANK, MLA_HEADS, dq)
    wq = jnp.pad(wq, ((0, 0), (0, 0), (0, 0), (0, LANES - dq))).reshape(depth, MLA_Q_RANK, -1)
    wkv = w_ukv.astype(BF16).reshape(depth, MLA_KV_RANK, MLA_HEADS, MLA_NOPE_DIM + MLA_V_DIM)
    wk = jnp.pad(wkv[..., :MLA_NOPE_DIM], ((0, 0), (0, 0), (0, 0), (0, LANES - MLA_NOPE_DIM)))
    wk = wk.reshape(depth, MLA_KV_RANK, -1)
    wv_t = jnp.swapaxes(wkv[..., MLA_NOPE_DIM:].reshape(depth, MLA_KV_RANK, -1), 1, 2)
    return w_main, w_small, b3, wq, wk, wv_t


def kernel(x, mem, positions, ln_in_g, ln_in_b, w_in, b_forget, w_uq, g_cq, w_ukv, g_ckv,
           w_mem_kv, w_br, w_out, ln1_g, ln1_b, w_ff1, w_ff2, ln2_g, ln2_b):
    batch, seq, _ = x.shape
    depth = w_in.shape[0]
    assert seq % ATTN_TILE == 0 and seq % ROW_TILE == 0
    alpha = float((2 * depth) ** 0.25)

    w_main, w_small, b3, wq, wk, wv_t = _layer_weights(w_in, b_forget, w_uq, w_ukv)
    w_br_b = w_br.astype(BF16)
    w_out_b = w_out.astype(BF16)
    w_ff1_b = w_ff1.astype(BF16)
    w_ff2_b = w_ff2.astype(BF16)

    rope_tabs = _rope_tables(positions)
    mk, mv = _mem_kv(mem, w_mem_kv)
    h = x.reshape(batch * seq, D_MODEL)
    for l in range(depth):
        outs = _mixer_inputs(h, (ln_in_g, ln_in_b) if l == 0 else None,
                             tuple(w[l] for w in w_main), w_small[l], b3[l],
                             rope_tabs, g_cq[l], g_ckv[l], wq[l], wk[l], wv_t[l], mk, mv, l,
                             batch, seq)
        gates, o_c, fq_t, fk_aug, fv_t, mq_t, mk_aug, mv_t = outs[:8]
        if l == 0:
            h = outs[8]
        o_a = _causal_attention(fq_t, fk_aug, fv_t, batch, seq, 1)
        o_b = _causal_attention(mq_t, mk_aug, mv_t, batch, seq, CHUNK)
        h = _merge_ffn(o_a, o_b, o_c, gates, h, w_br_b[l], w_out_b[l], ln1_g[l], ln1_b[l],
                       w_ff1_b[l], w_ff2_b[l], ln2_g[l], ln2_b[l], alpha)
    return h.reshape(batch, seq, D_MODEL)
```

```python
import functools
import math

import numpy as np
import jax
import jax.numpy as jnp
from jax import lax
from jax.experimental import pallas as pl
from jax.experimental.pallas import tpu as pltpu

D_MODEL = 1024
CHUNK = 64
FOX_HEADS = 8
FOX_HEAD_DIM = 64
MLA_HEADS = 8
MLA_NOPE_DIM = 64
MLA_ROPE_DIM = 32
MLA_V_DIM = 64
MLA_Q_RANK = 384
MLA_KV_RANK = 256
ROPE_BASE = 10000.0
MEM_HEADS = 4
MEM_HEAD_DIM = 128
N_BRANCHES = 3
BRANCH_WIDTH = 512
D_FF = 4 * D_MODEL
LN_EPS = 1e-5
RMS_EPS = 1e-6
NEG_INF = -1e30
LOG2E = math.log2(math.e)

LANES = 128
HALF = 64
N_GROUPS = 8
WIDE = N_GROUPS * LANES
MXU_WIDTH = 256
VMEM_LIMIT = 56 * 1024 * 1024

ROW_TILE = 512
ATTN_TILE = 512
SCORE_LEAD = 1
SUM_ROWS = 16

F32 = jnp.float32
BF16 = jnp.bfloat16


def _params(n_axes):
    return pltpu.CompilerParams(
        dimension_semantics=("arbitrary",) * n_axes, vmem_limit_bytes=VMEM_LIMIT)


def _dot(a, b):
    return jnp.dot(a, b, preferred_element_type=F32)


def _dot_nt(a, b):
    return lax.dot_general(a, b, (((1,), (1,)), ((), ())), preferred_element_type=F32)


def _layer_norm(z, g, b):
    mu = jnp.mean(z, axis=-1, keepdims=True)
    d = z - mu
    var = jnp.mean(d * d, axis=-1, keepdims=True)
    return d * lax.rsqrt(var + LN_EPS) * g + b


def _rms_norm(z, g):
    return z * lax.rsqrt(jnp.mean(z * z, axis=-1, keepdims=True) + RMS_EPS) * g


def _row_spec(tile, width, col_block=0):
    return pl.BlockSpec((tile, width), lambda i: (i, col_block))


def _full_spec(shape):
    return pl.BlockSpec(shape, lambda *_: (0,) * len(shape), pipeline_mode=pl.Buffered(1))


ROPE_HALF = MLA_ROPE_DIM // 2


def _rope_table_kernel(pos_ref, invf_ref, c_ref, slo_ref, shi_ref):
    ang = pos_ref[...].astype(F32) * invf_ref[...]
    lane = lax.broadcasted_iota(jnp.int32, ang.shape, 1)
    lo = (lane >= MLA_NOPE_DIM) & (lane < MLA_NOPE_DIM + ROPE_HALF)
    hi = (lane >= MLA_NOPE_DIM + ROPE_HALF) & (lane < MLA_NOPE_DIM + MLA_ROPE_DIM)
    sin = jnp.sin(ang)
    c_ref[...] = jnp.where(lane < MLA_NOPE_DIM, 1.0, jnp.where(lo | hi, jnp.cos(ang), 0.0))
    slo_ref[...] = jnp.where(lo, -sin, 0.0)
    shi_ref[...] = jnp.where(hi, sin, 0.0)


def _rotary(x, ct, s_lo, s_hi):
    width = x.shape[1]
    return (x * ct + pltpu.roll(x, width - ROPE_HALF, 1) * s_lo
            + pltpu.roll(x, ROPE_HALF, 1) * s_hi)


def _rope_tables(positions):
    t = positions.size
    inv_freq = ROPE_BASE ** (-jnp.arange(0, MLA_ROPE_DIM, 2, dtype=F32) / MLA_ROPE_DIM)
    invf = jnp.zeros((1, LANES), F32).at[0, MLA_NOPE_DIM:MLA_NOPE_DIM + MLA_ROPE_DIM].set(
        jnp.concatenate([inv_freq, inv_freq]))
    return pl.pallas_call(
        _rope_table_kernel,
        grid=(t // ROW_TILE,),
        in_specs=[_row_spec(ROW_TILE, 1), _full_spec((1, LANES))],
        out_specs=[_row_spec(ROW_TILE, LANES)] * 3,
        out_shape=[jax.ShapeDtypeStruct((t, LANES), F32)] * 3,
        compiler_params=_params(1),
        name="rope_tables",
    )(positions.reshape(t, 1), invf)


MEM_BATCH_GROUP = 4


def _mem_kv_kernel(mem_ref, w_ref, k_ref, v_ref):
    group, n_mem, _ = mem_ref.shape
    rows = mem_ref[...].reshape(group * n_mem, D_MODEL).astype(BF16)
    kv = _dot(rows, w_ref[0].astype(BF16))
    k = (kv[:, :BRANCH_WIDTH] * (MEM_HEAD_DIM ** -0.5 * LOG2E)).astype(BF16)
    k_ref[0] = k.reshape(group, n_mem, BRANCH_WIDTH)
    v_ref[0] = kv[:, BRANCH_WIDTH:].astype(BF16).reshape(group, n_mem, BRANCH_WIDTH)


def _mem_kv(mem, w_mem_kv):
    depth = w_mem_kv.shape[0]
    batch, n_mem, _ = mem.shape
    group = math.gcd(batch, MEM_BATCH_GROUP)
    return pl.pallas_call(
        _mem_kv_kernel,
        grid=(batch // group, depth),
        in_specs=[pl.BlockSpec((group, n_mem, D_MODEL), lambda b, l: (b, 0, 0)),
                  pl.BlockSpec((1, D_MODEL, 2 * BRANCH_WIDTH), lambda b, l: (l, 0, 0))],
        out_specs=[pl.BlockSpec((1, group, n_mem, BRANCH_WIDTH), lambda b, l: (l, b, 0, 0))] * 2,
        out_shape=[jax.ShapeDtypeStruct((depth, batch, n_mem, BRANCH_WIDTH), BF16)] * 2,
        compiler_params=_params(2),
        name="mem_kv",
    )(mem, w_mem_kv)


GATE_W = N_BRANCHES * D_MODEL
COL_FOX_Q = 0
COL_FOX_K = COL_FOX_Q + BRANCH_WIDTH
COL_FOX_V = COL_FOX_K + BRANCH_WIDTH
COL_CQ = COL_FOX_V + BRANCH_WIDTH
COL_CKV = COL_CQ + MLA_Q_RANK
COL_QMEM = COL_CKV + MLA_KV_RANK
COL_GATE = COL_QMEM + BRANCH_WIDTH
MAIN_W = COL_GATE + GATE_W
GATE_CHUNK = 256
N_SPLIT = 3
FORGET_LANE = MLA_NOPE_DIM + MLA_ROPE_DIM


def _split3(x):
    hi = x.astype(BF16)
    r = x - hi.astype(F32)
    mid = r.astype(BF16)
    lo = (r - mid.astype(F32)).astype(BF16)
    return hi, mid, lo


def _mixer_inputs_kernel(*refs, entry_norm, tiles_per_seq):
    if entry_norm:
        x_ref, lng_ref, lnb_ref, *refs = refs
    else:
        x_ref, *refs = refs
    (wfox_ref, wmla_ref, wtail_ref, ws_ref, bf_ref, pq_ref, pk_ref, oq_ref, ok_ref,
     ct_ref, slo_ref, shi_ref, gq_ref, gkv_ref, wq_ref, wk_ref, wvt_ref, mk_ref, mv_ref,
     gates_ref, oc_ref, fqt_ref, fka_ref, fvt_ref, mqt_ref, mka_ref, mvt_ref, *rest) = refs
    carry_ref = rest[-1]
    tm = x_ref.shape[0]

    @pl.when(pl.program_id(0) % tiles_per_seq == 0)
    def _():
        carry_ref[...] = jnp.zeros_like(carry_ref)

    h = x_ref[...]
    if entry_norm:
        h = _layer_norm(h, lng_ref[...], lnb_ref[...])
        rest[0][...] = h
    xb = h.astype(BF16)

    def mm(c0, n):
        for w_ref, first in ((wtail_ref, COL_QMEM), (wmla_ref, COL_CQ), (wfox_ref, COL_FOX_Q)):
            if c0 >= first:
                return _dot(xb, w_ref[:, c0 - first:c0 - first + n])

    gate_cols = iter(range(0, GATE_W, GATE_CHUNK))

    def emit_gates(n_chunks=2):
        for _ in range(n_chunks):
            c = next(gate_cols, None)
            if c is not None:
                g = mm(COL_GATE + c, GATE_CHUNK)
                gates_ref[:, c:c + GATE_CHUNK] = (0.5 * jnp.tanh(0.5 * g) + 0.5).astype(BF16)

    small = _dot(xb, ws_ref[...])
    xq = _rms_norm(mm(COL_CQ, MLA_Q_RANK), gq_ref[...]).astype(BF16)
    xkv = _rms_norm(mm(COL_CKV, MLA_KV_RANK), gkv_ref[...]).astype(BF16)

    x = small + bf_ref[...]
    logf = jnp.minimum(x, 0.0) - jnp.log1p(jnp.exp(-jnp.abs(x)))
    row = lax.broadcasted_iota(jnp.int32, (tm, tm), 0)
    col = lax.broadcasted_iota(jnp.int32, (tm, tm), 1)
    tri = jnp.where(col <= row, 1.0, 0.0).astype(BF16)
    hi, mid, lo = _split3(logf)
    cum = _dot(tri, hi) + _dot(tri, mid) + _dot(tri, lo) + carry_ref[...]
    carry_ref[...] = cum[tm - 1:tm, :]
    emit_gates()

    lane = lax.broadcasted_iota(jnp.int32, (tm, LANES), 1)
    span = lane - FORGET_LANE
    chi, cmid, clo = _split3(cum * LOG2E)
    pieces = jnp.where((span >= 0) & (span < FOX_HEADS), chi,
                       jnp.where((span >= FOX_HEADS) & (span < 2 * FOX_HEADS), cmid,
                                 jnp.where((span >= 2 * FOX_HEADS) & (span < 3 * FOX_HEADS), clo,
                                           jnp.zeros_like(chi))))
    left = lane < HALF
    heads_per_dot = MXU_WIDTH // HALF
    for h0 in range(0, FOX_HEADS, heads_per_dot):
        c0 = h0 * HALF
        qp2 = mm(COL_FOX_Q + c0, MXU_WIDTH) * (FOX_HEAD_DIM ** -0.5 * LOG2E)
        kp2 = mm(COL_FOX_K + c0, MXU_WIDTH)
        grp4 = slice(h0 * LANES, (h0 + heads_per_dot) * LANES)
        q_spare = _dot(pieces, pq_ref[:, grp4]) + oq_ref[:, grp4]
        k_spare = _dot(pieces, pk_ref[:, grp4]) + ok_ref[:, grp4]
        for i in range(heads_per_dot):
            head = h0 + i
            pair = slice((i // 2) * LANES, (i // 2 + 1) * LANES)
            loc = slice(i * LANES, (i + 1) * LANES)
            grp = slice(head * LANES, (head + 1) * LANES)
            own = left if head % 2 == 0 else jnp.logical_not(left)
            fqt_ref[0, grp, :] = jnp.where(own, qp2[:, pair], q_spare[:, loc]).T.astype(BF16)
            fka_ref[:, grp] = jnp.where(own, kp2[:, pair], k_spare[:, loc]).astype(BF16)
        emit_gates()
    fvt_ref[0] = mm(COL_FOX_V, BRANCH_WIDTH).T.astype(BF16)

    qm = mm(COL_QMEM, BRANCH_WIDTH).astype(BF16)
    for hm in range(MEM_HEADS):
        grp = slice(hm * MEM_HEAD_DIM, (hm + 1) * MEM_HEAD_DIM)
        s = _dot_nt(qm[:, grp], mk_ref[0, 0, :, grp])
        e = jnp.exp2(s - jnp.max(s, axis=1, keepdims=True))
        o = _dot(e.astype(BF16), mv_ref[0, 0, :, grp])
        oc_ref[:, grp] = (o * (1.0 / jnp.sum(e, axis=1, keepdims=True))).astype(BF16)
    emit_gates()

    ct, s_lo, s_hi = ct_ref[...], slo_ref[...], shi_ref[...]
    groups_per_dot = MXU_WIDTH // LANES
    ct2, s_lo2, s_hi2 = (jnp.concatenate([tab] * groups_per_dot, axis=1) for tab in (ct, s_lo, s_hi))
    scale = (MLA_NOPE_DIM + MLA_ROPE_DIM) ** -0.5 * LOG2E
    kpe = _rotary(small, ct, s_lo, s_hi)
    kpe2 = jnp.concatenate([kpe] * groups_per_dot, axis=1)
    for n, c in enumerate(range(0, WIDE, MXU_WIDTH)):
        q = _rotary(_dot(xq, wq_ref[:, c:c + MXU_WIDTH]), ct2, s_lo2, s_hi2) * scale
        mqt_ref[0, c:c + MXU_WIDTH, :] = q.T.astype(BF16)
        mka_ref[:, c:c + MXU_WIDTH] = (_dot(xkv, wk_ref[:, c:c + MXU_WIDTH]) + kpe2).astype(BF16)
        if n == 1:
            emit_gates(3)
    mvt_ref[0] = _dot_nt(wvt_ref[...], xkv).astype(BF16)
    emit_gates(GATE_W // GATE_CHUNK)


def _fox_placement():
    pq = np.zeros((LANES, WIDE), np.float32)
    pk = np.zeros((LANES, WIDE), np.float32)
    oq = np.zeros((1, WIDE), np.float32)
    ok = np.zeros((1, WIDE), np.float32)
    for h in range(FOX_HEADS):
        base = h * LANES + (HALF if h % 2 == 0 else 0)
        for j in range(N_SPLIT):
            src = FORGET_LANE + j * FOX_HEADS + h
            pq[src, base + j] = 1.0
            ok[0, base + j] = 1.0
            oq[0, base + N_SPLIT + j] = 1.0
            pk[src, base + N_SPLIT + j] = -1.0
    return (jnp.asarray(pq, BF16), jnp.asarray(pk, BF16), jnp.asarray(oq), jnp.asarray(ok))


def _mixer_inputs(h, ln_in, w_main, w_small, b_forget3, rope_tabs, g_cq, g_ckv, wq, wk, wvt,
                  mk, mv, layer, batch, seq):
    t = h.shape[0]
    nt = seq // ROW_TILE
    n_mem = mk.shape[2]
    entry_norm = ln_in is not None
    pq, pk, oq, ok = _fox_placement()
    mem_spec = pl.BlockSpec((1, 1, n_mem, BRANCH_WIDTH), lambda i: (layer, i // nt, 0, 0))

    def transposed(rows):
        return (pl.BlockSpec((1, rows, ROW_TILE), lambda i: (i // nt, 0, i % nt)),
                jax.ShapeDtypeStruct((batch, rows, seq), BF16))

    def row_major(width):
        return _row_spec(ROW_TILE, width), jax.ShapeDtypeStruct((t, width), BF16)

    operands = [h]
    in_specs = [_row_spec(ROW_TILE, D_MODEL)]
    if entry_norm:
        operands += [ln_in[0].reshape(1, -1), ln_in[1].reshape(1, -1)]
        in_specs += [_full_spec((1, D_MODEL))] * 2
    operands += [*w_main, w_small, b_forget3, pq, pk, oq, ok, *rope_tabs,
                 g_cq.reshape(1, -1), g_ckv.reshape(1, -1), wq, wk, wvt, mk, mv]
    in_specs += [_full_spec((D_MODEL, COL_CQ - COL_FOX_Q)), _full_spec((D_MODEL, COL_QMEM - COL_CQ)),
                 _full_spec((D_MODEL, MAIN_W - COL_QMEM)), _full_spec((D_MODEL, LANES)),
                 _full_spec((1, LANES)),
                 _full_spec((LANES, WIDE)), _full_spec((LANES, WIDE)), _full_spec((1, WIDE)),
                 _full_spec((1, WIDE))] + [_row_spec(ROW_TILE, LANES)] * 3 + [
                 _full_spec((1, MLA_Q_RANK)), _full_spec((1, MLA_KV_RANK)),
                 _full_spec((MLA_Q_RANK, WIDE)), _full_spec((MLA_KV_RANK, WIDE)),
                 _full_spec((BRANCH_WIDTH, MLA_KV_RANK)), mem_spec, mem_spec]

    outs = [row_major(GATE_W), row_major(BRANCH_WIDTH),
            transposed(WIDE), row_major(WIDE), transposed(BRANCH_WIDTH),
            transposed(WIDE), row_major(WIDE), transposed(BRANCH_WIDTH)]
    out_specs = [spec for spec, _ in outs]
    out_shape = [shape for _, shape in outs]
    if entry_norm:
        out_specs.append(_row_spec(ROW_TILE, D_MODEL))
        out_shape.append(jax.ShapeDtypeStruct((t, D_MODEL), F32))
    return pl.pallas_call(
        functools.partial(_mixer_inputs_kernel, entry_norm=entry_norm, tiles_per_seq=nt),
        grid=(t // ROW_TILE,),
        in_specs=in_specs,
        out_specs=out_specs,
        out_shape=out_shape,
        scratch_shapes=[pltpu.VMEM((1, LANES), F32)],
        compiler_params=_params(1),
        name="mixer_inputs",
    )(*operands)


def _attn_kernel(qt_ref, k_ref, vt_ref, o_ref, sa_ref, sb_ref, ta_ref, tb_ref, m_ref, acc_ref,
                 *maybe_bias_ref, granule_shift):
    tq = tk = sa_ref.shape[1]
    n_q_tiles = qt_ref.shape[2] // tq
    heads = range(N_GROUPS)
    buf_a = (sa_ref, ta_ref)
    buf_b = (sb_ref, tb_ref)
    key = lax.broadcasted_iota(jnp.int32, (tk, tq), 0)
    qry = lax.broadcasted_iota(jnp.int32, (tk, tq), 1)
    if granule_shift:
        (bias_ref,) = maybe_bias_ref
        bias_ref[...] = jnp.where((key >> granule_shift) <= (qry >> granule_shift), 0.0, NEG_INF)
    ones_rows = jnp.ones((SUM_ROWS, tk), BF16)

    def start_of(tile_index, size):
        start = tile_index * size
        return start if isinstance(start, int) else pl.multiple_of(start, size)

    def scores(qi, j, buf, h, masked):
        q0 = start_of(qi, tq)
        ks = start_of(j, tk)
        grp = slice(h * LANES, (h + 1) * LANES)
        s = _dot(k_ref[0, pl.ds(ks, tk), grp], qt_ref[0, grp, pl.ds(q0, tq)])
        if masked and granule_shift:
            s = s + bias_ref[...]
        elif masked:
            s = jnp.where(key <= qry, s, NEG_INF)
        buf[0][h] = s
        buf[1][h] = jnp.max(s, axis=0, keepdims=True)

    def update(j, buf, h):
        ks = start_of(j, tk)
        m = m_ref[h]
        m_new = jnp.maximum(m, buf[1][h])
        alpha = jnp.exp2(m - m_new)
        e = jnp.exp2(buf[0][h] - m_new)
        m_ref[h] = m_new
        v_t = jnp.concatenate([vt_ref[0, h * HALF:(h + 1) * HALF, pl.ds(ks, tk)], ones_rows],
                              axis=0)
        acc_ref[h] = alpha * acc_ref[h] + _dot(v_t, e.astype(BF16))

    def tile_step(qi, j, cur, nxt=None, next_masked=False):
        if nxt is not None:
            for h in range(SCORE_LEAD):
                scores(qi, j + 1, nxt, h, next_masked)
        for h in heads:
            update(j, cur, h)
            if nxt is not None and h + SCORE_LEAD < N_GROUPS:
                scores(qi, j + 1, nxt, h + SCORE_LEAD, next_masked)

    def start_tile():
        m_ref[...] = jnp.full(m_ref.shape, NEG_INF, F32)
        acc_ref[...] = jnp.zeros(acc_ref.shape, F32)

    def finish_tile(qi, next_qi=None):
        if next_qi is not None:
            for h in heads:
                scores(next_qi, 0, buf_a, h, False)
        q0 = start_of(qi, tq)
        for pair in range(N_GROUPS // 2):
            halves = []
            for h in (2 * pair, 2 * pair + 1):
                halves.append(acc_ref[h, :HALF, :] * (1.0 / acc_ref[h, HALF:HALF + 1, :]))
            o_t = jnp.concatenate(halves, axis=0)
            o_ref[0, pl.ds(q0, tq), pair * LANES:(pair + 1) * LANES] = o_t.T.astype(BF16)

    start_tile()
    for h in heads:
        scores(0, 0, buf_a, h, True)
    tile_step(0, 0, buf_a)
    finish_tile(0, 1 if n_q_tiles > 1 else None)

    def q_tile(qi, _):
        start_tile()

        def two_tiles(i, _):
            tile_step(qi, 2 * i, buf_a, buf_b)
            tile_step(qi, 2 * i + 1, buf_b, buf_a)
            return 0

        n_pairs = (qi - 1) // 2
        lax.fori_loop(0, n_pairs, two_tiles, 0)
        j = 2 * n_pairs

        @pl.when(qi % 2 == 1)
        def _():
            tile_step(qi, j, buf_a, buf_b, next_masked=True)
            tile_step(qi, j + 1, buf_b)

        @pl.when(qi % 2 == 0)
        def _():
            tile_step(qi, j, buf_a, buf_b)
            tile_step(qi, j + 1, buf_b, buf_a, next_masked=True)
            tile_step(qi, j + 2, buf_a)

        @pl.when(qi < n_q_tiles - 1)
        def _():
            finish_tile(qi, qi + 1)

        @pl.when(qi == n_q_tiles - 1)
        def _():
            finish_tile(qi)

        return 0

    lax.fori_loop(1, n_q_tiles, q_tile, 0)


def _causal_attention(q_t, k_aug, v_t, batch, seq, granule):
    tile = ATTN_TILE
    k3 = k_aug.reshape(batch, seq, WIDE)
    out = pl.pallas_call(
        functools.partial(_attn_kernel, granule_shift=granule.bit_length() - 1),
        grid=(batch,),
        in_specs=[pl.BlockSpec((1, WIDE, seq), lambda b: (b, 0, 0)),
                  pl.BlockSpec((1, seq, WIDE), lambda b: (b, 0, 0)),
                  pl.BlockSpec((1, BRANCH_WIDTH, seq), lambda b: (b, 0, 0))],
        out_specs=pl.BlockSpec((1, seq, BRANCH_WIDTH), lambda b: (b, 0, 0)),
        out_shape=jax.ShapeDtypeStruct((batch, seq, BRANCH_WIDTH), BF16),
        scratch_shapes=[pltpu.VMEM((N_GROUPS, tile, tile), F32),
                        pltpu.VMEM((N_GROUPS, tile, tile), F32),
                        pltpu.VMEM((N_GROUPS, 1, tile), F32),
                        pltpu.VMEM((N_GROUPS, 1, tile), F32),
                        pltpu.VMEM((N_GROUPS, 1, tile), F32),
                        pltpu.VMEM((N_GROUPS, HALF + SUM_ROWS, tile), F32)]
        + ([pltpu.VMEM((tile, tile), F32)] if granule > 1 else []),
        compiler_params=_params(1),
        name="causal_attention_g%d" % granule,
    )(q_t, k3, v_t)
    return out.reshape(batch * seq, BRANCH_WIDTH)


FF_CHUNK = 1024
ROW_PARTS = 2


def _merge_ffn_kernel(oa_ref, ob_ref, oc_ref, gates_ref, h_ref, wbr_ref, wout_ref, g1_ref, b1_ref,
                      w1_ref, w2_ref, g2_ref, b2_ref, o_ref, *, alpha):
    n_parts = ROW_PARTS
    part = o_ref.shape[0] // n_parts
    rows = [slice(p * part, (p + 1) * part) for p in range(n_parts)]

    def merge(r):
        merged = None
        for n, br_ref in enumerate((oa_ref, ob_ref, oc_ref)):
            gate = gates_ref[r, n * D_MODEL:(n + 1) * D_MODEL].astype(F32)
            term = gate * _dot(br_ref[r, :], wbr_ref[n])
            merged = term if merged is None else merged + term
        y = _dot(merged.astype(BF16), wout_ref[...])
        return _layer_norm(alpha * h_ref[r, :] + y, g1_ref[...], b1_ref[...])

    def mlp(h1):
        xb = h1.astype(BF16)
        acc = alpha * h1
        for c in range(0, D_FF, FF_CHUNK):
            u = jnp.maximum(_dot(xb, w1_ref[:, c:c + FF_CHUNK]), 0.0)
            acc = acc + _dot((u * u).astype(BF16), w2_ref[c:c + FF_CHUNK, :])
        return _layer_norm(acc, g2_ref[...], b2_ref[...])

    h1 = [merge(r) for r in rows]
    for r, h1_part in zip(rows, h1):
        o_ref[r, :] = mlp(h1_part)


def _merge_ffn(oa, ob, oc, gates, h, w_br, w_out, g1, b1, w1, w2, g2, b2, alpha):
    t = h.shape[0]
    br = _row_spec(ROW_TILE, BRANCH_WIDTH)
    vec = _full_spec((1, D_MODEL))
    return pl.pallas_call(
        functools.partial(_merge_ffn_kernel, alpha=alpha),
        grid=(t // ROW_TILE,),
        in_specs=[br, br, br, _row_spec(ROW_TILE, GATE_W), _row_spec(ROW_TILE, D_MODEL),
                  _full_spec((N_BRANCHES, BRANCH_WIDTH, D_MODEL)), _full_spec((D_MODEL, D_MODEL)),
                  vec, vec, _full_spec((D_MODEL, D_FF)), _full_spec((D_FF, D_MODEL)), vec, vec],
        out_specs=_row_spec(ROW_TILE, D_MODEL),
        out_shape=jax.ShapeDtypeStruct((t, D_MODEL), F32),
        compiler_params=_params(1),
        name="merge_ffn",
    )(oa, ob, oc, gates, h, w_br, w_out, g1.reshape(1, -1), b1.reshape(1, -1), w1, w2,
      g2.reshape(1, -1), b2.reshape(1, -1))


def _layer_weights(w_in, b_forget, w_uq, w_ukv):
    depth = w_in.shape[0]
    fox_w = 3 * BRANCH_WIDTH
    tail = LANES - FORGET_LANE - N_SPLIT * FOX_HEADS
    f0 = fox_w
    f1 = f0 + FOX_HEADS
    r0 = f1 + MLA_Q_RANK + MLA_KV_RANK
    r1 = r0 + MLA_ROPE_DIM
    f_log, k_rope = w_in[..., f0:f1], w_in[..., r0:r1]
    w_main = (w_in[..., :f0].astype(BF16), w_in[..., f1:r0].astype(BF16), w_in[..., r1:].astype(BF16))
    w_small = jnp.pad(jnp.concatenate([k_rope] + [f_log] * N_SPLIT, axis=-1).astype(BF16),
                      ((0, 0), (0, 0), (MLA_NOPE_DIM, tail)))
    b3 = jnp.pad(jnp.concatenate([b_forget] * N_SPLIT, axis=-1),
                 ((0, 0), (FORGET_LANE, tail))).reshape(depth, 1, LANES)

    dq = MLA_NOPE_DIM + MLA_ROPE_DIM
    wq = w_uq.astype(BF16).reshape(depth, MLA_Q_RANK, MLA_HEADS, dq)
    wq = jnp.pad(wq, ((0, 0), (0, 0), (0, 0), (0, LANES - dq))).reshape(depth, MLA_Q_RANK, -1)
    wkv = w_ukv.astype(BF16).reshape(depth, MLA_KV_RANK, MLA_HEADS, MLA_NOPE_DIM + MLA_V_DIM)
    wk = jnp.pad(wkv[..., :MLA_NOPE_DIM], ((0, 0), (0, 0), (0, 0), (0, LANES - MLA_NOPE_DIM)))
    wk = wk.reshape(depth, MLA_KV_RANK, -1)
    wv_t = jnp.swapaxes(wkv[..., MLA_NOPE_DIM:].reshape(depth, MLA_KV_RANK, -1), 1, 2)
    return w_main, w_small, b3, wq, wk, wv_t


def kernel(x, mem, positions, ln_in_g, ln_in_b, w_in, b_forget, w_uq, g_cq, w_ukv, g_ckv,
           w_mem_kv, w_br, w_out, ln1_g, ln1_b, w_ff1, w_ff2, ln2_g, ln2_b):
    batch, seq, _ = x.shape
    depth = w_in.shape[0]
    assert seq % ATTN_TILE == 0 and seq % ROW_TILE == 0
    alpha = float((2 * depth) ** 0.25)

    w_main, w_small, b3, wq, wk, wv_t = _layer_weights(w_in, b_forget, w_uq, w_ukv)
    w_br_b = w_br.astype(BF16)
    w_out_b = w_out.astype(BF16)
    w_ff1_b = w_ff1.astype(BF16)
    w_ff2_b = w_ff2.astype(BF16)

    rope_tabs = _rope_tables(positions)
    mk, mv = _mem_kv(mem, w_mem_kv)
    h = x.reshape(batch * seq, D_MODEL)
    for l in range(depth):
        outs = _mixer_inputs(h, (ln_in_g, ln_in_b) if l == 0 else None,
                             tuple(w[l] for w in w_main), w_small[l], b3[l],
                             rope_tabs, g_cq[l], g_ckv[l], wq[l], wk[l], wv_t[l], mk, mv, l,
                             batch, seq)
        gates, o_c, fq_t, fk_aug, fv_t, mq_t, mk_aug, mv_t = outs[:8]
        if l == 0:
            h = outs[8]
        o_a = _causal_attention(fq_t, fk_aug, fv_t, batch, seq, 1)
        o_b = _causal_attention(mq_t, mk_aug, mv_t, batch, seq, CHUNK)
        h = _merge_ffn(o_a, o_b, o_c, gates, h, w_br_b[l], w_out_b[l], ln1_g[l], ln1_b[l],
                       w_ff1_b[l], w_ff2_b[l], ln2_g[l], ln2_b[l], alpha)
    return h.reshape(batch, seq, D_MODEL)
```
